```python
import math
import jax, jax.numpy as jnp
from jax import lax
import numpy as np

D_MODEL = 2048
BATCH = 2
SEQ = 4096
DEPTH = 4
DEC_BATCH = 32
DEC_SEQ = 1
PAST_LEN = 16384
PAGE_SIZE = 128

N_BRANCH = 4
BRANCH_W = D_MODEL // N_BRANCH
ATT_HD = 64
ATT_HQ = BRANCH_W // ATT_HD
ATT_HKV = 2
ATT_G = ATT_HQ // ATT_HKV
WINDOW = 128
ROPE_THETA = 10000.0
SSM_GROUP = 16
SSM_NG = BRANCH_W // SSM_GROUP
SSM_P = 64
CHUNK = 128
SG_GROUPS = 4
SG_GW = BRANCH_W // SG_GROUPS
DN_HD = 128
DN_H = BRANCH_W // DN_HD
DN_CONV = 4
DN_CHUNK = 64
D_FF = 4 * D_MODEL
EPS = 1e-6
NEG_INF = -1e30
IN_SIZES = (ATT_HQ * ATT_HD, ATT_HKV * ATT_HD, ATT_HKV * ATT_HD, BRANCH_W, 2 * BRANCH_W, 3 * BRANCH_W, BRANCH_W, DN_H, DN_H, N_BRANCH * D_MODEL)
N_IN = sum(IN_SIZES)

kernel_name = 'hybrid_gated_parallel_decoder_step'


def _split(z, sizes):
    idx = np.cumsum(np.array(sizes))[:-1].tolist()
    return jnp.split(z, idx, axis=-1)


def _rmsnorm(x, g):
    xf = x.astype(jnp.float32)
    y = xf * lax.rsqrt(jnp.mean(xf * xf, axis=-1, keepdims=True) + EPS)
    return (y * g.astype(jnp.float32)).astype(x.dtype)


def _l2norm(x):
    return x * lax.rsqrt(jnp.sum(x * x, axis=-1, keepdims=True) + EPS)


def _rope(x, pos):
    half = ATT_HD // 2
    inv = ROPE_THETA ** (-jnp.arange(half, dtype=jnp.float32) / half)
    ang = pos.astype(jnp.float32)[:, None] * inv[None, :]
    cos = jnp.cos(ang)[None, :, None, :]
    sin = jnp.sin(ang)[None, :, None, :]
    xf = x.astype(jnp.float32)
    x1, x2 = xf[..., :half], xf[..., half:]
    return jnp.concatenate([x1 * cos - x2 * sin, x2 * cos + x1 * sin], axis=-1).astype(x.dtype)


def _sink_probs(s, sink):
    sk = jnp.broadcast_to(sink.astype(jnp.float32)[:, :, None, None], s.shape[:-1] + (1,))
    return jax.nn.softmax(jnp.concatenate([s, sk], axis=-1), axis=-1)[..., :-1]


def _attn_prompt(q, k, v, sink):
    B, L = q.shape[0], q.shape[1]
    nb = L // WINDOW
    scale = ATT_HD ** -0.5
    qb = q.reshape(B, nb, WINDOW, ATT_HKV, ATT_G, ATT_HD).astype(jnp.float32)
    pad = jnp.zeros((B, WINDOW, ATT_HKV, ATT_HD), k.dtype)
    kp = jnp.concatenate([pad, k], axis=1).reshape(B, nb + 1, WINDOW, ATT_HKV, ATT_HD)
    vp = jnp.concatenate([pad.astype(v.dtype), v], axis=1).reshape(B, nb + 1, WINDOW, ATT_HKV, ATT_HD)
    kb = jnp.concatenate([kp[:, :-1], kp[:, 1:]], axis=2)
    vb = jnp.concatenate([vp[:, :-1], vp[:, 1:]], axis=2)
    s = jnp.einsum('bnqkgd,bnskd->bnkgqs', qb, kb.astype(jnp.float32)) * scale
    blk = jnp.arange(nb)[:, None] * WINDOW
    qpos = blk + jnp.arange(WINDOW)[None, :]
    kpos = blk - WINDOW + jnp.arange(2 * WINDOW)[None, :]
    rel = qpos[:, :, None] - kpos[:, None, :]
    mask = (rel >= 0) & (rel < WINDOW) & (kpos[:, None, :] >= 0)
    s = jnp.where(mask[None, :, None, None], s, NEG_INF)
    p = _sink_probs(s, sink.reshape(ATT_HKV, ATT_G))
    o = jnp.einsum('bnkgqs,bnskd->bnqkgd', p.astype(vb.dtype), vb)
    return o.reshape(B, L, ATT_HQ * ATT_HD)


def _attn_sample(q, k, v, ck, cv, sink, pos0):
    B, L = q.shape[0], q.shape[1]
    lc = ck.shape[1]
    scale = ATT_HD ** -0.5
    K = jnp.concatenate([ck, k.astype(ck.dtype)], axis=1)
    V = jnp.concatenate([cv, v.astype(cv.dtype)], axis=1)
    qpos = pos0 + jnp.arange(L)
    kpos = jnp.concatenate([pos0 - lc + jnp.arange(lc), qpos])
    rel = qpos[:, None] - kpos[None, :]
    mask = (rel >= 0) & (rel < WINDOW)
    qg = q.reshape(B, L, ATT_HKV, ATT_G, ATT_HD).astype(jnp.float32)
    s = jnp.einsum('bqkgd,bskd->bkgqs', qg, K.astype(jnp.float32)) * scale
    s = jnp.where(mask[None, None, None], s, NEG_INF)
    p = _sink_probs(s, sink.reshape(ATT_HKV, ATT_G))
    o = jnp.einsum('bkgqs,bskd->bqkgd', p.astype(V.dtype), V)
    return o.reshape(B, L, ATT_HQ * ATT_HD), K[:, -lc:], V[:, -lc:]


def _attn_branch(zq, zk, zv, qn_g, kn_g, sink, pos0, cache_kv):
    B, L, _ = zq.shape
    pos = pos0 + jnp.arange(L)
    q = _rope(_rmsnorm(zq.reshape(B, L, ATT_HQ, ATT_HD), qn_g), pos)
    k = _rope(_rmsnorm(zk.reshape(B, L, ATT_HKV, ATT_HD), kn_g), pos)
    v = zv.reshape(B, L, ATT_HKV, ATT_HD)
    if cache_kv is None:
        keep = min(WINDOW, L)
        return _attn_prompt(q, k, v, sink), k[:, L - keep:], v[:, L - keep:]
    ck, cv = cache_kv
    return _attn_sample(q, k, v, ck, cv, sink, pos0)


def _lin_combine(e1, e2):
    a1, b1 = e1
    a2, b2 = e2
    return (a2 * a1, a2 * b1 + b2)


def _ssm_branch(zu, s0_re, s0_im, lam_re, lam_im, log_dt, b_re, b_im, c_re, c_im, d_skip, glu_w, glu_b):
    f32 = jnp.float32
    B, L, _ = zu.shape
    u = zu.astype(f32).reshape(B, L, SSM_NG, SSM_GROUP)
    lam = lax.complex(lam_re.astype(f32), lam_im.astype(f32))
    a_bar = jnp.exp(lam * jnp.exp(log_dt.astype(f32))[:, None])
    b_bar = ((a_bar - 1.0) / lam)[..., None] * lax.complex(b_re.astype(f32), b_im.astype(f32))
    c = lax.complex(c_re.astype(f32), c_im.astype(f32))
    bu = jnp.einsum('gpc,blgc->blgp', b_bar, u.astype(jnp.complex64))
    a = jnp.broadcast_to(a_bar, bu.shape)
    a_cum, s_zero = lax.associative_scan(_lin_combine, (a, bu), axis=1)
    s = a_cum * lax.complex(s0_re.astype(f32), s0_im.astype(f32))[:, None] + s_zero
    y = jnp.real(jnp.einsum('gcp,blgp->blgc', c, s)) + d_skip.astype(f32).reshape(SSM_NG, SSM_GROUP) * u
    y = jax.nn.gelu(y.reshape(B, L, BRANCH_W))
    y = y * jax.nn.sigmoid(y @ glu_w.astype(f32) + glu_b.astype(f32))
    s_last = s[:, -1]
    return y, jnp.real(s_last), jnp.imag(s_last)


def _spatial_gate_branch(zc, norm_g, w_s, b_s):
    B, L, _ = zc.shape
    uv = jax.nn.gelu(zc)
    u, v = uv[..., :BRANCH_W], uv[..., BRANCH_W:]
    v = _rmsnorm(v, norm_g)
    n_ch = -(-L // CHUNK)
    vp = jnp.pad(v, ((0, 0), (0, n_ch * CHUNK - L), (0, 0))).reshape(B, n_ch, CHUNK, SG_GROUPS, SG_GW)
    mixed = jnp.einsum('gts,bnsgh->bntgh', jnp.tril(w_s), vp) + jnp.transpose(b_s)[None, None, :, :, None]
    mixed = mixed.reshape(B, n_ch * CHUNK, BRANCH_W)[:, :L]
    start = ((L - 1) // CHUNK) * CHUNK
    return u * mixed, v[:, start:]


def _short_conv(x, buf, w):
    L = x.shape[1]
    xin = jnp.concatenate([buf.astype(x.dtype), x], axis=1)
    y = w[0] * xin[:, :L]
    for i in range(1, DN_CONV):
        y = y + w[i] * xin[:, i:i + L]
    return jax.nn.silu(y), xin[:, L:]


def _gated_delta(q, k, v, g, beta, S0):
    B, L, H, dk = q.shape
    dv = v.shape[-1]
    C = min(DN_CHUNK, L)
    Lp = -(-L // C) * C
    N = Lp // C

    def prep(t):
        t = jnp.pad(t, [(0, 0), (0, Lp - L)] + [(0, 0)] * (t.ndim - 2))
        t = t.reshape((B, N, C) + t.shape[2:])
        return jnp.moveaxis(t, (1, 2), (0, 3))

    q, k, v, g, beta = prep(q), prep(k), prep(v), prep(g), prep(beta)
    gc = jnp.cumsum(g, axis=-1)
    idx = jnp.arange(C)
    tril = idx[:, None] >= idx[None, :]
    stril = idx[:, None] > idx[None, :]
    diff = gc[..., :, None] - gc[..., None, :]
    dec = jnp.where(tril, jnp.exp(jnp.where(tril, diff, 0.0)), 0.0)
    kk = jnp.einsum('nbhtd,nbhsd->nbhts', k, k)
    a_mat = jnp.where(stril, beta[..., :, None] * kk * dec, 0.0) + jnp.eye(C, dtype=jnp.float32)
    eg = jnp.exp(gc)
    rhs = jnp.concatenate([beta[..., None] * v, (beta * eg)[..., None] * k], axis=-1)
    sol = lax.linalg.triangular_solve(a_mat, rhs, left_side=True, lower=True, unit_diagonal=True)
    u_t, w_k = sol[..., :dv], sol[..., dv:]
    qk = jnp.einsum('nbhtd,nbhsd->nbhts', q, k) * dec
    q_g = q * eg[..., None]
    k_end = k * jnp.exp(gc[..., -1:] - gc)[..., None]
    g_end = eg[..., -1]

    def step(S, xs):
        u_c, w_c, qk_c, qg_c, ke_c, ge_c = xs
        U = u_c - jnp.einsum('bhck,bhkv->bhcv', w_c, S)
        O = jnp.einsum('bhck,bhkv->bhcv', qg_c, S) + jnp.einsum('bhts,bhsv->bhtv', qk_c, U)
        S = ge_c[..., None, None] * S + jnp.einsum('bhck,bhcv->bhkv', ke_c, U)
        return S, O

    S, O = lax.scan(step, S0, (u_t, w_k, qk, q_g, k_end, g_end))
    O = jnp.moveaxis(O, (0, 3), (1, 2)).reshape(B, Lp, H, dv)[:, :L]
    return O, S


def _delta_branch(zqkv, zgate, za, zb, S0, conv0, conv_w, a_log, dt_bias, norm_g):
    f32 = jnp.float32
    B, L, _ = zqkv.shape
    qkv, new_conv = _short_conv(zqkv, conv0, conv_w)
    qkv = qkv.astype(f32).reshape(B, L, 3, DN_H, DN_HD)
    q = _l2norm(qkv[:, :, 0]) * (DN_HD ** -0.5)
    k = _l2norm(qkv[:, :, 1])
    v = qkv[:, :, 2]
    g = -jnp.exp(a_log.astype(f32)) * jax.nn.softplus(za.astype(f32) + dt_bias.astype(f32))
    beta = jax.nn.sigmoid(zb.astype(f32))
    o, S = _gated_delta(q, k, v, g, beta, S0.astype(f32))
    o = _rmsnorm(o, norm_g) * jax.nn.silu(zgate.astype(f32).reshape(B, L, DN_H, DN_HD))
    return o.reshape(B, L, BRANCH_W), S, new_conv


def _layer(x, pos0, state, prm, prompt):
    B, L, _ = x.shape
    h = _rmsnorm(x, prm['norm1_g'])
    z = h @ prm['w_in']
    zq, zk, zv, zu, zc, zdqkv, zdg, zda, zdb, zgate = _split(z, IN_SIZES)
    if prompt:
        cache_kv = None
        s_re = jnp.zeros((B, SSM_NG, SSM_P), jnp.float32)
        s_im = jnp.zeros((B, SSM_NG, SSM_P), jnp.float32)
        S0 = jnp.zeros((B, DN_H, DN_HD, DN_HD), jnp.float32)
        conv0 = jnp.zeros((B, DN_CONV - 1, 3 * BRANCH_W), x.dtype)
    else:
        ck, cv, s_re, s_im, S0, conv0 = state
        cache_kv = (ck, cv)
    ya, nk, nv = _attn_branch(zq, zk, zv, prm['att_qn_g'], prm['att_kn_g'], prm['att_sink'], pos0, cache_kv)
    yb, nre, nim = _ssm_branch(zu, s_re, s_im, prm['ssm_lam_re'], prm['ssm_lam_im'], prm['ssm_log_dt'],
                               prm['ssm_b_re'], prm['ssm_b_im'], prm['ssm_c_re'], prm['ssm_c_im'],
                               prm['ssm_d'], prm['ssm_glu_w'], prm['ssm_glu_b'])
    yc, vrows = _spatial_gate_branch(zc, prm['sg_norm_g'], prm['sg_w'], prm['sg_b'])
    yd, nS, nconv = _delta_branch(zdqkv, zdg, zda, zdb, S0, conv0, prm['dn_conv_w'], prm['dn_a_log'],
                                  prm['dn_dt_bias'], prm['dn_norm_g'])
    ys = jnp.stack([ya, yb, yc, yd], axis=2).astype(x.dtype)
    br = jnp.einsum('blnw,nwd->blnd', ys, prm['w_branch'])
    gates = jax.nn.sigmoid(zgate.reshape(B, L, N_BRANCH, D_MODEL))
    x = x + jnp.sum(gates * br, axis=2) @ prm['w_out']
    h2 = _rmsnorm(x, prm['norm2_g'])
    x = x + jnp.square(jax.nn.relu(h2 @ prm['w_ff1'])) @ prm['w_ff2']
    return x, (nk, nv, nre, nim, nS, nconv, vrows)


def setup_inputs(seed: int = 0) -> dict:
    key = jax.random.key(seed)
    ks = jax.random.split(key, 40)
    f32 = jnp.float32

    def nrm(k, shape, scale=1.0):
        return scale * jax.random.normal(k, shape, f32)

    w_cache = min(WINDOW, PAST_LEN)
    dn_dt = jnp.exp(jax.random.uniform(ks[30], (DEPTH, DN_H), f32, math.log(1e-3), math.log(1e-1)))
    return {
        'x_prompt': nrm(ks[0], (BATCH, SEQ, D_MODEL)),
        'x_sample': nrm(ks[1], (DEC_BATCH, DEC_SEQ, D_MODEL)),
        'cache_k': nrm(ks[2], (DEPTH, DEC_BATCH, w_cache, ATT_HKV, ATT_HD)),
        'cache_v': nrm(ks[3], (DEPTH, DEC_BATCH, w_cache, ATT_HKV, ATT_HD)),
        'state_ssm_re': nrm(ks[4], (DEPTH, DEC_BATCH, SSM_NG, SSM_P), 0.1),
        'state_ssm_im': nrm(ks[5], (DEPTH, DEC_BATCH, SSM_NG, SSM_P), 0.1),
        'state_delta': nrm(ks[6], (DEPTH, DEC_BATCH, DN_H, DN_HD, DN_HD), 0.1),
        'state_conv': nrm(ks[7], (DEPTH, DEC_BATCH, DN_CONV - 1, 3 * BRANCH_W)),
        'norm1_g': 1.0 + nrm(ks[8], (DEPTH, D_MODEL), 0.02),
        'w_in': nrm(ks[9], (DEPTH, D_MODEL, N_IN), D_MODEL ** -0.5),
        'att_qn_g': 1.0 + nrm(ks[10], (DEPTH, ATT_HD), 0.02),
        'att_kn_g': 1.0 + nrm(ks[11], (DEPTH, ATT_HD), 0.02),
        'att_sink': nrm(ks[12], (DEPTH, ATT_HQ), 0.5),
        'ssm_lam_re': -0.5 + nrm(ks[13], (DEPTH, SSM_NG, SSM_P), 0.01),
        'ssm_lam_im': jnp.pi * jnp.arange(SSM_P, dtype=f32)[None, None, :] + nrm(ks[14], (DEPTH, SSM_NG, SSM_P), 0.01),
        'ssm_log_dt': jax.random.uniform(ks[15], (DEPTH, SSM_NG), f32, math.log(1e-3), math.log(1e-1)),
        'ssm_b_re': nrm(ks[16], (DEPTH, SSM_NG, SSM_P, SSM_GROUP), (2 * SSM_GROUP) ** -0.5),
        'ssm_b_im': nrm(ks[17], (DEPTH, SSM_NG, SSM_P, SSM_GROUP), (2 * SSM_GROUP) ** -0.5),
        'ssm_c_re': nrm(ks[18], (DEPTH, SSM_NG, SSM_GROUP, SSM_P), SSM_P ** -0.5),
        'ssm_c_im': nrm(ks[19], (DEPTH, SSM_NG, SSM_GROUP, SSM_P), SSM_P ** -0.5),
        'ssm_d': nrm(ks[20], (DEPTH, BRANCH_W)),
        'ssm_glu_w': nrm(ks[21], (DEPTH, BRANCH_W, BRANCH_W), BRANCH_W ** -0.5),
        'ssm_glu_b': nrm(ks[22], (DEPTH, BRANCH_W), 0.01),
        'sg_norm_g': 1.0 + nrm(ks[23], (DEPTH, BRANCH_W), 0.02),
        'sg_w': nrm(ks[24], (DEPTH, SG_GROUPS, CHUNK, CHUNK), CHUNK ** -0.5),
        'sg_b': 1.0 + nrm(ks[25], (DEPTH, SG_GROUPS, CHUNK), 0.02),
        'dn_conv_w': nrm(ks[26], (DEPTH, DN_CONV, 3 * BRANCH_W), DN_CONV ** -0.5),
        'dn_a_log': jnp.log(jax.random.uniform(ks[27], (DEPTH, DN_H), f32, 1.0, 16.0)),
        'dn_dt_bias': dn_dt + jnp.log(-jnp.expm1(-dn_dt)),
        'dn_norm_g': 1.0 + nrm(ks[28], (DEPTH, DN_HD), 0.02),
        'w_branch': nrm(ks[29], (DEPTH, N_BRANCH, BRANCH_W, D_MODEL), BRANCH_W ** -0.5),
        'w_out': nrm(ks[31], (DEPTH, D_MODEL, D_MODEL), D_MODEL ** -0.5),
        'norm2_g': 1.0 + nrm(ks[32], (DEPTH, D_MODEL), 0.02),
        'w_ff1': nrm(ks[33], (DEPTH, D_MODEL, D_FF), D_MODEL ** -0.5),
        'w_ff2': nrm(ks[34], (DEPTH, D_FF, D_MODEL), D_FF ** -0.5),
    }


def reference(x_prompt, x_sample, cache_k, cache_v, state_ssm_re, state_ssm_im, state_delta, state_conv,
              norm1_g, w_in, att_qn_g, att_kn_g, att_sink, ssm_lam_re, ssm_lam_im, ssm_log_dt,
              ssm_b_re, ssm_b_im, ssm_c_re, ssm_c_im, ssm_d, ssm_glu_w, ssm_glu_b,
              sg_norm_g, sg_w, sg_b, dn_conv_w, dn_a_log, dn_dt_bias, dn_norm_g,
              w_branch, w_out, norm2_g, w_ff1, w_ff2):
    xp, xs = x_prompt, x_sample
    new_p = [[] for _ in range(7)]
    new_s = [[] for _ in range(7)]
    for l in range(DEPTH):
        prm = {
            'norm1_g': norm1_g[l], 'w_in': w_in[l], 'att_qn_g': att_qn_g[l], 'att_kn_g': att_kn_g[l],
            'att_sink': att_sink[l], 'ssm_lam_re': ssm_lam_re[l], 'ssm_lam_im': ssm_lam_im[l],
            'ssm_log_dt': ssm_log_dt[l], 'ssm_b_re': ssm_b_re[l], 'ssm_b_im': ssm_b_im[l],
            'ssm_c_re': ssm_c_re[l], 'ssm_c_im': ssm_c_im[l], 'ssm_d': ssm_d[l], 'ssm_glu_w': ssm_glu_w[l],
            'ssm_glu_b': ssm_glu_b[l], 'sg_norm_g': sg_norm_g[l], 'sg_w': sg_w[l], 'sg_b': sg_b[l],
            'dn_conv_w': dn_conv_w[l], 'dn_a_log': dn_a_log[l], 'dn_dt_bias': dn_dt_bias[l],
            'dn_norm_g': dn_norm_g[l], 'w_branch': w_branch[l], 'w_out': w_out[l], 'norm2_g': norm2_g[l],
            'w_ff1': w_ff1[l], 'w_ff2': w_ff2[l],
        }
        xp, sp = _layer(xp, 0, None, prm, True)
        st = (cache_k[l], cache_v[l], state_ssm_re[l], state_ssm_im[l], state_delta[l], state_conv[l])
        xs, ss = _layer(xs, PAST_LEN, st, prm, False)
        for i in range(7):
            new_p[i].append(sp[i])
            new_s[i].append(ss[i])
    p_k, p_v, p_ssm_re, p_ssm_im, p_delta, p_conv, p_chunk_v = [jnp.stack(a, axis=0) for a in new_p]
    s_k, s_v, s_ssm_re, s_ssm_im, s_delta, s_conv, s_chunk_v = [jnp.stack(a, axis=0) for a in new_s]
    return (xp, xs, p_k, p_v, p_ssm_re, p_ssm_im, p_delta, p_conv, p_chunk_v,
            s_k, s_v, s_ssm_re, s_ssm_im, s_delta, s_conv, s_chunk_v)
```

```python
import jax
import jax.numpy as jnp
from jax import lax
from jax.experimental import pallas as pl
from jax.experimental.pallas import tpu as pltpu

F32 = jnp.float32
BF16 = jnp.bfloat16
HIGHEST = lax.Precision.HIGHEST

D_MODEL = 2048
DEPTH = 4
PAST_LEN = 16384
N_BRANCH = 4
BRANCH_W = D_MODEL // N_BRANCH
ATT_HD = 64
ATT_HQ = BRANCH_W // ATT_HD
ATT_HKV = 2
ATT_G = ATT_HQ // ATT_HKV
WINDOW = 128
ROPE_THETA = 10000.0
SSM_GROUP = 16
SSM_NG = BRANCH_W // SSM_GROUP
SSM_P = 64
SSM_N = SSM_NG * SSM_P
SSM_SEGS = 8
CHUNK = 128
SG_GROUPS = 4
SG_GW = BRANCH_W // SG_GROUPS
DN_HD = 128
DN_H = BRANCH_W // DN_HD
DN_CONV = 4
DN_CHUNK = 64
D_FF = 4 * D_MODEL
EPS = 1e-6
NEG_INF = -1e30

LANES = 128
SUBLANES = 8
V7X_VMEM_LIMIT = 56 * 1024 * 1024

W_ATT = ATT_HQ * ATT_HD + 2 * ATT_HKV * ATT_HD
W_SSM = BRANCH_W
W_SG = 2 * BRANCH_W
W_DN_QKV = 3 * BRANCH_W
W_DN = W_DN_QKV + BRANCH_W + LANES
N_MAIN = W_ATT + W_SSM + W_SG + W_DN_QKV + BRANCH_W + 2 * DN_H


def _params(*sem):
    return pltpu.CompilerParams(dimension_semantics=sem, vmem_limit_bytes=V7X_VMEM_LIMIT)


def _sds(shape, dtype):
    return jax.ShapeDtypeStruct(shape, dtype)


def _dot(a, b, precision=None):
    return jnp.dot(a, b, preferred_element_type=F32, precision=precision)


def _dot_nt(a, b, precision=None):
    return lax.dot_general(a, b, (((1,), (1,)), ((), ())), preferred_element_type=F32, precision=precision)


def _dot_tn(a, b, precision=None):
    return lax.dot_general(a, b, (((0,), (0,)), ((), ())), preferred_element_type=F32, precision=precision)


def _rms(x, g):
    ms = jnp.mean(x * x, axis=-1, keepdims=True)
    return x * lax.rsqrt(ms + EPS) * g


def _split_dot(x, e):
    hi = x.astype(BF16)
    lo = (x - hi.astype(F32)).astype(BF16)
    return _dot(hi, e) + _dot(lo, e)


def _rmsnorm_body(x_ref, g_ref, o_ref):
    o_ref[...] = _rms(x_ref[...], g_ref[...]).astype(o_ref.dtype)


def _rmsnorm_call(x, g, tm):
    m, d = x.shape
    return pl.pallas_call(
        _rmsnorm_body,
        out_shape=_sds((m, d), BF16),
        grid=(m // tm,),
        in_specs=[pl.BlockSpec((tm, d), lambda i: (i, 0)), pl.BlockSpec((1, d), lambda i: (0, 0))],
        out_specs=pl.BlockSpec((tm, d), lambda i: (i, 0)),
        compiler_params=_params("parallel"),
        name="rmsnorm",
    )(x, g)


def _matmul_body(h_ref, w_ref, o_ref):
    o_ref[...] = _dot(h_ref[...], w_ref[...])


def _matmul_call(h, w, tm):
    m, k = h.shape
    n = w.shape[1]
    return pl.pallas_call(
        _matmul_body,
        out_shape=_sds((m, n), F32),
        grid=(m // tm,),
        in_specs=[pl.BlockSpec((tm, k), lambda i: (i, 0)), pl.BlockSpec((k, n), lambda i: (0, 0))],
        out_specs=pl.BlockSpec((tm, n), lambda i: (i, 0)),
        compiler_params=_params("parallel"),
        name="in_proj",
    )(h, w)


def _merge_body(h_ref, y_ref, wg_ref, wb_ref, o_ref, acc_ref):
    n = pl.program_id(1)
    gate = jax.nn.sigmoid(_dot(h_ref[...], wg_ref[0]))
    contrib = gate * _dot(y_ref[0], wb_ref[0])

    @pl.when(n == 0)
    def _():
        acc_ref[...] = contrib

    @pl.when(n > 0)
    def _():
        acc_ref[...] += contrib

    @pl.when(n == N_BRANCH - 1)
    def _():
        o_ref[...] = acc_ref[...].astype(o_ref.dtype)


def _merge_call(h, ys, wg, wb, tm):
    m, d = h.shape
    return pl.pallas_call(
        _merge_body,
        out_shape=_sds((m, d), BF16),
        grid=(m // tm, N_BRANCH),
        in_specs=[
            pl.BlockSpec((tm, d), lambda i, n: (i, 0)),
            pl.BlockSpec((1, tm, BRANCH_W), lambda i, n: (n, i, 0)),
            pl.BlockSpec((1, d, d), lambda i, n: (n, 0, 0)),
            pl.BlockSpec((1, BRANCH_W, d), lambda i, n: (n, 0, 0)),
        ],
        out_specs=pl.BlockSpec((tm, d), lambda i, n: (i, 0)),
        scratch_shapes=[pltpu.VMEM((tm, d), F32)],
        compiler_params=_params("parallel", "arbitrary"),
        name="merge",
    )(h, ys, wg, wb)


def _proj_body(m_ref, w_ref, x_ref, g_ref, xo_ref, ho_ref):
    xn = x_ref[...] + _dot(m_ref[...], w_ref[...])
    xo_ref[...] = xn
    ho_ref[...] = _rms(xn, g_ref[...]).astype(ho_ref.dtype)


def _proj_call(mix, w, x, g, tm):
    m, d = x.shape
    return pl.pallas_call(
        _proj_body,
        out_shape=(_sds((m, d), F32), _sds((m, d), BF16)),
        grid=(m // tm,),
        in_specs=[
            pl.BlockSpec((tm, d), lambda i: (i, 0)),
            pl.BlockSpec((d, d), lambda i: (0, 0)),
            pl.BlockSpec((tm, d), lambda i: (i, 0)),
            pl.BlockSpec((1, d), lambda i: (0, 0)),
        ],
        out_specs=(pl.BlockSpec((tm, d), lambda i: (i, 0)), pl.BlockSpec((tm, d), lambda i: (i, 0))),
        compiler_params=_params("parallel"),
        name="out_proj",
    )(mix, w, x, g)


def _ffn_body(h_ref, w1_ref, w2_ref, x_ref, g_ref, xo_ref, ho_ref):
    f = pl.program_id(1)
    a = _dot(h_ref[...], w1_ref[...])
    a = jnp.square(jnp.maximum(a, 0.0)).astype(BF16)
    c = _dot(a, w2_ref[...])

    @pl.when(f == 0)
    def _():
        xo_ref[...] = x_ref[...] + c

    @pl.when(f > 0)
    def _():
        xo_ref[...] += c

    @pl.when(f == pl.num_programs(1) - 1)
    def _():
        ho_ref[...] = _rms(xo_ref[...], g_ref[...]).astype(ho_ref.dtype)


def _ffn_call(h, w1, w2, x, g_next, tm, tf):
    m, d = x.shape
    dff = w1.shape[1]
    return pl.pallas_call(
        _ffn_body,
        out_shape=(_sds((m, d), F32), _sds((m, d), BF16)),
        grid=(m // tm, dff // tf),
        in_specs=[
            pl.BlockSpec((tm, d), lambda i, f: (i, 0)),
            pl.BlockSpec((d, tf), lambda i, f: (0, f)),
            pl.BlockSpec((tf, d), lambda i, f: (f, 0)),
            pl.BlockSpec((tm, d), lambda i, f: (i, 0)),
            pl.BlockSpec((1, d), lambda i, f: (0, 0)),
        ],
        out_specs=(pl.BlockSpec((tm, d), lambda i, f: (i, 0)), pl.BlockSpec((tm, d), lambda i, f: (i, 0))),
        compiler_params=_params("parallel", "arbitrary"),
        name="ffn",
    )(h, w1, w2, x, g_next)


def _head_norm_rope(x, g, cos, sin, e):
    w = x.shape[1]
    ss = _split_dot(x * x, e)
    xn = x * lax.rsqrt(ss * (1.0 / ATT_HD) + EPS) * g
    lane = lax.broadcasted_iota(jnp.int32, x.shape, 1)
    first_half = (lane % ATT_HD) < (ATT_HD // 2)
    rot = jnp.where(first_half, pltpu.roll(xn, w - ATT_HD // 2, 1), pltpu.roll(xn, ATT_HD // 2, 1))
    return xn * cos + rot * sin


def _half_placed(x):
    lane = lax.broadcasted_iota(jnp.int32, x.shape, 1)
    lo = lane < ATT_HD
    xr = pltpu.roll(x, ATT_HD, 1)
    zero = jnp.zeros_like(x)
    return ((jnp.where(lo, x, zero), jnp.where(lo, zero, xr)),
            (jnp.where(lo, xr, zero), jnp.where(lo, zero, x)))


def _attn_prompt_body(z_ref, cos_ref, sin_ref, qg_ref, kg_ref, e_ref, sink_ref,
                      y_ref, ko_ref, vo_ref, kprev_ref, vprev_ref):
    i = pl.program_id(1)

    @pl.when(i == 0)
    def _():
        kprev_ref[...] = jnp.zeros_like(kprev_ref)
        vprev_ref[...] = jnp.zeros_like(vprev_ref)

    z = z_ref[...]
    cos = cos_ref[...]
    sin = sin_ref[...]
    nq = ATT_HQ * ATT_HD
    nk = ATT_HKV * ATT_HD
    rep = nq // LANES
    q = _head_norm_rope(z[:, :nq], qg_ref[...], jnp.tile(cos, (1, rep)), jnp.tile(sin, (1, rep)), e_ref[...])
    k = _head_norm_rope(z[:, nq:nq + nk], kg_ref[...], cos, sin, e_ref[:nk, :nk])
    v = z[:, nq + nk:]
    ko_ref[0] = k
    vo_ref[0] = v
    kcat = jnp.concatenate([kprev_ref[...], k], axis=0)
    vcat = jnp.concatenate([vprev_ref[...], v], axis=0)
    kprev_ref[...] = k
    vprev_ref[...] = v
    kparts = _half_placed(kcat)
    vparts = _half_placed(vcat)
    qb = q.astype(BF16)

    rows = 2 * WINDOW
    r = lax.broadcasted_iota(jnp.int32, (rows, rows), 0) % WINDOW
    c = lax.broadcasted_iota(jnp.int32, (rows, rows), 1)
    rel = r - c + WINDOW
    mask = (rel >= 0) & (rel < WINDOW) & ((c >= WINDOW) | (i > 0))
    top = lax.broadcasted_iota(jnp.int32, (rows, 1), 0) < WINDOW

    pair_out = []
    for kv in range(ATT_HKV):
        qst = jnp.concatenate([qb[:, LANES * (2 * kv):LANES * (2 * kv + 1)],
                               qb[:, LANES * (2 * kv + 1):LANES * (2 * kv + 2)]], axis=0)
        acc = None
        for parity in range(2):
            s = _dot_nt(qst, kparts[kv][parity].astype(BF16)) * (ATT_HD ** -0.5)
            s = jnp.where(mask, s, NEG_INF)
            sink = jnp.where(top, sink_ref[ATT_G * kv + parity], sink_ref[ATT_G * kv + 2 + parity])
            mx = jnp.maximum(jnp.max(s, axis=-1, keepdims=True), sink)
            ex = jnp.exp(s - mx)
            den = jnp.sum(ex, axis=-1, keepdims=True) + jnp.exp(sink - mx)
            p = (ex / den).astype(BF16)
            o = _dot(p, vparts[kv][parity].astype(BF16))
            acc = o if acc is None else acc + o
        pair_out += [acc[:WINDOW], acc[WINDOW:]]
    y_ref[...] = jnp.concatenate(pair_out, axis=1).astype(y_ref.dtype)


def _attn_prompt_call(z_att, cos, sin, qg, kg, e, sink, batch, seq):
    nb = seq // WINDOW
    nkv = ATT_HKV * ATT_HD
    return pl.pallas_call(
        _attn_prompt_body,
        out_shape=(_sds((batch * seq, BRANCH_W), BF16), _sds((batch, WINDOW, nkv), F32),
                   _sds((batch, WINDOW, nkv), F32)),
        grid=(batch, nb),
        in_specs=[
            pl.BlockSpec((WINDOW, W_ATT), lambda b, i: (b * nb + i, 0)),
            pl.BlockSpec((WINDOW, LANES), lambda b, i: (i, 0)),
            pl.BlockSpec((WINDOW, LANES), lambda b, i: (i, 0)),
            pl.BlockSpec((1, BRANCH_W), lambda b, i: (0, 0)),
            pl.BlockSpec((1, nkv), lambda b, i: (0, 0)),
            pl.BlockSpec((BRANCH_W, BRANCH_W), lambda b, i: (0, 0)),
            pl.BlockSpec(memory_space=pltpu.SMEM),
        ],
        out_specs=(
            pl.BlockSpec((WINDOW, BRANCH_W), lambda b, i: (b * nb + i, 0)),
            pl.BlockSpec((1, WINDOW, nkv), lambda b, i: (b, 0, 0)),
            pl.BlockSpec((1, WINDOW, nkv), lambda b, i: (b, 0, 0)),
        ),
        scratch_shapes=[pltpu.VMEM((WINDOW, nkv), F32), pltpu.VMEM((WINDOW, nkv), F32)],
        compiler_params=_params("arbitrary", "arbitrary"),
        name="attn_prompt",
    )(z_att, cos, sin, qg, kg, e, sink)


def _attn_sample_body(z_ref, ck_ref, cv_ref, cos_ref, sin_ref, qg_ref, kg_ref, e_ref, sink_ref,
                      y_ref, ko_ref, vo_ref):
    z = z_ref[0]
    cos = cos_ref[...]
    sin = sin_ref[...]
    nq = ATT_HQ * ATT_HD
    nk = ATT_HKV * ATT_HD
    rep = nq // LANES
    q = _head_norm_rope(z[:, :nq], qg_ref[...], jnp.tile(cos, (1, rep)), jnp.tile(sin, (1, rep)), e_ref[...])
    k = _head_norm_rope(z[:, nq:nq + nk], kg_ref[...], cos, sin, e_ref[:nk, :nk])
    v = z[:, nq + nk:]
    ck = ck_ref[0]
    cv = cv_ref[0]
    ko_ref[0, 0:WINDOW - 1, :] = ck_ref[0, 1:WINDOW, :]
    ko_ref[0, WINDOW - 1:WINDOW, :] = k
    vo_ref[0, 0:WINDOW - 1, :] = cv_ref[0, 1:WINDOW, :]
    vo_ref[0, WINDOW - 1:WINDOW, :] = v

    lane = lax.broadcasted_iota(jnp.int32, (1, LANES), 1)
    lo = lane < ATT_HD
    row = lax.broadcasted_iota(jnp.int32, (ATT_HQ, LANES), 0)
    qexp = jnp.zeros((ATT_HQ, LANES), F32)
    for h in range(ATT_HQ):
        pair = q[:, LANES * (h // 2):LANES * (h // 2 + 1)]
        want_lo = (h // ATT_G) == 0
        have_lo = (h % 2) == 0
        src = pair if want_lo == have_lo else pltpu.roll(pair, ATT_HD, 1)
        placed = jnp.where(lo if want_lo else jnp.logical_not(lo), src, 0.0)
        qexp = jnp.where(row == h, jnp.broadcast_to(placed, (ATT_HQ, LANES)), qexp)

    scale = ATT_HD ** -0.5
    s = _dot_nt(qexp.astype(BF16), ck.astype(BF16)) * scale
    key = lax.broadcasted_iota(jnp.int32, s.shape, 1)
    s = jnp.where(key >= 1, s, NEG_INF)
    qr = qexp.astype(BF16).astype(F32)
    kr = k.astype(BF16).astype(F32)
    s_new = jnp.sum(qr * kr, axis=-1, keepdims=True) * scale
    sink = sink_ref[:, 0:1]
    mx = jnp.maximum(jnp.maximum(jnp.max(s, axis=-1, keepdims=True), s_new), sink)
    ex = jnp.exp(s - mx)
    ex_new = jnp.exp(s_new - mx)
    den = jnp.sum(ex, axis=-1, keepdims=True) + ex_new + jnp.exp(sink - mx)
    p = (ex / den).astype(BF16)
    p_new = (ex_new / den).astype(BF16).astype(F32)
    o = _dot(p, cv.astype(BF16)) + p_new * v.astype(BF16).astype(F32)

    outs = []
    for j in range(ATT_HQ // 2):
        parts = []
        for par in range(2):
            h = 2 * j + par
            orow = o[h:h + 1, :]
            have_lo = (h // ATT_G) == 0
            want_lo = par == 0
            src = orow if want_lo == have_lo else pltpu.roll(orow, ATT_HD, 1)
            parts.append(jnp.where(lo if want_lo else jnp.logical_not(lo), src, 0.0))
        outs.append(parts[0] + parts[1])
    y_ref[0] = jnp.concatenate(outs, axis=1).astype(y_ref.dtype)


def _attn_sample_call(z_att, ck, cv, cos, sin, qg, kg, e, sink_b):
    nbatch = z_att.shape[0]
    nkv = ATT_HKV * ATT_HD
    z3 = z_att.reshape(nbatch, 1, W_ATT)
    y, ko, vo = pl.pallas_call(
        _attn_sample_body,
        out_shape=(_sds((nbatch, 1, BRANCH_W), BF16), _sds((nbatch, WINDOW, nkv), F32),
                   _sds((nbatch, WINDOW, nkv), F32)),
        grid=(nbatch,),
        in_specs=[
            pl.BlockSpec((1, 1, W_ATT), lambda b: (b, 0, 0)),
            pl.BlockSpec((1, WINDOW, nkv), lambda b: (b, 0, 0)),
            pl.BlockSpec((1, WINDOW, nkv), lambda b: (b, 0, 0)),
            pl.BlockSpec((1, LANES), lambda b: (0, 0)),
            pl.BlockSpec((1, LANES), lambda b: (0, 0)),
            pl.BlockSpec((1, BRANCH_W), lambda b: (0, 0)),
            pl.BlockSpec((1, nkv), lambda b: (0, 0)),
            pl.BlockSpec((BRANCH_W, BRANCH_W), lambda b: (0, 0)),
            pl.BlockSpec((ATT_HQ, LANES), lambda b: (0, 0)),
        ],
        out_specs=(
            pl.BlockSpec((1, 1, BRANCH_W), lambda b: (b, 0, 0)),
            pl.BlockSpec((1, WINDOW, nkv), lambda b: (b, 0, 0)),
            pl.BlockSpec((1, WINDOW, nkv), lambda b: (b, 0, 0)),
        ),
        compiler_params=_params("parallel"),
        name="attn_sample",
    )(z3, ck, cv, cos, sin, qg, kg, e, sink_b)
    return y.reshape(nbatch, BRANCH_W), ko, vo


def _cmul(ar, ai, br, bi):
    return ar * br - ai * bi, ar * bi + ai * br


def _ssm_tail(y, u, d_ref, gw_ref, gb_ref):
    y = jax.nn.gelu(y + d_ref[...] * u)
    return y * jax.nn.sigmoid(_dot(y.astype(BF16), gw_ref[...]) + gb_ref[...])


def _ssm_prompt_body(u_ref, bm_ref, cm_ref, a_ref, d_ref, gw_ref, gb_ref,
                     y_ref, sre_ref, sim_ref,
                     u3_ref, up_ref, bu_ref, start_ref, carry_ref, yp_ref):
    tb = u_ref.shape[0]
    ts = tb // SSM_SEGS
    nlc = BRANCH_W // LANES
    j = pl.program_id(1)

    @pl.when(j == 0)
    def _():
        carry_ref[...] = jnp.zeros_like(carry_ref)

    for c in range(nlc):
        u3_ref[c] = u_ref[:, c * LANES:(c + 1) * LANES]

    def gather(t, carry):
        for c in range(nlc):
            up_ref[pl.ds(pl.multiple_of(t * SUBLANES, SUBLANES), SUBLANES), c * LANES:(c + 1) * LANES] = (
                u3_ref[c, pl.ds(t, SUBLANES, stride=ts), :])
        return carry

    lax.fori_loop(0, ts, gather, 0)
    up = up_ref[...]
    bu_ref[...] = _dot(up.astype(BF16), bm_ref[...])

    cw = 4 * LANES
    nsweep = SSM_N // cw

    def scan(c0, init_re, init_im, store):
        ar = jnp.broadcast_to(a_ref[0:1, c0:c0 + cw], (SUBLANES, cw))
        ai = jnp.broadcast_to(a_ref[1:2, c0:c0 + cw], (SUBLANES, cw))

        def step(t, st):
            sr, si = st
            r0 = pl.multiple_of(t * SUBLANES, SUBLANES)
            br = bu_ref[pl.ds(r0, SUBLANES), c0:c0 + cw]
            bi = bu_ref[pl.ds(r0, SUBLANES), SSM_N + c0:SSM_N + c0 + cw]
            pr, pi = _cmul(ar, ai, sr, si)
            sr, si = pr + br, pi + bi
            if store:
                bu_ref[pl.ds(r0, SUBLANES), c0:c0 + cw] = sr
                bu_ref[pl.ds(r0, SUBLANES), SSM_N + c0:SSM_N + c0 + cw] = si
            return sr, si

        return lax.fori_loop(0, ts, step, (init_re, init_im))

    zero = jnp.zeros((SUBLANES, cw), F32)
    for sw in range(nsweep):
        c0 = sw * cw
        er, ei = scan(c0, zero, zero, False)
        pr, pi = a_ref[0:1, c0:c0 + cw], a_ref[1:2, c0:c0 + cw]
        n = 1
        while n < ts:
            pr, pi = _cmul(pr, pi, pr, pi)
            n *= 2
        sr = carry_ref[0:1, c0:c0 + cw]
        si = carry_ref[1:2, c0:c0 + cw]
        for s in range(SSM_SEGS):
            start_ref[s:s + 1, c0:c0 + cw] = sr
            start_ref[s:s + 1, SSM_N + c0:SSM_N + c0 + cw] = si
            qr, qi = _cmul(pr, pi, sr, si)
            sr, si = qr + er[s:s + 1, :], qi + ei[s:s + 1, :]
        carry_ref[0:1, c0:c0 + cw] = sr
        carry_ref[1:2, c0:c0 + cw] = si
        scan(c0, start_ref[:, c0:c0 + cw], start_ref[:, SSM_N + c0:SSM_N + c0 + cw], True)

    sre_ref[0] = carry_ref[0:1, :]
    sim_ref[0] = carry_ref[1:2, :]

    y = _dot(bu_ref[...].astype(BF16), cm_ref[...])
    yp_ref[...] = _ssm_tail(y, up, d_ref, gw_ref, gb_ref)

    def scatter(t, carry):
        for c in range(nlc):
            u3_ref[c, pl.ds(t, SUBLANES, stride=ts), :] = (
                yp_ref[pl.ds(pl.multiple_of(t * SUBLANES, SUBLANES), SUBLANES), c * LANES:(c + 1) * LANES])
        return carry

    lax.fori_loop(0, ts, scatter, 0)
    for c in range(nlc):
        y_ref[:, c * LANES:(c + 1) * LANES] = u3_ref[c].astype(y_ref.dtype)


def _ssm_prompt_call(z_u, bm, cm, a, d, gw, gb, batch, seq, tb):
    nblk = seq // tb
    const = lambda b, j: (0, 0)
    return pl.pallas_call(
        _ssm_prompt_body,
        out_shape=(_sds((batch * seq, BRANCH_W), BF16), _sds((batch, 1, SSM_N), F32), _sds((batch, 1, SSM_N), F32)),
        grid=(batch, nblk),
        in_specs=[
            pl.BlockSpec((tb, BRANCH_W), lambda b, j: (b * nblk + j, 0)),
            pl.BlockSpec((BRANCH_W, 2 * SSM_N), const),
            pl.BlockSpec((2 * SSM_N, BRANCH_W), const),
            pl.BlockSpec((2, SSM_N), const),
            pl.BlockSpec((1, BRANCH_W), const),
            pl.BlockSpec((BRANCH_W, BRANCH_W), const),
            pl.BlockSpec((1, BRANCH_W), const),
        ],
        out_specs=(
            pl.BlockSpec((tb, BRANCH_W), lambda b, j: (b * nblk + j, 0)),
            pl.BlockSpec((1, 1, SSM_N), lambda b, j: (b, 0, 0)),
            pl.BlockSpec((1, 1, SSM_N), lambda b, j: (b, 0, 0)),
        ),
        scratch_shapes=[
            pltpu.VMEM((BRANCH_W // LANES, tb, LANES), F32),
            pltpu.VMEM((tb, BRANCH_W), F32),
            pltpu.VMEM((tb, 2 * SSM_N), F32),
            pltpu.VMEM((SSM_SEGS, 2 * SSM_N), F32),
            pltpu.VMEM((2, SSM_N), F32),
            pltpu.VMEM((tb, BRANCH_W), F32),
        ],
        compiler_params=_params("arbitrary", "arbitrary"),
        name="ssm_prompt",
    )(z_u, bm, cm, a, d, gw, gb)


def _ssm_sample_body(u_ref, s0r_ref, s0i_ref, bm_ref, cm_ref, a_ref, d_ref, gw_ref, gb_ref,
                     y_ref, sre_ref, sim_ref):
    u = u_ref[...]
    bu = _dot(u.astype(BF16), bm_ref[...])
    pr, pi = _cmul(a_ref[0:1, :], a_ref[1:2, :], s0r_ref[...], s0i_ref[...])
    sr = pr + bu[:, :SSM_N]
    si = pi + bu[:, SSM_N:]
    sre_ref[...] = sr
    sim_ref[...] = si
    y = _dot(jnp.concatenate([sr, si], axis=1).astype(BF16), cm_ref[...])
    y_ref[...] = _ssm_tail(y, u, d_ref, gw_ref, gb_ref).astype(y_ref.dtype)


def _ssm_sample_call(z_u, s0r, s0i, bm, cm, a, d, gw, gb):
    nbatch = z_u.shape[0]
    return pl.pallas_call(
        _ssm_sample_body,
        out_shape=(_sds((nbatch, BRANCH_W), BF16), _sds((nbatch, SSM_N), F32), _sds((nbatch, SSM_N), F32)),
        compiler_params=pltpu.CompilerParams(vmem_limit_bytes=V7X_VMEM_LIMIT),
        name="ssm_sample",
    )(z_u, s0r, s0i, bm, cm, a, d, gw, gb)


def _sg_prompt_body(z_ref, g_ref, w_ref, b_ref, y_ref, v_ref):
    uv = jax.nn.gelu(z_ref[...])
    u = uv[:, :BRANCH_W]
    v = _rms(uv[:, BRANCH_W:], g_ref[...])
    v_ref[0] = v
    t = lax.broadcasted_iota(jnp.int32, (CHUNK, CHUNK), 0)
    s = lax.broadcasted_iota(jnp.int32, (CHUNK, CHUNK), 1)
    causal = t >= s
    vb = v.astype(BF16)
    mixed = []
    for g in range(SG_GROUPS):
        w = jnp.where(causal, w_ref[g], 0.0).astype(BF16)
        mixed.append(_dot(w, vb[:, g * SG_GW:(g + 1) * SG_GW]))
    mixed = jnp.concatenate(mixed, axis=1) + b_ref[...]
    y_ref[...] = (u * mixed).astype(y_ref.dtype)


def _sg_prompt_call(z_c, g, w, b_exp, batch, seq):
    nch = seq // CHUNK
    return pl.pallas_call(
        _sg_prompt_body,
        out_shape=(_sds((batch * seq, BRANCH_W), BF16), _sds((batch, CHUNK, BRANCH_W), F32)),
        grid=(batch, nch),
        in_specs=[
            pl.BlockSpec((CHUNK, W_SG), lambda b, i: (b * nch + i, 0)),
            pl.BlockSpec((1, BRANCH_W), lambda b, i: (0, 0)),
            pl.BlockSpec((SG_GROUPS, CHUNK, CHUNK), lambda b, i: (0, 0, 0)),
            pl.BlockSpec((CHUNK, BRANCH_W), lambda b, i: (0, 0)),
        ],
        out_specs=(
            pl.BlockSpec((CHUNK, BRANCH_W), lambda b, i: (b * nch + i, 0)),
            pl.BlockSpec((1, CHUNK, BRANCH_W), lambda b, i: (b, 0, 0)),
        ),
        compiler_params=_params("arbitrary", "arbitrary"),
        name="sg_prompt",
    )(z_c, g, w, b_exp)


def _sg_sample_body(z_ref, g_ref, w00_ref, b_ref, y_ref, v_ref):
    uv = jax.nn.gelu(z_ref[...])
    u = uv[:, :BRANCH_W]
    v = _rms(uv[:, BRANCH_W:], g_ref[...])
    v_ref[...] = v
    w = w00_ref[...].astype(BF16).astype(F32)
    mixed = w * v.astype(BF16).astype(F32) + b_ref[0:1, :]
    y_ref[...] = (u * mixed).astype(y_ref.dtype)


def _sg_sample_call(z_c, g, w00, b_exp):
    nbatch = z_c.shape[0]
    return pl.pallas_call(
        _sg_sample_body,
        out_shape=(_sds((nbatch, BRANCH_W), BF16), _sds((nbatch, BRANCH_W), F32)),
        name="sg_sample",
    )(z_c, g, w00, b_exp)


def _l2n(x):
    return x * lax.rsqrt(jnp.sum(x * x, axis=-1, keepdims=True) + EPS)


def _dn_gates(zab, alog_ref, dtb_ref):
    g = -jnp.exp(alog_ref[...]) * jax.nn.softplus(zab + dtb_ref[...])
    beta = jax.nn.sigmoid(zab)
    return g, beta


def _unit_lower_inverse(a):
    n = a.shape[0]
    r = lax.broadcasted_iota(jnp.int32, (n, n), 0)
    c = lax.broadcasted_iota(jnp.int32, (n, n), 1)
    inv = jnp.where(r == c, 1.0, 0.0).astype(F32)
    size = 1
    while size < n:
        off = ((r // (2 * size)) == (c // (2 * size))) & ((r // size) % 2 == 1) & ((c // size) % 2 == 0)
        inv = inv - _dot(_dot(inv, jnp.where(off, a, 0.0), HIGHEST), inv, HIGHEST)
        size *= 2
    return inv


def _dn_prompt_body(x_ref, zg_ref, zab_ref, cw_ref, alog_ref, dtb_ref, ng_ref,
                    y_ref, so_ref, co_ref, xin_ref, s_ref):
    ci = pl.program_id(1)
    tc = x_ref.shape[0]
    pad = SUBLANES

    @pl.when(ci == 0)
    def _():
        xin_ref[0:pad, :] = jnp.zeros((pad, W_DN_QKV), F32)
        s_ref[...] = jnp.zeros_like(s_ref)

    @pl.when(ci > 0)
    def _():
        xin_ref[0:pad, :] = xin_ref[tc:tc + pad, :]

    x = x_ref[...]
    xin_ref[pad:pad + tc, :] = x
    co_ref[0] = x_ref[tc - (DN_CONV - 1):tc, :]
    acc = cw_ref[DN_CONV - 1:DN_CONV, :] * x
    for i in range(DN_CONV - 1):
        lag = DN_CONV - 1 - i
        acc = acc + cw_ref[i:i + 1, :] * xin_ref[pad - lag:pad - lag + tc, :]
    qkv = jax.nn.silu(acc)

    g_all, beta_all = _dn_gates(zab_ref[...], alog_ref, dtb_ref)
    r = lax.broadcasted_iota(jnp.int32, (tc, tc), 0)
    c = lax.broadcasted_iota(jnp.int32, (tc, tc), 1)
    tril = r >= c
    stril = r > c
    gc_all = _dot(jnp.where(tril, 1.0, 0.0).astype(F32), g_all, HIGHEST)
    lane = lax.broadcasted_iota(jnp.int32, (tc, LANES), 1)
    ones = jnp.ones((tc, LANES), F32)
    zg = zg_ref[...]

    outs = []
    for h in range(DN_H):
        q = _l2n(qkv[:, h * DN_HD:(h + 1) * DN_HD]) * (DN_HD ** -0.5)
        k = _l2n(qkv[:, BRANCH_W + h * DN_HD:BRANCH_W + (h + 1) * DN_HD])
        v = qkv[:, 2 * BRANCH_W + h * DN_HD:2 * BRANCH_W + (h + 1) * DN_HD]
        gc = gc_all[:, h:h + 1]
        beta = beta_all[:, DN_H + h:DN_H + h + 1]
        gc_row = _dot_nt(ones, jnp.where(lane == h, gc_all, 0.0), HIGHEST)
        dec = jnp.where(tril, jnp.exp(jnp.where(tril, gc - gc_row, 0.0)), 0.0)
        kb = k.astype(BF16)
        kk = _dot_nt(kb, kb)
        a = jnp.where(stril, beta * kk * dec, 0.0)
        inv = _unit_lower_inverse(a)
        eg = jnp.exp(gc)
        rhs = jnp.concatenate([beta * v, (beta * eg) * k], axis=1)
        sol = _dot(inv, rhs, HIGHEST)
        u_t, w_k = sol[:, :DN_HD], sol[:, DN_HD:]
        qk = _dot_nt(q.astype(BF16), kb) * dec
        q_g = q * eg
        gc_last = gc_all[tc - 1:tc, h:h + 1]
        k_end = k * jnp.exp(gc_last - gc)
        g_end = jnp.exp(gc_last)

        s = s_ref[h]
        sb = s.astype(BF16)
        u = u_t - _dot(w_k.astype(BF16), sb)
        ub = u.astype(BF16)
        o = _dot(q_g.astype(BF16), sb) + _dot(qk.astype(BF16), ub)
        s_ref[h] = g_end * s + _dot_tn(k_end.astype(BF16), ub)
        o = _rms(o, ng_ref[...]) * jax.nn.silu(zg[:, h * DN_HD:(h + 1) * DN_HD])
        outs.append(o)
    y_ref[...] = jnp.concatenate(outs, axis=1).astype(y_ref.dtype)

    @pl.when(ci == pl.num_programs(1) - 1)
    def _():
        so_ref[0] = s_ref[...]


def _dn_prompt_call(z_dn, cw, alog, dtb, ng, batch, seq):
    tc = DN_CHUNK
    nch = seq // tc
    const = lambda b, i: (0, 0)
    return pl.pallas_call(
        _dn_prompt_body,
        out_shape=(_sds((batch * seq, BRANCH_W), BF16), _sds((batch, DN_H, DN_HD, DN_HD), F32),
                   _sds((batch, DN_CONV - 1, W_DN_QKV), F32)),
        grid=(batch, nch),
        in_specs=[
            pl.BlockSpec((tc, W_DN_QKV), lambda b, i: (b * nch + i, 0)),
            pl.BlockSpec((tc, BRANCH_W), lambda b, i: (b * nch + i, W_DN_QKV // BRANCH_W)),
            pl.BlockSpec((tc, LANES), lambda b, i: (b * nch + i, (W_DN_QKV + BRANCH_W) // LANES)),
            pl.BlockSpec((DN_CONV, W_DN_QKV), const),
            pl.BlockSpec((1, LANES), const),
            pl.BlockSpec((1, LANES), const),
            pl.BlockSpec((1, DN_HD), const),
        ],
        out_specs=(
            pl.BlockSpec((tc, BRANCH_W), lambda b, i: (b * nch + i, 0)),
            pl.BlockSpec((1, DN_H, DN_HD, DN_HD), lambda b, i: (b, 0, 0, 0)),
            pl.BlockSpec((1, DN_CONV - 1, W_DN_QKV), lambda b, i: (b, 0, 0)),
        ),
        scratch_shapes=[pltpu.VMEM((tc + SUBLANES, W_DN_QKV), F32), pltpu.VMEM((DN_H, DN_HD, DN_HD), F32)],
        compiler_params=_params("arbitrary", "arbitrary"),
        name="dn_prompt",
    )(z_dn, z_dn, z_dn, cw, alog, dtb, ng)


def _dn_sample_body(x_ref, zg_ref, zab_ref, conv_ref, s0_ref, cw_ref, alog_ref, dtb_ref, ng_ref,
                    y_ref, so_ref, co_ref):
    x = x_ref[0]
    co_ref[0, 0:DN_CONV - 2, :] = conv_ref[0, 1:DN_CONV - 1, :]
    co_ref[0, DN_CONV - 2:DN_CONV - 1, :] = x
    acc = cw_ref[DN_CONV - 1:DN_CONV, :] * x
    for i in range(DN_CONV - 1):
        acc = acc + cw_ref[i:i + 1, :] * conv_ref[0, i:i + 1, :]
    qkv = jax.nn.silu(acc)
    g_all, beta_all = _dn_gates(zab_ref[0], alog_ref, dtb_ref)
    zg = zg_ref[0]
    r = lax.broadcasted_iota(jnp.int32, (DN_HD, DN_HD), 0)
    c = lax.broadcasted_iota(jnp.int32, (DN_HD, DN_HD), 1)
    eye = r == c
    row8 = lax.broadcasted_iota(jnp.int32, (SUBLANES, DN_HD), 0)
    outs = []
    for h in range(DN_H):
        q = _l2n(qkv[:, h * DN_HD:(h + 1) * DN_HD]) * (DN_HD ** -0.5)
        k = _l2n(qkv[:, BRANCH_W + h * DN_HD:BRANCH_W + (h + 1) * DN_HD])
        v = qkv[:, 2 * BRANCH_W + h * DN_HD:2 * BRANCH_W + (h + 1) * DN_HD]
        eg = jnp.exp(g_all[:, h:h + 1])
        beta = beta_all[:, DN_H + h:DN_H + h + 1]
        s = s0_ref[0, h]
        sb = s.astype(BF16)
        kq = jnp.where(row8 == 0, jnp.broadcast_to((beta * eg) * k, (SUBLANES, DN_HD)),
                       jnp.where(row8 == 1, jnp.broadcast_to(q * eg, (SUBLANES, DN_HD)), 0.0))
        ks_qs = _dot(kq.astype(BF16), sb)
        u = beta * v - ks_qs[0:1, :]
        qk = jnp.sum(q.astype(BF16).astype(F32) * k.astype(BF16).astype(F32), axis=-1, keepdims=True)
        o = ks_qs[1:2, :] + qk * u
        kdiag = jnp.where(eye, jnp.broadcast_to(k, (DN_HD, DN_HD)), 0.0)
        outer = _dot(kdiag, jnp.broadcast_to(u, (DN_HD, DN_HD)), HIGHEST)
        so_ref[0, h] = eg * s + outer
        o = _rms(o, ng_ref[...]) * jax.nn.silu(zg[:, h * DN_HD:(h + 1) * DN_HD])
        outs.append(o)
    y_ref[0] = jnp.concatenate(outs, axis=1).astype(y_ref.dtype)


def _dn_sample_call(z_dn, conv0, s0, cw, alog, dtb, ng):
    nbatch = z_dn.shape[0]
    x3 = z_dn[:, :W_DN_QKV].reshape(nbatch, 1, W_DN_QKV)
    zg3 = z_dn[:, W_DN_QKV:W_DN_QKV + BRANCH_W].reshape(nbatch, 1, BRANCH_W)
    zab3 = z_dn[:, W_DN_QKV + BRANCH_W:].reshape(nbatch, 1, LANES)
    const = lambda b: (0, 0)
    y, so, co = pl.pallas_call(
        _dn_sample_body,
        out_shape=(_sds((nbatch, 1, BRANCH_W), BF16), _sds((nbatch, DN_H, DN_HD, DN_HD), F32),
                   _sds((nbatch, DN_CONV - 1, W_DN_QKV), F32)),
        grid=(nbatch,),
        in_specs=[
            pl.BlockSpec((1, 1, W_DN_QKV), lambda b: (b, 0, 0)),
            pl.BlockSpec((1, 1, BRANCH_W), lambda b: (b, 0, 0)),
            pl.BlockSpec((1, 1, LANES), lambda b: (b, 0, 0)),
            pl.BlockSpec((1, DN_CONV - 1, W_DN_QKV), lambda b: (b, 0, 0)),
            pl.BlockSpec((1, DN_H, DN_HD, DN_HD), lambda b: (b, 0, 0, 0)),
            pl.BlockSpec((DN_CONV, W_DN_QKV), const),
            pl.BlockSpec((1, LANES), const),
            pl.BlockSpec((1, LANES), const),
            pl.BlockSpec((1, DN_HD), const),
        ],
        out_specs=(
            pl.BlockSpec((1, 1, BRANCH_W), lambda b: (b, 0, 0)),
            pl.BlockSpec((1, DN_H, DN_HD, DN_HD), lambda b: (b, 0, 0, 0)),
            pl.BlockSpec((1, DN_CONV - 1, W_DN_QKV), lambda b: (b, 0, 0)),
        ),
        compiler_params=_params("parallel"),
        name="dn_sample",
    )(x3, zg3, zab3, conv0, s0, cw, alog, dtb, ng)
    return y.reshape(nbatch, BRANCH_W), so, co


def _rope_tables(pos):
    half = ATT_HD // 2
    inv = ROPE_THETA ** (-jnp.arange(half, dtype=F32) / half)
    ang = pos.astype(F32)[:, None] * inv[None, :]
    cos, sin = jnp.cos(ang), jnp.sin(ang)
    reps = LANES // ATT_HD
    return (jnp.tile(jnp.concatenate([cos, cos], axis=1), (1, reps)),
            jnp.tile(jnp.concatenate([-sin, sin], axis=1), (1, reps)))


def _ssm_matrices(lam_re, lam_im, log_dt, b_re, b_im, c_re, c_im):
    lam = lax.complex(lam_re, lam_im)
    a_bar = jnp.exp(lam * jnp.exp(log_dt)[:, None])
    b_bar = ((a_bar - 1.0) / lam)[..., None] * lax.complex(b_re, b_im)
    eye = jnp.eye(SSM_NG, dtype=F32)

    def in_layout(b):
        return jnp.einsum("gpc,gh->gchp", b, eye).reshape(BRANCH_W, SSM_N)

    def out_layout(c):
        return jnp.einsum("gcp,gh->gphc", c, eye).reshape(SSM_N, BRANCH_W)

    bm = jnp.concatenate([in_layout(jnp.real(b_bar)), in_layout(jnp.imag(b_bar))], axis=1).astype(BF16)
    cm = jnp.concatenate([out_layout(c_re), out_layout(-c_im)], axis=0).astype(BF16)
    a = jnp.stack([jnp.real(a_bar).reshape(SSM_N), jnp.imag(a_bar).reshape(SSM_N)], axis=0)
    return bm, cm, a


def _lane_row(x):
    return jnp.zeros((1, LANES), F32).at[0, :x.shape[0]].set(x)


def _layer_params(l, p):
    w_in = p["w_in"][l]
    w_main = w_in[:, :N_MAIN].astype(BF16)
    o_att, o_ssm, o_sg, o_dn = 0, W_ATT, W_ATT + W_SSM, W_ATT + W_SSM + W_SG
    w_dn = jnp.concatenate([w_main[:, o_dn:], jnp.zeros((D_MODEL, W_DN - (N_MAIN - o_dn)), BF16)], axis=1)
    bm, cm, a = _ssm_matrices(p["ssm_lam_re"][l], p["ssm_lam_im"][l], p["ssm_log_dt"][l], p["ssm_b_re"][l],
                              p["ssm_b_im"][l], p["ssm_c_re"][l], p["ssm_c_im"][l])
    return dict(
        w_att=w_main[:, o_att:o_ssm], w_ssm=w_main[:, o_ssm:o_sg], w_sg=w_main[:, o_sg:o_dn], w_dn=w_dn,
        w_gate=jnp.transpose(w_in[:, N_MAIN:].reshape(D_MODEL, N_BRANCH, D_MODEL), (1, 0, 2)).astype(BF16),
        w_branch=p["w_branch"][l].astype(BF16), w_out=p["w_out"][l].astype(BF16),
        w_ff1=p["w_ff1"][l].astype(BF16), w_ff2=p["w_ff2"][l].astype(BF16),
        norm2_g=p["norm2_g"][l].reshape(1, D_MODEL),
        qg=jnp.tile(p["att_qn_g"][l], ATT_HQ).reshape(1, BRANCH_W),
        kg=jnp.tile(p["att_kn_g"][l], ATT_HKV).reshape(1, ATT_HKV * ATT_HD),
        sink=p["att_sink"][l],
        sink_b=jnp.broadcast_to(p["att_sink"][l][:, None], (ATT_HQ, LANES)),
        bm=bm, cm=cm, a=a, ssm_d=p["ssm_d"][l].reshape(1, BRANCH_W),
        glu_w=p["ssm_glu_w"][l].astype(BF16), glu_b=p["ssm_glu_b"][l].reshape(1, BRANCH_W),
        sg_g=p["sg_norm_g"][l].reshape(1, BRANCH_W), sg_w=p["sg_w"][l],
        sg_b=jnp.repeat(jnp.transpose(p["sg_b"][l]), SG_GW, axis=1),
        sg_w00=jnp.repeat(p["sg_w"][l][:, 0, 0], SG_GW).reshape(1, BRANCH_W),
        cw=p["dn_conv_w"][l], alog=_lane_row(p["dn_a_log"][l]), dtb=_lane_row(p["dn_dt_bias"][l]),
        ng=p["dn_norm_g"][l].reshape(1, DN_HD),
    )


def _tile_rows(m):
    return 512 if m % 512 == 0 else m


def _layer(xp, hp, xs, hs, st, lp, g_next, rope_p, rope_s, e, batch, seq):
    tmp, tms = _tile_rows(xp.shape[0]), _tile_rows(xs.shape[0])
    ck, cv, s0r, s0i, s0d, conv0 = st

    ya, pk, pv = _attn_prompt_call(_matmul_call(hp, lp["w_att"], tmp), rope_p[0], rope_p[1], lp["qg"], lp["kg"], e,
                                   lp["sink"], batch, seq)
    tb = 512 if seq % 512 == 0 else seq
    yb, pre, pim = _ssm_prompt_call(_matmul_call(hp, lp["w_ssm"], tmp), lp["bm"], lp["cm"], lp["a"], lp["ssm_d"],
                                    lp["glu_w"], lp["glu_b"], batch, seq, tb)
    yc, pchunk = _sg_prompt_call(_matmul_call(hp, lp["w_sg"], tmp), lp["sg_g"], lp["sg_w"], lp["sg_b"], batch, seq)
    yd, pdelta, pconv = _dn_prompt_call(_matmul_call(hp, lp["w_dn"], tmp), lp["cw"], lp["alog"], lp["dtb"], lp["ng"],
                                        batch, seq)
    mix = _merge_call(hp, jnp.stack([ya, yb, yc, yd]), lp["w_gate"], lp["w_branch"], tmp)
    xp, h2 = _proj_call(mix, lp["w_out"], xp, lp["norm2_g"], tmp)
    xp, hp = _ffn_call(h2, lp["w_ff1"], lp["w_ff2"], xp, g_next, tmp, 512)

    ya, sk, sv = _attn_sample_call(_matmul_call(hs, lp["w_att"], tms), ck, cv, rope_s[0], rope_s[1], lp["qg"],
                                   lp["kg"], e, lp["sink_b"])
    yb, sre, sim = _ssm_sample_call(_matmul_call(hs, lp["w_ssm"], tms), s0r, s0i, lp["bm"], lp["cm"], lp["a"],
                                    lp["ssm_d"], lp["glu_w"], lp["glu_b"])
    yc, schunk = _sg_sample_call(_matmul_call(hs, lp["w_sg"], tms), lp["sg_g"], lp["sg_w00"], lp["sg_b"])
    yd, sdelta, sconv = _dn_sample_call(_matmul_call(hs, lp["w_dn"], tms), conv0, s0d, lp["cw"], lp["alog"],
                                        lp["dtb"], lp["ng"])
    mix = _merge_call(hs, jnp.stack([ya, yb, yc, yd]), lp["w_gate"], lp["w_branch"], tms)
    xs, h2 = _proj_call(mix, lp["w_out"], xs, lp["norm2_g"], tms)
    xs, hs = _ffn_call(h2, lp["w_ff1"], lp["w_ff2"], xs, g_next, tms, 512)

    new_p = (pk, pv, pre, pim, pdelta, pconv, pchunk)
    new_s = (sk, sv, sre, sim, sdelta, sconv, schunk)
    return xp, hp, xs, hs, new_p, new_s


def kernel(x_prompt, x_sample, cache_k, cache_v, state_ssm_re, state_ssm_im, state_delta, state_conv, norm1_g, w_in, att_qn_g, att_kn_g, att_sink, ssm_lam_re, ssm_lam_im, ssm_log_dt, ssm_b_re, ssm_b_im, ssm_c_re, ssm_c_im, ssm_d, ssm_glu_w, ssm_glu_b, sg_norm_g, sg_w, sg_b, dn_conv_w, dn_a_log, dn_dt_bias, dn_norm_g, w_branch, w_out, norm2_g, w_ff1, w_ff2):
    p = dict(w_in=w_in, att_qn_g=att_qn_g, att_kn_g=att_kn_g, att_sink=att_sink, ssm_lam_re=ssm_lam_re,
             ssm_lam_im=ssm_lam_im, ssm_log_dt=ssm_log_dt, ssm_b_re=ssm_b_re, ssm_b_im=ssm_b_im, ssm_c_re=ssm_c_re,
             ssm_c_im=ssm_c_im, ssm_d=ssm_d, ssm_glu_w=ssm_glu_w, ssm_glu_b=ssm_glu_b, sg_norm_g=sg_norm_g, sg_w=sg_w,
             sg_b=sg_b, dn_conv_w=dn_conv_w, dn_a_log=dn_a_log, dn_dt_bias=dn_dt_bias, dn_norm_g=dn_norm_g,
             w_branch=w_branch, w_out=w_out, norm2_g=norm2_g, w_ff1=w_ff1, w_ff2=w_ff2)
    depth = w_in.shape[0]
    batch, seq, d = x_prompt.shape
    nsamp = x_sample.shape[0]
    past = cache_k.shape[2]
    assert x_sample.shape[1] == 1 and past == WINDOW and seq % CHUNK == 0 and d == D_MODEL
    nkv = ATT_HKV * ATT_HD

    xp = x_prompt.reshape(batch * seq, d)
    xs = x_sample.reshape(nsamp, d)
    rope_p = _rope_tables(jnp.arange(seq))
    rope_s = _rope_tables(PAST_LEN + jnp.arange(1))
    lane = jnp.arange(BRANCH_W) // ATT_HD
    e = (lane[:, None] == lane[None, :]).astype(BF16)

    g1 = norm1_g.reshape(depth, 1, d)
    hp = _rmsnorm_call(xp, g1[0], _tile_rows(xp.shape[0]))
    hs = _rmsnorm_call(xs, g1[0], _tile_rows(xs.shape[0]))
    new_p = [[] for _ in range(7)]
    new_s = [[] for _ in range(7)]
    for l in range(depth):
        lp = _layer_params(l, p)
        st = (cache_k[l].reshape(nsamp, past, nkv), cache_v[l].reshape(nsamp, past, nkv),
              state_ssm_re[l].reshape(nsamp, SSM_N), state_ssm_im[l].reshape(nsamp, SSM_N),
              state_delta[l], state_conv[l])
        g_next = g1[l + 1] if l + 1 < depth else jnp.ones((1, d), F32)
        xp, hp, xs, hs, sp, ss = _layer(xp, hp, xs, hs, st, lp, g_next, rope_p, rope_s, e, batch, seq)
        for i in range(7):
            new_p[i].append(sp[i])
            new_s[i].append(ss[i])

    def stack(parts, shape):
        return jnp.stack(parts, axis=0).reshape((depth,) + shape)

    keep = min(WINDOW, seq)
    return (
        xp.reshape(batch, seq, d), xs.reshape(nsamp, 1, d),
        stack(new_p[0], (batch, keep, ATT_HKV, ATT_HD)), stack(new_p[1], (batch, keep, ATT_HKV, ATT_HD)),
        stack(new_p[2], (batch, SSM_NG, SSM_P)), stack(new_p[3], (batch, SSM_NG, SSM_P)),
        stack(new_p[4], (batch, DN_H, DN_HD, DN_HD)), stack(new_p[5], (batch, DN_CONV - 1, W_DN_QKV)),
        stack(new_p[6], (batch, CHUNK, BRANCH_W)),
        stack(new_s[0], (nsamp, past, ATT_HKV, ATT_HD)), stack(new_s[1], (nsamp, past, ATT_HKV, ATT_HD)),
        stack(new_s[2], (nsamp, SSM_NG, SSM_P)), stack(new_s[3], (nsamp, SSM_NG, SSM_P)),
        stack(new_s[4], (nsamp, DN_H, DN_HD, DN_HD)), stack(new_s[5], (nsamp, DN_CONV - 1, W_DN_QKV)),
        stack(new_s[6], (nsamp, 1, BRANCH_W)),
    )
```

```python
import jax
import jax.numpy as jnp
from jax import lax
from jax.experimental import pallas as pl
from jax.experimental.pallas import tpu as pltpu

F32 = jnp.float32
BF16 = jnp.bfloat16
HIGHEST = lax.Precision.HIGHEST

D_MODEL = 2048
DEPTH = 4
PAST_LEN = 16384
N_BRANCH = 4
BRANCH_W = D_MODEL // N_BRANCH
ATT_HD = 64
ATT_HQ = BRANCH_W // ATT_HD
ATT_HKV = 2
ATT_G = ATT_HQ // ATT_HKV
WINDOW = 128
ROPE_THETA = 10000.0
SSM_GROUP = 16
SSM_NG = BRANCH_W // SSM_GROUP
SSM_P = 64
SSM_N = SSM_NG * SSM_P
SSM_SEGS = 8
CHUNK = 128
SG_GROUPS = 4
SG_GW = BRANCH_W // SG_GROUPS
DN_HD = 128
DN_H = BRANCH_W // DN_HD
DN_CONV = 4
DN_CHUNK = 64
DN_STEP_CHUNKS = 2
D_FF = 4 * D_MODEL
EPS = 1e-6
NEG_INF = -1e30

LANES = 128
SUBLANES = 8
V7X_VMEM_LIMIT = 56 * 1024 * 1024

W_ATT = ATT_HQ * ATT_HD + 2 * ATT_HKV * ATT_HD
W_SSM = BRANCH_W
W_SG = 2 * BRANCH_W
W_DN_QKV = 3 * BRANCH_W
W_DN = W_DN_QKV + BRANCH_W + LANES
N_MAIN = W_ATT + W_SSM + W_SG + W_DN_QKV + BRANCH_W + 2 * DN_H


def _params(*sem):
    return pltpu.CompilerParams(dimension_semantics=sem, vmem_limit_bytes=V7X_VMEM_LIMIT)


def _sds(shape, dtype):
    return jax.ShapeDtypeStruct(shape, dtype)


def _dot(a, b, precision=None):
    return jnp.dot(a, b, preferred_element_type=F32, precision=precision)


def _dot_nt(a, b, precision=None):
    return lax.dot_general(a, b, (((1,), (1,)), ((), ())), preferred_element_type=F32, precision=precision)


def _dot_tn(a, b, precision=None):
    return lax.dot_general(a, b, (((0,), (0,)), ((), ())), preferred_element_type=F32, precision=precision)


def _rms(x, g):
    ms = jnp.mean(x * x, axis=-1, keepdims=True)
    return x * lax.rsqrt(ms + EPS) * g


def _split_dot(x, e):
    hi = x.astype(BF16)
    lo = (x - hi.astype(F32)).astype(BF16)
    return _dot(hi, e) + _dot(lo, e)


def _rmsnorm_body(x_ref, g_ref, o_ref):
    o_ref[...] = _rms(x_ref[...], g_ref[...]).astype(o_ref.dtype)


def _rmsnorm_call(x, g, tm):
    m, d = x.shape
    return pl.pallas_call(
        _rmsnorm_body,
        out_shape=_sds((m, d), BF16),
        grid=(m // tm,),
        in_specs=[pl.BlockSpec((tm, d), lambda i: (i, 0)), pl.BlockSpec((1, d), lambda i: (0, 0))],
        out_specs=pl.BlockSpec((tm, d), lambda i: (i, 0)),
        compiler_params=_params("parallel"),
        name="rmsnorm",
    )(x, g)


def _matmul_body(h_ref, w_ref, o_ref):
    o_ref[...] = _dot(h_ref[...], w_ref[...])


def _matmul_call(h, w, tm):
    m, k = h.shape
    n = w.shape[1]
    return pl.pallas_call(
        _matmul_body,
        out_shape=_sds((m, n), F32),
        grid=(m // tm,),
        in_specs=[pl.BlockSpec((tm, k), lambda i: (i, 0)), pl.BlockSpec((k, n), lambda i: (0, 0))],
        out_specs=pl.BlockSpec((tm, n), lambda i: (i, 0)),
        compiler_params=_params("parallel"),
        name="in_proj",
    )(h, w)


def _merge_body(h_ref, y_ref, wg_ref, wb_ref, o_ref, acc_ref):
    n = pl.program_id(1)
    gate = jax.nn.sigmoid(_dot(h_ref[...], wg_ref[0]))
    contrib = gate * _dot(y_ref[0], wb_ref[0])

    @pl.when(n == 0)
    def _():
        acc_ref[...] = contrib

    @pl.when(n > 0)
    def _():
        acc_ref[...] += contrib

    @pl.when(n == N_BRANCH - 1)
    def _():
        o_ref[...] = acc_ref[...].astype(o_ref.dtype)


def _merge_call(h, ys, wg, wb, tm):
    m, d = h.shape
    return pl.pallas_call(
        _merge_body,
        out_shape=_sds((m, d), BF16),
        grid=(m // tm, N_BRANCH),
        in_specs=[
            pl.BlockSpec((tm, d), lambda i, n: (i, 0)),
            pl.BlockSpec((1, tm, BRANCH_W), lambda i, n: (n, i, 0)),
            pl.BlockSpec((1, d, d), lambda i, n: (n, 0, 0)),
            pl.BlockSpec((1, BRANCH_W, d), lambda i, n: (n, 0, 0)),
        ],
        out_specs=pl.BlockSpec((tm, d), lambda i, n: (i, 0)),
        scratch_shapes=[pltpu.VMEM((tm, d), F32)],
        compiler_params=_params("parallel", "arbitrary"),
        name="merge",
    )(h, ys, wg, wb)


def _proj_body(m_ref, w_ref, x_ref, g_ref, xo_ref, ho_ref):
    xn = x_ref[...] + _dot(m_ref[...], w_ref[...])
    xo_ref[...] = xn
    ho_ref[...] = _rms(xn, g_ref[...]).astype(ho_ref.dtype)


def _proj_call(mix, w, x, g, tm):
    m, d = x.shape
    return pl.pallas_call(
        _proj_body,
        out_shape=(_sds((m, d), F32), _sds((m, d), BF16)),
        grid=(m // tm,),
        in_specs=[
            pl.BlockSpec((tm, d), lambda i: (i, 0)),
            pl.BlockSpec((d, d), lambda i: (0, 0)),
            pl.BlockSpec((tm, d), lambda i: (i, 0)),
            pl.BlockSpec((1, d), lambda i: (0, 0)),
        ],
        out_specs=(pl.BlockSpec((tm, d), lambda i: (i, 0)), pl.BlockSpec((tm, d), lambda i: (i, 0))),
        compiler_params=_params("parallel"),
        name="out_proj",
    )(mix, w, x, g)


def _ffn_body(h_ref, w1_ref, w2_ref, x_ref, g_ref, xo_ref, ho_ref):
    f = pl.program_id(1)
    a = _dot(h_ref[...], w1_ref[...])
    a = jnp.square(jnp.maximum(a, 0.0)).astype(BF16)
    c = _dot(a, w2_ref[...])

    @pl.when(f == 0)
    def _():
        xo_ref[...] = x_ref[...] + c

    @pl.when(f > 0)
    def _():
        xo_ref[...] += c

    @pl.when(f == pl.num_programs(1) - 1)
    def _():
        ho_ref[...] = _rms(xo_ref[...], g_ref[...]).astype(ho_ref.dtype)


def _ffn_call(h, w1, w2, x, g_next, tm, tf):
    m, d = x.shape
    dff = w1.shape[1]
    return pl.pallas_call(
        _ffn_body,
        out_shape=(_sds((m, d), F32), _sds((m, d), BF16)),
        grid=(m // tm, dff // tf),
        in_specs=[
            pl.BlockSpec((tm, d), lambda i, f: (i, 0)),
            pl.BlockSpec((d, tf), lambda i, f: (0, f)),
            pl.BlockSpec((tf, d), lambda i, f: (f, 0)),
            pl.BlockSpec((tm, d), lambda i, f: (i, 0)),
            pl.BlockSpec((1, d), lambda i, f: (0, 0)),
        ],
        out_specs=(pl.BlockSpec((tm, d), lambda i, f: (i, 0)), pl.BlockSpec((tm, d), lambda i, f: (i, 0))),
        compiler_params=_params("parallel", "arbitrary"),
        name="ffn",
    )(h, w1, w2, x, g_next)


def _head_norm_rope(x, g, cos, sin, e):
    w = x.shape[1]
    ss = _split_dot(x * x, e)
    xn = x * lax.rsqrt(ss * (1.0 / ATT_HD) + EPS) * g
    lane = lax.broadcasted_iota(jnp.int32, x.shape, 1)
    first_half = (lane % ATT_HD) < (ATT_HD // 2)
    rot = jnp.where(first_half, pltpu.roll(xn, w - ATT_HD // 2, 1), pltpu.roll(xn, ATT_HD // 2, 1))
    return xn * cos + rot * sin


def _half_placed(x):
    lane = lax.broadcasted_iota(jnp.int32, x.shape, 1)
    lo = lane < ATT_HD
    xr = pltpu.roll(x, ATT_HD, 1)
    zero = jnp.zeros_like(x)
    return ((jnp.where(lo, x, zero), jnp.where(lo, zero, xr)),
            (jnp.where(lo, xr, zero), jnp.where(lo, zero, x)))


def _attn_prompt_body(z_ref, cos_ref, sin_ref, qg_ref, kg_ref, e_ref, sink_ref,
                      y_ref, ko_ref, vo_ref, kprev_ref, vprev_ref):
    i = pl.program_id(1)

    @pl.when(i == 0)
    def _():
        kprev_ref[...] = jnp.zeros_like(kprev_ref)
        vprev_ref[...] = jnp.zeros_like(vprev_ref)

    z = z_ref[...]
    cos = cos_ref[...]
    sin = sin_ref[...]
    nq = ATT_HQ * ATT_HD
    nk = ATT_HKV * ATT_HD
    rep = nq // LANES
    q = _head_norm_rope(z[:, :nq], qg_ref[...], jnp.tile(cos, (1, rep)), jnp.tile(sin, (1, rep)), e_ref[...])
    k = _head_norm_rope(z[:, nq:nq + nk], kg_ref[...], cos, sin, e_ref[:nk, :nk])
    v = z[:, nq + nk:]
    ko_ref[0] = k
    vo_ref[0] = v
    kcat = jnp.concatenate([kprev_ref[...], k], axis=0)
    vcat = jnp.concatenate([vprev_ref[...], v], axis=0)
    kprev_ref[...] = k
    vprev_ref[...] = v
    kparts = _half_placed(kcat)
    vparts = _half_placed(vcat)
    qb = q.astype(BF16)

    rows = 2 * WINDOW
    r = lax.broadcasted_iota(jnp.int32, (rows, rows), 0) % WINDOW
    c = lax.broadcasted_iota(jnp.int32, (rows, rows), 1)
    rel = r - c + WINDOW
    mask = (rel >= 0) & (rel < WINDOW) & ((c >= WINDOW) | (i > 0))
    top = lax.broadcasted_iota(jnp.int32, (rows, 1), 0) < WINDOW

    pair_out = []
    for kv in range(ATT_HKV):
        qst = jnp.concatenate([qb[:, LANES * (2 * kv):LANES * (2 * kv + 1)],
                               qb[:, LANES * (2 * kv + 1):LANES * (2 * kv + 2)]], axis=0)
        acc = None
        for parity in range(2):
            s = _dot_nt(qst, kparts[kv][parity].astype(BF16)) * (ATT_HD ** -0.5)
            s = jnp.where(mask, s, NEG_INF)
            sink = jnp.where(top, sink_ref[ATT_G * kv + parity], sink_ref[ATT_G * kv + 2 + parity])
            mx = jnp.maximum(jnp.max(s, axis=-1, keepdims=True), sink)
            ex = jnp.exp(s - mx)
            den = jnp.sum(ex, axis=-1, keepdims=True) + jnp.exp(sink - mx)
            p = (ex / den).astype(BF16)
            o = _dot(p, vparts[kv][parity].astype(BF16))
            acc = o if acc is None else acc + o
        pair_out += [acc[:WINDOW], acc[WINDOW:]]
    y_ref[...] = jnp.concatenate(pair_out, axis=1).astype(y_ref.dtype)


def _attn_prompt_call(z_att, cos, sin, qg, kg, e, sink, batch, seq):
    nb = seq // WINDOW
    nkv = ATT_HKV * ATT_HD
    return pl.pallas_call(
        _attn_prompt_body,
        out_shape=(_sds((batch * seq, BRANCH_W), BF16), _sds((batch, WINDOW, nkv), F32),
                   _sds((batch, WINDOW, nkv), F32)),
        grid=(batch, nb),
        in_specs=[
            pl.BlockSpec((WINDOW, W_ATT), lambda b, i: (b * nb + i, 0)),
            pl.BlockSpec((WINDOW, LANES), lambda b, i: (i, 0)),
            pl.BlockSpec((WINDOW, LANES), lambda b, i: (i, 0)),
            pl.BlockSpec((1, BRANCH_W), lambda b, i: (0, 0)),
            pl.BlockSpec((1, nkv), lambda b, i: (0, 0)),
            pl.BlockSpec((BRANCH_W, BRANCH_W), lambda b, i: (0, 0)),
            pl.BlockSpec(memory_space=pltpu.SMEM),
        ],
        out_specs=(
            pl.BlockSpec((WINDOW, BRANCH_W), lambda b, i: (b * nb + i, 0)),
            pl.BlockSpec((1, WINDOW, nkv), lambda b, i: (b, 0, 0)),
            pl.BlockSpec((1, WINDOW, nkv), lambda b, i: (b, 0, 0)),
        ),
        scratch_shapes=[pltpu.VMEM((WINDOW, nkv), F32), pltpu.VMEM((WINDOW, nkv), F32)],
        compiler_params=_params("arbitrary", "arbitrary"),
        name="attn_prompt",
    )(z_att, cos, sin, qg, kg, e, sink)


def _attn_sample_body(z_ref, ck_ref, cv_ref, cos_ref, sin_ref, qg_ref, kg_ref, e_ref, sink_ref,
                      y_ref, ko_ref, vo_ref):
    z = z_ref[0]
    cos = cos_ref[...]
    sin = sin_ref[...]
    nq = ATT_HQ * ATT_HD
    nk = ATT_HKV * ATT_HD
    rep = nq // LANES
    q = _head_norm_rope(z[:, :nq], qg_ref[...], jnp.tile(cos, (1, rep)), jnp.tile(sin, (1, rep)), e_ref[...])
    k = _head_norm_rope(z[:, nq:nq + nk], kg_ref[...], cos, sin, e_ref[:nk, :nk])
    v = z[:, nq + nk:]
    ck = ck_ref[0]
    cv = cv_ref[0]
    ko_ref[0, 0:WINDOW - 1, :] = ck_ref[0, 1:WINDOW, :]
    ko_ref[0, WINDOW - 1:WINDOW, :] = k
    vo_ref[0, 0:WINDOW - 1, :] = cv_ref[0, 1:WINDOW, :]
    vo_ref[0, WINDOW - 1:WINDOW, :] = v

    lane = lax.broadcasted_iota(jnp.int32, (1, LANES), 1)
    lo = lane < ATT_HD
    row = lax.broadcasted_iota(jnp.int32, (ATT_HQ, LANES), 0)
    qexp = jnp.zeros((ATT_HQ, LANES), F32)
    for h in range(ATT_HQ):
        pair = q[:, LANES * (h // 2):LANES * (h // 2 + 1)]
        want_lo = (h // ATT_G) == 0
        have_lo = (h % 2) == 0
        src = pair if want_lo == have_lo else pltpu.roll(pair, ATT_HD, 1)
        placed = jnp.where(lo if want_lo else jnp.logical_not(lo), src, 0.0)
        qexp = jnp.where(row == h, jnp.broadcast_to(placed, (ATT_HQ, LANES)), qexp)

    scale = ATT_HD ** -0.5
    s = _dot_nt(qexp.astype(BF16), ck.astype(BF16)) * scale
    key = lax.broadcasted_iota(jnp.int32, s.shape, 1)
    s = jnp.where(key >= 1, s, NEG_INF)
    qr = qexp.astype(BF16).astype(F32)
    kr = k.astype(BF16).astype(F32)
    s_new = jnp.sum(qr * kr, axis=-1, keepdims=True) * scale
    sink = sink_ref[:, 0:1]
    mx = jnp.maximum(jnp.maximum(jnp.max(s, axis=-1, keepdims=True), s_new), sink)
    ex = jnp.exp(s - mx)
    ex_new = jnp.exp(s_new - mx)
    den = jnp.sum(ex, axis=-1, keepdims=True) + ex_new + jnp.exp(sink - mx)
    p = (ex / den).astype(BF16)
    p_new = (ex_new / den).astype(BF16).astype(F32)
    o = _dot(p, cv.astype(BF16)) + p_new * v.astype(BF16).astype(F32)

    outs = []
    for j in range(ATT_HQ // 2):
        parts = []
        for par in range(2):
            h = 2 * j + par
            orow = o[h:h + 1, :]
            have_lo = (h // ATT_G) == 0
            want_lo = par == 0
            src = orow if want_lo == have_lo else pltpu.roll(orow, ATT_HD, 1)
            parts.append(jnp.where(lo if want_lo else jnp.logical_not(lo), src, 0.0))
        outs.append(parts[0] + parts[1])
    y_ref[0] = jnp.concatenate(outs, axis=1).astype(y_ref.dtype)


def _attn_sample_call(z_att, ck, cv, cos, sin, qg, kg, e, sink_b):
    nbatch = z_att.shape[0]
    nkv = ATT_HKV * ATT_HD
    z3 = z_att.reshape(nbatch, 1, W_ATT)
    y, ko, vo = pl.pallas_call(
        _attn_sample_body,
        out_shape=(_sds((nbatch, 1, BRANCH_W), BF16), _sds((nbatch, WINDOW, nkv), F32),
                   _sds((nbatch, WINDOW, nkv), F32)),
        grid=(nbatch,),
        in_specs=[
            pl.BlockSpec((1, 1, W_ATT), lambda b: (b, 0, 0)),
            pl.BlockSpec((1, WINDOW, nkv), lambda b: (b, 0, 0)),
            pl.BlockSpec((1, WINDOW, nkv), lambda b: (b, 0, 0)),
            pl.BlockSpec((1, LANES), lambda b: (0, 0)),
            pl.BlockSpec((1, LANES), lambda b: (0, 0)),
            pl.BlockSpec((1, BRANCH_W), lambda b: (0, 0)),
            pl.BlockSpec((1, nkv), lambda b: (0, 0)),
            pl.BlockSpec((BRANCH_W, BRANCH_W), lambda b: (0, 0)),
            pl.BlockSpec((ATT_HQ, LANES), lambda b: (0, 0)),
        ],
        out_specs=(
            pl.BlockSpec((1, 1, BRANCH_W), lambda b: (b, 0, 0)),
            pl.BlockSpec((1, WINDOW, nkv), lambda b: (b, 0, 0)),
            pl.BlockSpec((1, WINDOW, nkv), lambda b: (b, 0, 0)),
        ),
        compiler_params=_params("parallel"),
        name="attn_sample",
    )(z3, ck, cv, cos, sin, qg, kg, e, sink_b)
    return y.reshape(nbatch, BRANCH_W), ko, vo


def _cmul(ar, ai, br, bi):
    return ar * br - ai * bi, ar * bi + ai * br


def _ssm_tail(y, u, d_ref, gw_ref, gb_ref):
    y = jax.nn.gelu(y + d_ref[...] * u)
    return y * jax.nn.sigmoid(_dot(y.astype(BF16), gw_ref[...]) + gb_ref[...])


def _ssm_prompt_body(u_ref, bm_ref, cm_ref, a_ref, d_ref, gw_ref, gb_ref,
                     y_ref, sre_ref, sim_ref,
                     u3_ref, up_ref, bu_ref, start_ref, carry_ref, yp_ref):
    tb = u_ref.shape[0]
    ts = tb // SSM_SEGS
    nlc = BRANCH_W // LANES
    j = pl.program_id(1)

    @pl.when(j == 0)
    def _():
        carry_ref[...] = jnp.zeros_like(carry_ref)

    for c in range(nlc):
        u3_ref[c] = u_ref[:, c * LANES:(c + 1) * LANES]

    def gather(t, carry):
        for c in range(nlc):
            up_ref[pl.ds(pl.multiple_of(t * SUBLANES, SUBLANES), SUBLANES), c * LANES:(c + 1) * LANES] = (
                u3_ref[c, pl.ds(t, SUBLANES, stride=ts), :])
        return carry

    lax.fori_loop(0, ts, gather, 0)
    up = up_ref[...]
    bu_ref[...] = _dot(up.astype(BF16), bm_ref[...])

    cw = 4 * LANES
    nsweep = SSM_N // cw

    def scan(c0, init_re, init_im, store):
        ar = jnp.broadcast_to(a_ref[0:1, c0:c0 + cw], (SUBLANES, cw))
        ai = jnp.broadcast_to(a_ref[1:2, c0:c0 + cw], (SUBLANES, cw))

        def step(t, st):
            sr, si = st
            r0 = pl.multiple_of(t * SUBLANES, SUBLANES)
            br = bu_ref[pl.ds(r0, SUBLANES), c0:c0 + cw]
            bi = bu_ref[pl.ds(r0, SUBLANES), SSM_N + c0:SSM_N + c0 + cw]
            pr, pi = _cmul(ar, ai, sr, si)
            sr, si = pr + br, pi + bi
            if store:
                bu_ref[pl.ds(r0, SUBLANES), c0:c0 + cw] = sr
                bu_ref[pl.ds(r0, SUBLANES), SSM_N + c0:SSM_N + c0 + cw] = si
            return sr, si

        return lax.fori_loop(0, ts, step, (init_re, init_im))

    zero = jnp.zeros((SUBLANES, cw), F32)
    for sw in range(nsweep):
        c0 = sw * cw
        er, ei = scan(c0, zero, zero, False)
        pr, pi = a_ref[0:1, c0:c0 + cw], a_ref[1:2, c0:c0 + cw]
        n = 1
        while n < ts:
            pr, pi = _cmul(pr, pi, pr, pi)
            n *= 2
        sr = carry_ref[0:1, c0:c0 + cw]
        si = carry_ref[1:2, c0:c0 + cw]
        for s in range(SSM_SEGS):
            start_ref[s:s + 1, c0:c0 + cw] = sr
            start_ref[s:s + 1, SSM_N + c0:SSM_N + c0 + cw] = si
            qr, qi = _cmul(pr, pi, sr, si)
            sr, si = qr + er[s:s + 1, :], qi + ei[s:s + 1, :]
        carry_ref[0:1, c0:c0 + cw] = sr
        carry_ref[1:2, c0:c0 + cw] = si
        scan(c0, start_ref[:, c0:c0 + cw], start_ref[:, SSM_N + c0:SSM_N + c0 + cw], True)

    sre_ref[0] = carry_ref[0:1, :]
    sim_ref[0] = carry_ref[1:2, :]

    y = _dot(bu_ref[...].astype(BF16), cm_ref[...])
    yp_ref[...] = _ssm_tail(y, up, d_ref, gw_ref, gb_ref)

    def scatter(t, carry):
        for c in range(nlc):
            u3_ref[c, pl.ds(t, SUBLANES, stride=ts), :] = (
                yp_ref[pl.ds(pl.multiple_of(t * SUBLANES, SUBLANES), SUBLANES), c * LANES:(c + 1) * LANES])
        return carry

    lax.fori_loop(0, ts, scatter, 0)
    for c in range(nlc):
        y_ref[:, c * LANES:(c + 1) * LANES] = u3_ref[c].astype(y_ref.dtype)


def _ssm_prompt_call(z_u, bm, cm, a, d, gw, gb, batch, seq, tb):
    nblk = seq // tb
    const = lambda b, j: (0, 0)
    return pl.pallas_call(
        _ssm_prompt_body,
        out_shape=(_sds((batch * seq, BRANCH_W), BF16), _sds((batch, 1, SSM_N), F32), _sds((batch, 1, SSM_N), F32)),
        grid=(batch, nblk),
        in_specs=[
            pl.BlockSpec((tb, BRANCH_W), lambda b, j: (b * nblk + j, 0)),
            pl.BlockSpec((BRANCH_W, 2 * SSM_N), const),
            pl.BlockSpec((2 * SSM_N, BRANCH_W), const),
            pl.BlockSpec((2, SSM_N), const),
            pl.BlockSpec((1, BRANCH_W), const),
            pl.BlockSpec((BRANCH_W, BRANCH_W), const),
            pl.BlockSpec((1, BRANCH_W), const),
        ],
        out_specs=(
            pl.BlockSpec((tb, BRANCH_W), lambda b, j: (b * nblk + j, 0)),
            pl.BlockSpec((1, 1, SSM_N), lambda b, j: (b, 0, 0)),
            pl.BlockSpec((1, 1, SSM_N), lambda b, j: (b, 0, 0)),
        ),
        scratch_shapes=[
            pltpu.VMEM((BRANCH_W // LANES, tb, LANES), F32),
            pltpu.VMEM((tb, BRANCH_W), F32),
            pltpu.VMEM((tb, 2 * SSM_N), F32),
            pltpu.VMEM((SSM_SEGS, 2 * SSM_N), F32),
            pltpu.VMEM((2, SSM_N), F32),
            pltpu.VMEM((tb, BRANCH_W), F32),
        ],
        compiler_params=_params("arbitrary", "arbitrary"),
        name="ssm_prompt",
    )(z_u, bm, cm, a, d, gw, gb)


def _ssm_sample_body(u_ref, s0r_ref, s0i_ref, bm_ref, cm_ref, a_ref, d_ref, gw_ref, gb_ref,
                     y_ref, sre_ref, sim_ref):
    u = u_ref[...]
    bu = _dot(u.astype(BF16), bm_ref[...])
    pr, pi = _cmul(a_ref[0:1, :], a_ref[1:2, :], s0r_ref[...], s0i_ref[...])
    sr = pr + bu[:, :SSM_N]
    si = pi + bu[:, SSM_N:]
    sre_ref[...] = sr
    sim_ref[...] = si
    y = _dot(jnp.concatenate([sr, si], axis=1).astype(BF16), cm_ref[...])
    y_ref[...] = _ssm_tail(y, u, d_ref, gw_ref, gb_ref).astype(y_ref.dtype)


def _ssm_sample_call(z_u, s0r, s0i, bm, cm, a, d, gw, gb):
    nbatch = z_u.shape[0]
    return pl.pallas_call(
        _ssm_sample_body,
        out_shape=(_sds((nbatch, BRANCH_W), BF16), _sds((nbatch, SSM_N), F32), _sds((nbatch, SSM_N), F32)),
        compiler_params=pltpu.CompilerParams(vmem_limit_bytes=V7X_VMEM_LIMIT),
        name="ssm_sample",
    )(z_u, s0r, s0i, bm, cm, a, d, gw, gb)


def _sg_prompt_body(z_ref, g_ref, w_ref, b_ref, y_ref, v_ref):
    uv = jax.nn.gelu(z_ref[...])
    u = uv[:, :BRANCH_W]
    v = _rms(uv[:, BRANCH_W:], g_ref[...])
    v_ref[0] = v
    t = lax.broadcasted_iota(jnp.int32, (CHUNK, CHUNK), 0)
    s = lax.broadcasted_iota(jnp.int32, (CHUNK, CHUNK), 1)
    causal = t >= s
    vb = v.astype(BF16)
    mixed = []
    for g in range(SG_GROUPS):
        w = jnp.where(causal, w_ref[g], 0.0).astype(BF16)
        mixed.append(_dot(w, vb[:, g * SG_GW:(g + 1) * SG_GW]))
    mixed = jnp.concatenate(mixed, axis=1) + b_ref[...]
    y_ref[...] = (u * mixed).astype(y_ref.dtype)


def _sg_prompt_call(z_c, g, w, b_exp, batch, seq):
    nch = seq // CHUNK
    return pl.pallas_call(
        _sg_prompt_body,
        out_shape=(_sds((batch * seq, BRANCH_W), BF16), _sds((batch, CHUNK, BRANCH_W), F32)),
        grid=(batch, nch),
        in_specs=[
            pl.BlockSpec((CHUNK, W_SG), lambda b, i: (b * nch + i, 0)),
            pl.BlockSpec((1, BRANCH_W), lambda b, i: (0, 0)),
            pl.BlockSpec((SG_GROUPS, CHUNK, CHUNK), lambda b, i: (0, 0, 0)),
            pl.BlockSpec((CHUNK, BRANCH_W), lambda b, i: (0, 0)),
        ],
        out_specs=(
            pl.BlockSpec((CHUNK, BRANCH_W), lambda b, i: (b * nch + i, 0)),
            pl.BlockSpec((1, CHUNK, BRANCH_W), lambda b, i: (b, 0, 0)),
        ),
        compiler_params=_params("arbitrary", "arbitrary"),
        name="sg_prompt",
    )(z_c, g, w, b_exp)


def _sg_sample_body(z_ref, g_ref, w00_ref, b_ref, y_ref, v_ref):
    uv = jax.nn.gelu(z_ref[...])
    u = uv[:, :BRANCH_W]
    v = _rms(uv[:, BRANCH_W:], g_ref[...])
    v_ref[...] = v
    w = w00_ref[...].astype(BF16).astype(F32)
    mixed = w * v.astype(BF16).astype(F32) + b_ref[0:1, :]
    y_ref[...] = (u * mixed).astype(y_ref.dtype)


def _sg_sample_call(z_c, g, w00, b_exp):
    nbatch = z_c.shape[0]
    return pl.pallas_call(
        _sg_sample_body,
        out_shape=(_sds((nbatch, BRANCH_W), BF16), _sds((nbatch, BRANCH_W), F32)),
        name="sg_sample",
    )(z_c, g, w00, b_exp)


def _l2n(x):
    return x * lax.rsqrt(jnp.sum(x * x, axis=-1, keepdims=True) + EPS)


def _dn_gates(zab, alog_ref, dtb_ref):
    g = -jnp.exp(alog_ref[...]) * jax.nn.softplus(zab + dtb_ref[...])
    beta = jax.nn.sigmoid(zab)
    return g, beta


def _split2(x):
    hi = x.astype(BF16)
    return hi, (x - hi.astype(F32)).astype(BF16)


def _mm3(a, b):
    (ah, al), (bh, bl) = a, b
    return _dot(ah, bh) + (_dot(ah, bl) + _dot(al, bh))


def _unit_lower_inverse_off(a, blk):
    n = a.shape[0]
    r = lax.broadcasted_iota(jnp.int32, (n, n), 0)
    c = lax.broadcasted_iota(jnp.int32, (n, n), 1)

    def lower_left(size):
        return ((r // (2 * size)) == (c // (2 * size))) & ((r // size) % 2 == 1) & ((c // size) % 2 == 0)

    off = -jnp.where(lower_left(1), a, 0.0)
    size = 2
    while size < blk:
        am = jnp.where(lower_left(size), a, 0.0)
        ob = off.astype(BF16)
        p = am + _dot(ob, am.astype(BF16))
        off = off - (p + _dot(p.astype(BF16), ob))
        size *= 2
    return off


def _head_lanes(x, rows_per_head):
    r = lax.broadcasted_iota(jnp.int32, x.shape, 0) // rows_per_head
    return jnp.concatenate([jnp.where(r == h, x, 0.0) for h in range(DN_H)], axis=1)


def _dn_chunk_terms(qkv, zg, g_all, beta_all, row0):
    tc = DN_CHUNK
    n = DN_H * tc
    rows = slice(row0, row0 + tc)

    def stack(x, off):
        return jnp.concatenate([x[rows, off + h * DN_HD:off + (h + 1) * DN_HD] for h in range(DN_H)], axis=0)

    q = _l2n(stack(qkv, 0)) * (DN_HD ** -0.5)
    k = _l2n(stack(qkv, BRANCH_W))
    v = stack(qkv, 2 * BRANCH_W)
    zgs = stack(zg, 0)

    rt = lax.broadcasted_iota(jnp.int32, (tc, tc), 0)
    ct = lax.broadcasted_iota(jnp.int32, (tc, tc), 1)
    ones_tril = jnp.where(rt >= ct, 1.0, 0.0).astype(BF16)
    g = g_all[rows]
    g_hi, g_lo = _split2(g)
    g_lo2 = (g - g_hi.astype(F32) - g_lo.astype(F32)).astype(BF16)
    gc_c = _dot(ones_tril, g_hi) + (_dot(ones_tril, g_lo) + _dot(ones_tril, g_lo2))
    gc = jnp.concatenate([gc_c[:, h:h + 1] for h in range(DN_H)], axis=0)
    beta = jnp.concatenate([beta_all[rows, DN_H + h:DN_H + h + 1] for h in range(DN_H)], axis=0)
    gc_last = jnp.concatenate([jnp.broadcast_to(gc_c[tc - 1:tc, h:h + 1], (tc, 1)) for h in range(DN_H)], axis=0)

    r = lax.broadcasted_iota(jnp.int32, (n, n), 0)
    c = lax.broadcasted_iota(jnp.int32, (n, n), 1)
    same = (r // tc) == (c // tc)
    tril = same & (r >= c)
    stril = same & (r > c)
    gmat = jnp.broadcast_to(gc, (n, n))
    dec = jnp.where(tril, jnp.exp(jnp.where(tril, gmat - gmat.T, 0.0)), 0.0)
    kb = k.astype(BF16)
    a = jnp.where(stril, beta * _dot_nt(kb, kb) * dec, 0.0)
    inv_off = _unit_lower_inverse_off(a, tc)
    eg = jnp.exp(gc)
    rhs = jnp.concatenate([beta * v, (beta * eg) * k], axis=1)
    sol = rhs + _mm3(_split2(inv_off), _split2(rhs))
    u_t, w_k = sol[:, :DN_HD], sol[:, DN_HD:]
    qk = (_dot_nt(q.astype(BF16), kb) * dec).astype(BF16)
    wq = jnp.concatenate([_head_lanes(w_k, tc), _head_lanes(q * eg, tc)], axis=0).astype(BF16)
    ke_t = _head_lanes(k * jnp.exp(gc_last - gc), tc).T.astype(BF16)
    ge = jnp.concatenate([jnp.broadcast_to(jnp.exp(gc_c[tc - 1:tc, h:h + 1]), (DN_HD, 1)) for h in range(DN_H)],
                         axis=0)
    return u_t, wq, qk, ke_t, ge, zgs


def _dn_chunk_step(s, terms, ng):
    u_t, wq, qk, ke_t, ge, zgs = terms
    n = u_t.shape[0]
    ws = _dot(wq, s.astype(BF16))
    u = u_t - ws[:n]
    ub = u.astype(BF16)
    o = ws[n:] + _dot(qk, ub)
    s = ge * s + _dot(ke_t, ub)
    return s, _rms(o, ng) * jax.nn.silu(zgs)


def _dn_prompt_body(x_ref, zg_ref, zab_ref, cw_ref, alog_ref, dtb_ref, ng_ref,
                    y_ref, so_ref, co_ref, xin_ref, s_ref):
    step = pl.program_id(0)
    nb, tb = x_ref.shape[0], x_ref.shape[1]
    tc = DN_CHUNK
    pad = SUBLANES

    @pl.when(step == 0)
    def _():
        xin_ref[:, 0:pad, :] = jnp.zeros((nb, pad, W_DN_QKV), F32)
        s_ref[...] = jnp.zeros_like(s_ref)

    @pl.when(step > 0)
    def _():
        xin_ref[:, 0:pad, :] = xin_ref[:, tb:tb + pad, :]

    terms = []
    for b in range(nb):
        x = x_ref[b]
        xin_ref[b, pad:pad + tb, :] = x
        co_ref[b] = x_ref[b, tb - (DN_CONV - 1):tb, :]
        acc = cw_ref[DN_CONV - 1:DN_CONV, :] * x
        for i in range(DN_CONV - 1):
            lag = DN_CONV - 1 - i
            acc = acc + cw_ref[i:i + 1, :] * xin_ref[b, pad - lag:pad - lag + tb, :]
        qkv = jax.nn.silu(acc)
        g_all, beta_all = _dn_gates(zab_ref[b], alog_ref, dtb_ref)
        zg = zg_ref[b]
        terms.append([_dn_chunk_terms(qkv, zg, g_all, beta_all, ci * tc) for ci in range(tb // tc)])

    for b in range(nb):
        s = s_ref[b]
        for ci, t in enumerate(terms[b]):
            s, o = _dn_chunk_step(s, t, ng_ref[...])
            y_ref[b, ci * tc:(ci + 1) * tc, :] = jnp.concatenate(
                [o[h * tc:(h + 1) * tc] for h in range(DN_H)], axis=1).astype(y_ref.dtype)
        s_ref[b] = s

    @pl.when(step == pl.num_programs(0) - 1)
    def _():
        so_ref[...] = s_ref[...]


def _dn_prompt_call(z_dn, cw, alog, dtb, ng, batch, seq):
    tb = DN_STEP_CHUNKS * DN_CHUNK if seq % (DN_STEP_CHUNKS * DN_CHUNK) == 0 else DN_CHUNK
    z3 = z_dn.reshape(batch, seq, W_DN)
    const = lambda i: (0, 0)
    whole = lambda i: (0, 0, 0)
    y, so, co = pl.pallas_call(
        _dn_prompt_body,
        out_shape=(_sds((batch, seq, BRANCH_W), BF16), _sds((batch, DN_H * DN_HD, DN_HD), F32),
                   _sds((batch, DN_CONV - 1, W_DN_QKV), F32)),
        grid=(seq // tb,),
        in_specs=[
            pl.BlockSpec((batch, tb, W_DN_QKV), lambda i: (0, i, 0)),
            pl.BlockSpec((batch, tb, BRANCH_W), lambda i: (0, i, W_DN_QKV // BRANCH_W)),
            pl.BlockSpec((batch, tb, LANES), lambda i: (0, i, (W_DN_QKV + BRANCH_W) // LANES)),
            pl.BlockSpec((DN_CONV, W_DN_QKV), const),
            pl.BlockSpec((1, LANES), const),
            pl.BlockSpec((1, LANES), const),
            pl.BlockSpec((1, DN_HD), const),
        ],
        out_specs=(
            pl.BlockSpec((batch, tb, BRANCH_W), lambda i: (0, i, 0)),
            pl.BlockSpec((batch, DN_H * DN_HD, DN_HD), whole),
            pl.BlockSpec((batch, DN_CONV - 1, W_DN_QKV), whole),
        ),
        scratch_shapes=[pltpu.VMEM((batch, tb + SUBLANES, W_DN_QKV), F32),
                        pltpu.VMEM((batch, DN_H * DN_HD, DN_HD), F32)],
        compiler_params=_params("arbitrary"),
        name="dn_prompt",
    )(z3, z3, z3, cw, alog, dtb, ng)
    return y.reshape(batch * seq, BRANCH_W), so.reshape(batch, DN_H, DN_HD, DN_HD), co


def _dn_sample_body(x_ref, zg_ref, zab_ref, conv_ref, s0_ref, cw_ref, alog_ref, dtb_ref, ng_ref,
                    y_ref, so_ref, co_ref):
    x = x_ref[0]
    co_ref[0, 0:DN_CONV - 2, :] = conv_ref[0, 1:DN_CONV - 1, :]
    co_ref[0, DN_CONV - 2:DN_CONV - 1, :] = x
    acc = cw_ref[DN_CONV - 1:DN_CONV, :] * x
    for i in range(DN_CONV - 1):
        acc = acc + cw_ref[i:i + 1, :] * conv_ref[0, i:i + 1, :]
    qkv = jax.nn.silu(acc)
    g_all, beta_all = _dn_gates(zab_ref[0], alog_ref, dtb_ref)
    zg = zg_ref[0]
    r = lax.broadcasted_iota(jnp.int32, (DN_HD, DN_HD), 0)
    c = lax.broadcasted_iota(jnp.int32, (DN_HD, DN_HD), 1)
    eye = r == c
    row8 = lax.broadcasted_iota(jnp.int32, (SUBLANES, DN_HD), 0)
    outs = []
    for h in range(DN_H):
        q = _l2n(qkv[:, h * DN_HD:(h + 1) * DN_HD]) * (DN_HD ** -0.5)
        k = _l2n(qkv[:, BRANCH_W + h * DN_HD:BRANCH_W + (h + 1) * DN_HD])
        v = qkv[:, 2 * BRANCH_W + h * DN_HD:2 * BRANCH_W + (h + 1) * DN_HD]
        eg = jnp.exp(g_all[:, h:h + 1])
        beta = beta_all[:, DN_H + h:DN_H + h + 1]
        s = s0_ref[0, h]
        sb = s.astype(BF16)
        kq = jnp.where(row8 == 0, jnp.broadcast_to((beta * eg) * k, (SUBLANES, DN_HD)),
                       jnp.where(row8 == 1, jnp.broadcast_to(q * eg, (SUBLANES, DN_HD)), 0.0))
        ks_qs = _dot(kq.astype(BF16), sb)
        u = beta * v - ks_qs[0:1, :]
        qk = jnp.sum(q.astype(BF16).astype(F32) * k.astype(BF16).astype(F32), axis=-1, keepdims=True)
        o = ks_qs[1:2, :] + qk * u
        kdiag = jnp.where(eye, jnp.broadcast_to(k, (DN_HD, DN_HD)), 0.0)
        outer = _dot(kdiag, jnp.broadcast_to(u, (DN_HD, DN_HD)), HIGHEST)
        so_ref[0, h] = eg * s + outer
        o = _rms(o, ng_ref[...]) * jax.nn.silu(zg[:, h * DN_HD:(h + 1) * DN_HD])
        outs.append(o)
    y_ref[0] = jnp.concatenate(outs, axis=1).astype(y_ref.dtype)


def _dn_sample_call(z_dn, conv0, s0, cw, alog, dtb, ng):
    nbatch = z_dn.shape[0]
    x3 = z_dn[:, :W_DN_QKV].reshape(nbatch, 1, W_DN_QKV)
    zg3 = z_dn[:, W_DN_QKV:W_DN_QKV + BRANCH_W].reshape(nbatch, 1, BRANCH_W)
    zab3 = z_dn[:, W_DN_QKV + BRANCH_W:].reshape(nbatch, 1, LANES)
    const = lambda b: (0, 0)
    y, so, co = pl.pallas_call(
        _dn_sample_body,
        out_shape=(_sds((nbatch, 1, BRANCH_W), BF16), _sds((nbatch, DN_H, DN_HD, DN_HD), F32),
                   _sds((nbatch, DN_CONV - 1, W_DN_QKV), F32)),
        grid=(nbatch,),
        in_specs=[
            pl.BlockSpec((1, 1, W_DN_QKV), lambda b: (b, 0, 0)),
            pl.BlockSpec((1, 1, BRANCH_W), lambda b: (b, 0, 0)),
            pl.BlockSpec((1, 1, LANES), lambda b: (b, 0, 0)),
            pl.BlockSpec((1, DN_CONV - 1, W_DN_QKV), lambda b: (b, 0, 0)),
            pl.BlockSpec((1, DN_H, DN_HD, DN_HD), lambda b: (b, 0, 0, 0)),
            pl.BlockSpec((DN_CONV, W_DN_QKV), const),
            pl.BlockSpec((1, LANES), const),
            pl.BlockSpec((1, LANES), const),
            pl.BlockSpec((1, DN_HD), const),
        ],
        out_specs=(
            pl.BlockSpec((1, 1, BRANCH_W), lambda b: (b, 0, 0)),
            pl.BlockSpec((1, DN_H, DN_HD, DN_HD), lambda b: (b, 0, 0, 0)),
            pl.BlockSpec((1, DN_CONV - 1, W_DN_QKV), lambda b: (b, 0, 0)),
        ),
        compiler_params=_params("parallel"),
        name="dn_sample",
    )(x3, zg3, zab3, conv0, s0, cw, alog, dtb, ng)
    return y.reshape(nbatch, BRANCH_W), so, co


def _rope_tables(pos):
    half = ATT_HD // 2
    inv = ROPE_THETA ** (-jnp.arange(half, dtype=F32) / half)
    ang = pos.astype(F32)[:, None] * inv[None, :]
    cos, sin = jnp.cos(ang), jnp.sin(ang)
    reps = LANES // ATT_HD
    return (jnp.tile(jnp.concatenate([cos, cos], axis=1), (1, reps)),
            jnp.tile(jnp.concatenate([-sin, sin], axis=1), (1, reps)))


def _ssm_matrices(lam_re, lam_im, log_dt, b_re, b_im, c_re, c_im):
    dt = jnp.exp(log_dt)[:, None]
    mag = jnp.exp(lam_re * dt)
    a_re, a_im = mag * jnp.cos(lam_im * dt), mag * jnp.sin(lam_im * dt)
    den = lam_re * lam_re + lam_im * lam_im
    f_re = ((a_re - 1.0) * lam_re + a_im * lam_im) / den
    f_im = (a_im * lam_re - (a_re - 1.0) * lam_im) / den
    bb_re = f_re[..., None] * b_re - f_im[..., None] * b_im
    bb_im = f_re[..., None] * b_im + f_im[..., None] * b_re
    eye = jnp.eye(SSM_NG, dtype=F32)

    def in_layout(b):
        return jnp.einsum("gpc,gh->gchp", b, eye).reshape(BRANCH_W, SSM_N)

    def out_layout(c):
        return jnp.einsum("gcp,gh->gphc", c, eye).reshape(SSM_N, BRANCH_W)

    bm = jnp.concatenate([in_layout(bb_re), in_layout(bb_im)], axis=1).astype(BF16)
    cm = jnp.concatenate([out_layout(c_re), out_layout(-c_im)], axis=0).astype(BF16)
    a = jnp.stack([a_re.reshape(SSM_N), a_im.reshape(SSM_N)], axis=0)
    return bm, cm, a


def _lane_row(x):
    return jnp.zeros((1, LANES), F32).at[0, :x.shape[0]].set(x)


def _layer_params(l, p):
    w_in = p["w_in"][l]
    w_main = w_in[:, :N_MAIN].astype(BF16)
    o_att, o_ssm, o_sg, o_dn = 0, W_ATT, W_ATT + W_SSM, W_ATT + W_SSM + W_SG
    w_dn = jnp.concatenate([w_main[:, o_dn:], jnp.zeros((D_MODEL, W_DN - (N_MAIN - o_dn)), BF16)], axis=1)
    bm, cm, a = _ssm_matrices(p["ssm_lam_re"][l], p["ssm_lam_im"][l], p["ssm_log_dt"][l], p["ssm_b_re"][l],
                              p["ssm_b_im"][l], p["ssm_c_re"][l], p["ssm_c_im"][l])
    return dict(
        w_att=w_main[:, o_att:o_ssm], w_ssm=w_main[:, o_ssm:o_sg], w_sg=w_main[:, o_sg:o_dn], w_dn=w_dn,
        w_gate=jnp.transpose(w_in[:, N_MAIN:].reshape(D_MODEL, N_BRANCH, D_MODEL), (1, 0, 2)).astype(BF16),
        w_branch=p["w_branch"][l].astype(BF16), w_out=p["w_out"][l].astype(BF16),
        w_ff1=p["w_ff1"][l].astype(BF16), w_ff2=p["w_ff2"][l].astype(BF16),
        norm2_g=p["norm2_g"][l].reshape(1, D_MODEL),
        qg=jnp.tile(p["att_qn_g"][l], ATT_HQ).reshape(1, BRANCH_W),
        kg=jnp.tile(p["att_kn_g"][l], ATT_HKV).reshape(1, ATT_HKV * ATT_HD),
        sink=p["att_sink"][l],
        sink_b=jnp.broadcast_to(p["att_sink"][l][:, None], (ATT_HQ, LANES)),
        bm=bm, cm=cm, a=a, ssm_d=p["ssm_d"][l].reshape(1, BRANCH_W),
        glu_w=p["ssm_glu_w"][l].astype(BF16), glu_b=p["ssm_glu_b"][l].reshape(1, BRANCH_W),
        sg_g=p["sg_norm_g"][l].reshape(1, BRANCH_W), sg_w=p["sg_w"][l],
        sg_b=jnp.repeat(jnp.transpose(p["sg_b"][l]), SG_GW, axis=1),
        sg_w00=jnp.repeat(p["sg_w"][l][:, 0, 0], SG_GW).reshape(1, BRANCH_W),
        cw=p["dn_conv_w"][l], alog=_lane_row(p["dn_a_log"][l]), dtb=_lane_row(p["dn_dt_bias"][l]),
        ng=p["dn_norm_g"][l].reshape(1, DN_HD),
    )


def _tile_rows(m):
    return 512 if m % 512 == 0 else m


def _layer(xp, hp, xs, hs, st, lp, g_next, rope_p, rope_s, e, batch, seq):
    tmp, tms = _tile_rows(xp.shape[0]), _tile_rows(xs.shape[0])
    ck, cv, s0r, s0i, s0d, conv0 = st

    ya, pk, pv = _attn_prompt_call(_matmul_call(hp, lp["w_att"], tmp), rope_p[0], rope_p[1], lp["qg"], lp["kg"], e,
                                   lp["sink"], batch, seq)
    tb = 512 if seq % 512 == 0 else seq
    yb, pre, pim = _ssm_prompt_call(_matmul_call(hp, lp["w_ssm"], tmp), lp["bm"], lp["cm"], lp["a"], lp["ssm_d"],
                                    lp["glu_w"], lp["glu_b"], batch, seq, tb)
    yc, pchunk = _sg_prompt_call(_matmul_call(hp, lp["w_sg"], tmp), lp["sg_g"], lp["sg_w"], lp["sg_b"], batch, seq)
    yd, pdelta, pconv = _dn_prompt_call(_matmul_call(hp, lp["w_dn"], tmp), lp["cw"], lp["alog"], lp["dtb"], lp["ng"],
                                        batch, seq)
    mix = _merge_call(hp, jnp.stack([ya, yb, yc, yd]), lp["w_gate"], lp["w_branch"], tmp)
    xp, h2 = _proj_call(mix, lp["w_out"], xp, lp["norm2_g"], tmp)
    xp, hp = _ffn_call(h2, lp["w_ff1"], lp["w_ff2"], xp, g_next, tmp, 512)

    ya, sk, sv = _attn_sample_call(_matmul_call(hs, lp["w_att"], tms), ck, cv, rope_s[0], rope_s[1], lp["qg"],
                                   lp["kg"], e, lp["sink_b"])
    yb, sre, sim = _ssm_sample_call(_matmul_call(hs, lp["w_ssm"], tms), s0r, s0i, lp["bm"], lp["cm"], lp["a"],
                                    lp["ssm_d"], lp["glu_w"], lp["glu_b"])
    yc, schunk = _sg_sample_call(_matmul_call(hs, lp["w_sg"], tms), lp["sg_g"], lp["sg_w00"], lp["sg_b"])
    yd, sdelta, sconv = _dn_sample_call(_matmul_call(hs, lp["w_dn"], tms), conv0, s0d, lp["cw"], lp["alog"],
                                        lp["dtb"], lp["ng"])
    mix = _merge_call(hs, jnp.stack([ya, yb, yc, yd]), lp["w_gate"], lp["w_branch"], tms)
    xs, h2 = _proj_call(mix, lp["w_out"], xs, lp["norm2_g"], tms)
    xs, hs = _ffn_call(h2, lp["w_ff1"], lp["w_ff2"], xs, g_next, tms, 512)

    new_p = (pk, pv, pre, pim, pdelta, pconv, pchunk)
    new_s = (sk, sv, sre, sim, sdelta, sconv, schunk)
    return xp, hp, xs, hs, new_p, new_s


def kernel(x_prompt, x_sample, cache_k, cache_v, state_ssm_re, state_ssm_im, state_delta, state_conv, norm1_g, w_in, att_qn_g, att_kn_g, att_sink, ssm_lam_re, ssm_lam_im, ssm_log_dt, ssm_b_re, ssm_b_im, ssm_c_re, ssm_c_im, ssm_d, ssm_glu_w, ssm_glu_b, sg_norm_g, sg_w, sg_b, dn_conv_w, dn_a_log, dn_dt_bias, dn_norm_g, w_branch, w_out, norm2_g, w_ff1, w_ff2):
    p = dict(w_in=w_in, att_qn_g=att_qn_g, att_kn_g=att_kn_g, att_sink=att_sink, ssm_lam_re=ssm_lam_re,
             ssm_lam_im=ssm_lam_im, ssm_log_dt=ssm_log_dt, ssm_b_re=ssm_b_re, ssm_b_im=ssm_b_im, ssm_c_re=ssm_c_re,
             ssm_c_im=ssm_c_im, ssm_d=ssm_d, ssm_glu_w=ssm_glu_w, ssm_glu_b=ssm_glu_b, sg_norm_g=sg_norm_g, sg_w=sg_w,
             sg_b=sg_b, dn_conv_w=dn_conv_w, dn_a_log=dn_a_log, dn_dt_bias=dn_dt_bias, dn_norm_g=dn_norm_g,
             w_branch=w_branch, w_out=w_out, norm2_g=norm2_g, w_ff1=w_ff1, w_ff2=w_ff2)
    depth = w_in.shape[0]
    batch, seq, d = x_prompt.shape
    nsamp = x_sample.shape[0]
    past = cache_k.shape[2]
    assert x_sample.shape[1] == 1 and past == WINDOW and seq % CHUNK == 0 and d == D_MODEL
    nkv = ATT_HKV * ATT_HD

    xp = x_prompt.reshape(batch * seq, d)
    xs = x_sample.reshape(nsamp, d)
    rope_p = _rope_tables(jnp.arange(seq))
    rope_s = _rope_tables(PAST_LEN + jnp.arange(1))
    lane = jnp.arange(BRANCH_W) // ATT_HD
    e = (lane[:, None] == lane[None, :]).astype(BF16)

    g1 = norm1_g.reshape(depth, 1, d)
    hp = _rmsnorm_call(xp, g1[0], _tile_rows(xp.shape[0]))
    hs = _rmsnorm_call(xs, g1[0], _tile_rows(xs.shape[0]))
    new_p = [[] for _ in range(7)]
    new_s = [[] for _ in range(7)]
    for l in range(depth):
        lp = _layer_params(l, p)
        st = (cache_k[l].reshape(nsamp, past, nkv), cache_v[l].reshape(nsamp, past, nkv),
              state_ssm_re[l].reshape(nsamp, SSM_N), state_ssm_im[l].reshape(nsamp, SSM_N),
              state_delta[l], state_conv[l])
        g_next = g1[l + 1] if l + 1 < depth else jnp.ones((1, d), F32)
        xp, hp, xs, hs, sp, ss = _layer(xp, hp, xs, hs, st, lp, g_next, rope_p, rope_s, e, batch, seq)
        for i in range(7):
            new_p[i].append(sp[i])
            new_s[i].append(ss[i])

    def stack(parts, shape):
        return jnp.stack(parts, axis=0).reshape((depth,) + shape)

    keep = min(WINDOW, seq)
    return (
        xp.reshape(batch, seq, d), xs.reshape(nsamp, 1, d),
        stack(new_p[0], (batch, keep, ATT_HKV, ATT_HD)), stack(new_p[1], (batch, keep, ATT_HKV, ATT_HD)),
        stack(new_p[2], (batch, SSM_NG, SSM_P)), stack(new_p[3], (batch, SSM_NG, SSM_P)),
        stack(new_p[4], (batch, DN_H, DN_HD, DN_HD)), stack(new_p[5], (batch, DN_CONV - 1, W_DN_QKV)),
        stack(new_p[6], (batch, CHUNK, BRANCH_W)),
        stack(new_s[0], (nsamp, past, ATT_HKV, ATT_HD)), stack(new_s[1], (nsamp, past, ATT_HKV, ATT_HD)),
        stack(new_s[2], (nsamp, SSM_NG, SSM_P)), stack(new_s[3], (nsamp, SSM_NG, SSM_P)),
        stack(new_s[4], (nsamp, DN_H, DN_HD, DN_HD)), stack(new_s[5], (nsamp, DN_CONV - 1, W_DN_QKV)),
        stack(new_s[6], (nsamp, 1, BRANCH_W)),
    )
```

```python
import jax
import jax.numpy as jnp
from jax import lax
from jax.experimental import pallas as pl
from jax.experimental.pallas import tpu as pltpu

F32 = jnp.float32
BF16 = jnp.bfloat16
HIGHEST = lax.Precision.HIGHEST

D_MODEL = 2048
PAST_LEN = 16384
N_BRANCH = 4
BRANCH_W = D_MODEL // N_BRANCH
ATT_HD = 64
ATT_HQ = BRANCH_W // ATT_HD
ATT_HKV = 2
ATT_G = ATT_HQ // ATT_HKV
WINDOW = 128
ROPE_THETA = 10000.0
SSM_GROUP = 16
SSM_NG = BRANCH_W // SSM_GROUP
SSM_P = 64
SSM_N = SSM_NG * SSM_P
SSM_SEGS = 8
CHUNK = 128
SG_GROUPS = 4
SG_GW = BRANCH_W // SG_GROUPS
DN_HD = 128
DN_H = BRANCH_W // DN_HD
DN_CONV = 4
DN_CHUNK = 64
DN_STEP_CHUNKS = 2
D_FF = 4 * D_MODEL
EPS = 1e-6
NEG_INF = -1e30

LANES = 128
SUBLANES = 8
V7X_VMEM_LIMIT = 56 * 1024 * 1024

W_ATT = ATT_HQ * ATT_HD + 2 * ATT_HKV * ATT_HD
W_SSM = BRANCH_W
W_SG = 2 * BRANCH_W
W_DN_QKV = 3 * BRANCH_W
W_DN = W_DN_QKV + BRANCH_W + LANES
N_MAIN = W_ATT + W_SSM + W_SG + W_DN_QKV + BRANCH_W + 2 * DN_H
W_MAIN = W_ATT + W_SSM + W_SG + W_DN
GATE_COL0 = 3 * D_MODEL

SSM_BLOCK_CH = LANES
SSM_BLOCKS = BRANCH_W // SSM_BLOCK_CH
SSM_BLOCK_N = SSM_N // SSM_BLOCKS

ROWS_PROMPT = 512
FFN_TF = 1024
SAMPLE_ROWS = 8


def _params(*sem):
    return pltpu.CompilerParams(dimension_semantics=sem, vmem_limit_bytes=V7X_VMEM_LIMIT)


def _sds(shape, dtype):
    return jax.ShapeDtypeStruct(shape, dtype)


def _dot(a, b, precision=None):
    return jnp.dot(a, b, preferred_element_type=F32, precision=precision)


def _dot_nt(a, b, precision=None):
    return lax.dot_general(a, b, (((1,), (1,)), ((), ())), preferred_element_type=F32, precision=precision)


def _rms(x, g):
    ms = jnp.mean(x * x, axis=-1, keepdims=True)
    return x * lax.rsqrt(ms + EPS) * g


def _split2(x):
    hi = x.astype(BF16)
    return hi, (x - hi.astype(F32)).astype(BF16)


def _split_dot(x, e):
    hi, lo = _split2(x)
    return _dot(hi, e) + _dot(lo, e)


def _rmsnorm_body(x_ref, g_ref, o_ref):
    o_ref[...] = _rms(x_ref[...], g_ref[...]).astype(o_ref.dtype)


def _rmsnorm_call(x, g, tm):
    m, d = x.shape
    return pl.pallas_call(
        _rmsnorm_body,
        out_shape=_sds((m, d), BF16),
        grid=(m // tm,),
        in_specs=[pl.BlockSpec((tm, d), lambda i: (i, 0)), pl.BlockSpec((1, d), lambda i: (0, 0))],
        out_specs=pl.BlockSpec((tm, d), lambda i: (i, 0)),
        compiler_params=_params("parallel"),
        name="rmsnorm",
    )(x, g)


def _in_proj_body(h_ref, w_ref, *out_refs):
    h = h_ref[...]
    off = 0
    for o_ref in out_refs:
        width = o_ref.shape[1]
        o_ref[...] = _dot_nt(h, w_ref[off:off + width, :])
        off += width


def _in_proj_call(h, w_all, layer, tm):
    m, k = h.shape
    widths = (W_ATT, W_SSM, W_SG, W_DN)
    row = lambda i: (i, 0)
    return pl.pallas_call(
        _in_proj_body,
        out_shape=tuple(_sds((m, n), F32) for n in widths),
        grid=(m // tm,),
        in_specs=[pl.BlockSpec((tm, k), row),
                  pl.BlockSpec((None, W_MAIN, k), lambda i: (layer, 0, 0), pipeline_mode=pl.Buffered(1))],
        out_specs=tuple(pl.BlockSpec((tm, n), row) for n in widths),
        compiler_params=_params("parallel"),
        name="in_proj",
    )(h, w_all)


def _merge_body(h_ref, y_ref, wg_ref, wb_ref, o_ref, acc_ref):
    n = pl.program_id(1)

    @pl.when(n == 0)
    def _():
        acc_ref[...] = jnp.zeros_like(acc_ref)

    gate = jax.nn.sigmoid(_dot_nt(h_ref[...], wg_ref[...]))
    acc_ref[...] += gate * _dot(y_ref[...], wb_ref[...])

    @pl.when(n == N_BRANCH - 1)
    def _():
        o_ref[...] = acc_ref[...].astype(o_ref.dtype)


def _merge_call(h, ys, w_all, wb, layer, tm):
    m, d = h.shape
    gate0 = GATE_COL0 // d
    return pl.pallas_call(
        _merge_body,
        out_shape=_sds((m, d), BF16),
        grid=(m // tm, N_BRANCH),
        in_specs=[
            pl.BlockSpec((tm, d), lambda i, n: (i, 0)),
            pl.BlockSpec((None, tm, BRANCH_W), lambda i, n: (n, i, 0)),
            pl.BlockSpec((None, d, d), lambda i, n: (layer, gate0 + n, 0)),
            pl.BlockSpec((None, None, BRANCH_W, d), lambda i, n: (layer, n, 0, 0)),
        ],
        out_specs=pl.BlockSpec((tm, d), lambda i, n: (i, 0)),
        scratch_shapes=[pltpu.VMEM((tm, d), F32)],
        compiler_params=_params("parallel", "arbitrary"),
        name="merge",
    )(h, ys, w_all, wb)


def _proj_body(m_ref, w_ref, x_ref, g_ref, xo_ref, ho_ref):
    xn = x_ref[...] + _dot(m_ref[...], w_ref[...])
    xo_ref[...] = xn
    ho_ref[...] = _rms(xn, g_ref[...]).astype(ho_ref.dtype)


def _proj_call(mix, w, layer, x, g, tm):
    m, d = x.shape
    return pl.pallas_call(
        _proj_body,
        out_shape=(_sds((m, d), F32), _sds((m, d), BF16)),
        grid=(m // tm,),
        in_specs=[
            pl.BlockSpec((tm, d), lambda i: (i, 0)),
            pl.BlockSpec((None, d, d), lambda i: (layer, 0, 0)),
            pl.BlockSpec((tm, d), lambda i: (i, 0)),
            pl.BlockSpec((1, d), lambda i: (0, 0)),
        ],
        out_specs=(pl.BlockSpec((tm, d), lambda i: (i, 0)), pl.BlockSpec((tm, d), lambda i: (i, 0))),
        compiler_params=_params("parallel"),
        name="out_proj",
    )(mix, w, x, g)


def _ffn_body(h_ref, w1_ref, w2_ref, x_ref, g_ref, xo_ref, ho_ref):
    f = pl.program_id(1)

    @pl.when(f == 0)
    def _():
        xo_ref[...] = x_ref[...]

    a = _dot(h_ref[...], w1_ref[...])
    a = jnp.square(jnp.maximum(a, 0.0)).astype(BF16)
    xo_ref[...] += _dot(a, w2_ref[...])

    @pl.when(f == pl.num_programs(1) - 1)
    def _():
        ho_ref[...] = _rms(xo_ref[...], g_ref[...]).astype(ho_ref.dtype)


def _ffn_call(h, w1, w2, layer, x, g_next, tm, tf):
    m, d = x.shape
    dff = w1.shape[2]
    return pl.pallas_call(
        _ffn_body,
        out_shape=(_sds((m, d), F32), _sds((m, d), BF16)),
        grid=(m // tm, dff // tf),
        in_specs=[
            pl.BlockSpec((tm, d), lambda i, f: (i, 0)),
            pl.BlockSpec((None, d, tf), lambda i, f: (layer, 0, f)),
            pl.BlockSpec((None, tf, d), lambda i, f: (layer, f, 0)),
            pl.BlockSpec((tm, d), lambda i, f: (i, 0)),
            pl.BlockSpec((1, d), lambda i, f: (0, 0)),
        ],
        out_specs=(pl.BlockSpec((tm, d), lambda i, f: (i, 0)), pl.BlockSpec((tm, d), lambda i, f: (i, 0))),
        compiler_params=_params("parallel", "arbitrary"),
        name="ffn",
    )(h, w1, w2, x, g_next)


def _head_norm_rope(x, g, cos, sin, e):
    w = x.shape[1]
    ss = _split_dot(x * x, e)
    xn = x * lax.rsqrt(ss * (1.0 / ATT_HD) + EPS) * g
    lane = lax.broadcasted_iota(jnp.int32, x.shape, 1)
    first_half = (lane % ATT_HD) < (ATT_HD // 2)
    rot = jnp.where(first_half, pltpu.roll(xn, w - ATT_HD // 2, 1), pltpu.roll(xn, ATT_HD // 2, 1))
    return xn * cos + rot * sin


def _qkv_norm_rope(z, cos, sin, qg_ref, kg_ref, e_ref):
    nq = ATT_HQ * ATT_HD
    nk = ATT_HKV * ATT_HD
    rep = nq // LANES
    q = _head_norm_rope(z[:, :nq], qg_ref[...], jnp.tile(cos, (1, rep)), jnp.tile(sin, (1, rep)), e_ref[...])
    k = _head_norm_rope(z[:, nq:nq + nk], kg_ref[...], cos, sin, e_ref[:nk, :nk])
    return q, k, z[:, nq + nk:]


def _half_placed(x):
    lane = lax.broadcasted_iota(jnp.int32, x.shape, 1)
    lo = lane < ATT_HD
    xr = pltpu.roll(x, ATT_HD, 1)
    zero = jnp.zeros_like(x)
    return ((jnp.where(lo, x, zero), jnp.where(lo, zero, xr)),
            (jnp.where(lo, xr, zero), jnp.where(lo, zero, x)))


def _attn_prompt_body(z_ref, cos_ref, sin_ref, qg_ref, kg_ref, e_ref, sink_ref,
                      y_ref, ko_ref, vo_ref, kprev_ref, vprev_ref):
    i = pl.program_id(1)

    @pl.when(i == 0)
    def _():
        kprev_ref[...] = jnp.zeros_like(kprev_ref)
        vprev_ref[...] = jnp.zeros_like(vprev_ref)

    q, k, v = _qkv_norm_rope(z_ref[...], cos_ref[...], sin_ref[...], qg_ref, kg_ref, e_ref)
    ko_ref[0] = k
    vo_ref[0] = v
    kcat = jnp.concatenate([kprev_ref[...], k], axis=0)
    vcat = jnp.concatenate([vprev_ref[...], v], axis=0)
    kprev_ref[...] = k
    vprev_ref[...] = v
    kparts = _half_placed(kcat)
    vparts = _half_placed(vcat)
    qb = q.astype(BF16)

    rows = 2 * WINDOW
    r = lax.broadcasted_iota(jnp.int32, (rows, rows), 0) % WINDOW
    c = lax.broadcasted_iota(jnp.int32, (rows, rows), 1)
    rel = r - c + WINDOW
    mask = (rel >= 0) & (rel < WINDOW) & ((c >= WINDOW) | (i > 0))
    top = lax.broadcasted_iota(jnp.int32, (rows, 1), 0) < WINDOW

    pair_out = []
    for kv in range(ATT_HKV):
        qst = jnp.concatenate([qb[:, LANES * (2 * kv):LANES * (2 * kv + 1)],
                               qb[:, LANES * (2 * kv + 1):LANES * (2 * kv + 2)]], axis=0)
        acc = None
        for parity in range(2):
            s = _dot_nt(qst, kparts[kv][parity].astype(BF16)) * (ATT_HD ** -0.5)
            s = jnp.where(mask, s, NEG_INF)
            sink = jnp.where(top, sink_ref[ATT_G * kv + parity], sink_ref[ATT_G * kv + 2 + parity])
            mx = jnp.maximum(jnp.max(s, axis=-1, keepdims=True), sink)
            ex = jnp.exp(s - mx)
            den = jnp.sum(ex, axis=-1, keepdims=True) + jnp.exp(sink - mx)
            p = (ex / den).astype(BF16)
            o = _dot(p, vparts[kv][parity].astype(BF16))
            acc = o if acc is None else acc + o
        pair_out += [acc[:WINDOW], acc[WINDOW:]]
    y_ref[...] = jnp.concatenate(pair_out, axis=1).astype(y_ref.dtype)


def _attn_prompt_call(z_att, cos, sin, qg, kg, e, sink, batch, seq):
    nb = seq // WINDOW
    nkv = ATT_HKV * ATT_HD
    return pl.pallas_call(
        _attn_prompt_body,
        out_shape=(_sds((N_BRANCH, batch * seq, BRANCH_W), BF16), _sds((batch, WINDOW, nkv), F32),
                   _sds((batch, WINDOW, nkv), F32)),
        grid=(batch, nb),
        in_specs=[
            pl.BlockSpec((WINDOW, W_ATT), lambda b, i: (b * nb + i, 0)),
            pl.BlockSpec((WINDOW, LANES), lambda b, i: (i, 0)),
            pl.BlockSpec((WINDOW, LANES), lambda b, i: (i, 0)),
            pl.BlockSpec((1, BRANCH_W), lambda b, i: (0, 0)),
            pl.BlockSpec((1, nkv), lambda b, i: (0, 0)),
            pl.BlockSpec((BRANCH_W, BRANCH_W), lambda b, i: (0, 0)),
            pl.BlockSpec(memory_space=pltpu.SMEM),
        ],
        out_specs=(
            pl.BlockSpec((None, WINDOW, BRANCH_W), lambda b, i: (0, b * nb + i, 0)),
            pl.BlockSpec((1, WINDOW, nkv), lambda b, i: (b, 0, 0)),
            pl.BlockSpec((1, WINDOW, nkv), lambda b, i: (b, 0, 0)),
        ),
        scratch_shapes=[pltpu.VMEM((WINDOW, nkv), F32), pltpu.VMEM((WINDOW, nkv), F32)],
        compiler_params=_params("arbitrary", "arbitrary"),
        name="attn_prompt",
    )(z_att, cos, sin, qg, kg, e, sink)


def _attn_sample_body(z_ref, ck_ref, cv_ref, cos_ref, sin_ref, qg_ref, kg_ref, e_ref, sink_ref,
                      y_ref, ko_ref, vo_ref):
    nrow = z_ref.shape[0]
    q_all, k_all, v_all = _qkv_norm_rope(z_ref[...], cos_ref[...], sin_ref[...], qg_ref, kg_ref, e_ref)
    lane = lax.broadcasted_iota(jnp.int32, (1, LANES), 1)
    lo = lane < ATT_HD
    row = lax.broadcasted_iota(jnp.int32, (ATT_HQ, LANES), 0)
    scale = ATT_HD ** -0.5
    sink = sink_ref[:, 0:1]
    ys = []
    for i in range(nrow):
        q = q_all[i:i + 1, :]
        k = k_all[i:i + 1, :]
        v = v_all[i:i + 1, :]
        ck = ck_ref[i]
        cv = cv_ref[i]
        ko_ref[i, 0:WINDOW - 1, :] = ck_ref[i, 1:WINDOW, :]
        ko_ref[i, WINDOW - 1:WINDOW, :] = k
        vo_ref[i, 0:WINDOW - 1, :] = cv_ref[i, 1:WINDOW, :]
        vo_ref[i, WINDOW - 1:WINDOW, :] = v

        qexp = jnp.zeros((ATT_HQ, LANES), F32)
        for h in range(ATT_HQ):
            pair = q[:, LANES * (h // 2):LANES * (h // 2 + 1)]
            want_lo = (h // ATT_G) == 0
            have_lo = (h % 2) == 0
            src = pair if want_lo == have_lo else pltpu.roll(pair, ATT_HD, 1)
            placed = jnp.where(lo if want_lo else jnp.logical_not(lo), src, 0.0)
            qexp = jnp.where(row == h, jnp.broadcast_to(placed, (ATT_HQ, LANES)), qexp)

        s = _dot_nt(qexp.astype(BF16), ck.astype(BF16)) * scale
        key = lax.broadcasted_iota(jnp.int32, s.shape, 1)
        s = jnp.where(key >= 1, s, NEG_INF)
        qr = qexp.astype(BF16).astype(F32)
        kr = k.astype(BF16).astype(F32)
        s_new = jnp.sum(qr * kr, axis=-1, keepdims=True) * scale
        mx = jnp.maximum(jnp.maximum(jnp.max(s, axis=-1, keepdims=True), s_new), sink)
        ex = jnp.exp(s - mx)
        ex_new = jnp.exp(s_new - mx)
        den = jnp.sum(ex, axis=-1, keepdims=True) + ex_new + jnp.exp(sink - mx)
        p = (ex / den).astype(BF16)
        p_new = (ex_new / den).astype(BF16).astype(F32)
        o = _dot(p, cv.astype(BF16)) + p_new * v.astype(BF16).astype(F32)

        outs = []
        for j in range(ATT_HQ // 2):
            parts = []
            for par in range(2):
                h = 2 * j + par
                orow = o[h:h + 1, :]
                have_lo = (h // ATT_G) == 0
                want_lo = par == 0
                src = orow if want_lo == have_lo else pltpu.roll(orow, ATT_HD, 1)
                parts.append(jnp.where(lo if want_lo else jnp.logical_not(lo), src, 0.0))
            outs.append(parts[0] + parts[1])
        ys.append(jnp.concatenate(outs, axis=1))
    y_ref[...] = jnp.concatenate(ys, axis=0).astype(y_ref.dtype)


def _attn_sample_call(z_att, ck, cv, cos, sin, qg, kg, e, sink_b):
    nbatch = z_att.shape[0]
    nkv = ATT_HKV * ATT_HD
    rows = SAMPLE_ROWS
    const = lambda b: (0, 0)
    return pl.pallas_call(
        _attn_sample_body,
        out_shape=(_sds((nbatch, BRANCH_W), F32), _sds((nbatch, WINDOW, nkv), F32),
                   _sds((nbatch, WINDOW, nkv), F32)),
        grid=(nbatch // rows,),
        in_specs=[
            pl.BlockSpec((rows, W_ATT), lambda b: (b, 0)),
            pl.BlockSpec((rows, WINDOW, nkv), lambda b: (b, 0, 0)),
            pl.BlockSpec((rows, WINDOW, nkv), lambda b: (b, 0, 0)),
            pl.BlockSpec((1, LANES), const),
            pl.BlockSpec((1, LANES), const),
            pl.BlockSpec((1, BRANCH_W), const),
            pl.BlockSpec((1, nkv), const),
            pl.BlockSpec((BRANCH_W, BRANCH_W), const),
            pl.BlockSpec((ATT_HQ, LANES), const),
        ],
        out_specs=(
            pl.BlockSpec((rows, BRANCH_W), lambda b: (b, 0)),
            pl.BlockSpec((rows, WINDOW, nkv), lambda b: (b, 0, 0)),
            pl.BlockSpec((rows, WINDOW, nkv), lambda b: (b, 0, 0)),
        ),
        compiler_params=_params("parallel"),
        name="attn_sample",
    )(z_att, ck, cv, cos, sin, qg, kg, e, sink_b)


def _cmul(ar, ai, br, bi):
    return ar * br - ai * bi, ar * bi + ai * br


def _ssm_in(ub, bm_ref, j):
    blk = slice(j * SSM_BLOCK_CH, (j + 1) * SSM_BLOCK_CH)
    return _dot(ub[:, blk], bm_ref[blk, :])


def _ssm_out(s_re, s_im, cm_ref, j):
    return (_dot(s_re.astype(BF16), cm_ref[j * SSM_BLOCK_N:(j + 1) * SSM_BLOCK_N, :])
            + _dot(s_im.astype(BF16), cm_ref[SSM_N + j * SSM_BLOCK_N:SSM_N + (j + 1) * SSM_BLOCK_N, :]))


def _ssm_tail(y, u, d_ref, gw_ref, gb_ref):
    y = jax.nn.gelu(y + d_ref[...] * u)
    return y * jax.nn.sigmoid(_dot(y.astype(BF16), gw_ref[...]) + gb_ref[...])


def _ssm_prompt_body(u_ref, bm_ref, cm_ref, a_ref, d_ref, gw_ref, gb_ref, ys_in_ref,
                     y_ref, sre_ref, sim_ref,
                     u3_ref, up_ref, bu_ref, start_ref, carry_ref, yp_ref):
    del ys_in_ref
    tb = u_ref.shape[0]
    ts = tb // SSM_SEGS
    nlc = BRANCH_W // LANES
    step_j = pl.program_id(1)

    @pl.when(step_j == 0)
    def _():
        carry_ref[...] = jnp.zeros_like(carry_ref)

    for c in range(nlc):
        u3_ref[c] = u_ref[:, c * LANES:(c + 1) * LANES]

    def gather(t, carry):
        for c in range(nlc):
            up_ref[pl.ds(pl.multiple_of(t * SUBLANES, SUBLANES), SUBLANES), c * LANES:(c + 1) * LANES] = (
                u3_ref[c, pl.ds(t, SUBLANES, stride=ts), :])
        return carry

    lax.fori_loop(0, ts, gather, 0)
    up = up_ref[...]
    ub = up.astype(BF16)
    cw = SSM_BLOCK_N
    for j in range(SSM_BLOCKS):
        bu = _ssm_in(ub, bm_ref, j)
        bu_ref[:, j * cw:(j + 1) * cw] = bu[:, :cw]
        bu_ref[:, SSM_N + j * cw:SSM_N + (j + 1) * cw] = bu[:, cw:]

    def scan(c0, init_re, init_im, store):
        ar = jnp.broadcast_to(a_ref[0:1, c0:c0 + cw], (SUBLANES, cw))
        ai = jnp.broadcast_to(a_ref[1:2, c0:c0 + cw], (SUBLANES, cw))

        def step(t, st):
            sr, si = st
            r0 = pl.multiple_of(t * SUBLANES, SUBLANES)
            br = bu_ref[pl.ds(r0, SUBLANES), c0:c0 + cw]
            bi = bu_ref[pl.ds(r0, SUBLANES), SSM_N + c0:SSM_N + c0 + cw]
            pr, pi = _cmul(ar, ai, sr, si)
            sr, si = pr + br, pi + bi
            if store:
                bu_ref[pl.ds(r0, SUBLANES), c0:c0 + cw] = sr
                bu_ref[pl.ds(r0, SUBLANES), SSM_N + c0:SSM_N + c0 + cw] = si
            return sr, si

        return lax.fori_loop(0, ts, step, (init_re, init_im))

    zero = jnp.zeros((SUBLANES, cw), F32)
    ys = []
    for j in range(SSM_BLOCKS):
        c0 = j * cw
        er, ei = scan(c0, zero, zero, False)
        pr, pi = a_ref[0:1, c0:c0 + cw], a_ref[1:2, c0:c0 + cw]
        n = 1
        while n < ts:
            pr, pi = _cmul(pr, pi, pr, pi)
            n *= 2
        sr = carry_ref[0:1, c0:c0 + cw]
        si = carry_ref[1:2, c0:c0 + cw]
        for s in range(SSM_SEGS):
            start_ref[s:s + 1, c0:c0 + cw] = sr
            start_ref[s:s + 1, SSM_N + c0:SSM_N + c0 + cw] = si
            qr, qi = _cmul(pr, pi, sr, si)
            sr, si = qr + er[s:s + 1, :], qi + ei[s:s + 1, :]
        carry_ref[0:1, c0:c0 + cw] = sr
        carry_ref[1:2, c0:c0 + cw] = si
        scan(c0, start_ref[:, c0:c0 + cw], start_ref[:, SSM_N + c0:SSM_N + c0 + cw], True)
        ys.append(_ssm_out(bu_ref[:, c0:c0 + cw], bu_ref[:, SSM_N + c0:SSM_N + c0 + cw], cm_ref, j))

    sre_ref[0] = carry_ref[0:1, :]
    sim_ref[0] = carry_ref[1:2, :]
    yp_ref[...] = _ssm_tail(jnp.concatenate(ys, axis=1), up, d_ref, gw_ref, gb_ref)

    def scatter(t, carry):
        for c in range(nlc):
            u3_ref[c, pl.ds(t, SUBLANES, stride=ts), :] = (
                yp_ref[pl.ds(pl.multiple_of(t * SUBLANES, SUBLANES), SUBLANES), c * LANES:(c + 1) * LANES])
        return carry

    lax.fori_loop(0, ts, scatter, 0)
    for c in range(nlc):
        y_ref[:, c * LANES:(c + 1) * LANES] = u3_ref[c].astype(y_ref.dtype)


def _ssm_prompt_call(z_u, bm, cm, a, d, gw, gb, layer, ys, batch, seq, tb):
    nblk = seq // tb
    const = lambda b, j: (0, 0)
    of_layer = lambda b, j: (layer, 0, 0)
    return pl.pallas_call(
        _ssm_prompt_body,
        out_shape=(_sds(ys.shape, ys.dtype), _sds((batch, 1, SSM_N), F32), _sds((batch, 1, SSM_N), F32)),
        grid=(batch, nblk),
        in_specs=[
            pl.BlockSpec((tb, BRANCH_W), lambda b, j: (b * nblk + j, 0)),
            pl.BlockSpec((None, BRANCH_W, 2 * SSM_BLOCK_N), of_layer),
            pl.BlockSpec((None, 2 * SSM_N, SSM_BLOCK_CH), of_layer),
            pl.BlockSpec((2, SSM_N), const),
            pl.BlockSpec((1, BRANCH_W), const),
            pl.BlockSpec((None, BRANCH_W, BRANCH_W), of_layer),
            pl.BlockSpec((1, BRANCH_W), const),
            pl.BlockSpec(memory_space=pl.ANY),
        ],
        out_specs=(
            pl.BlockSpec((None, tb, BRANCH_W), lambda b, j: (1, b * nblk + j, 0)),
            pl.BlockSpec((1, 1, SSM_N), lambda b, j: (b, 0, 0)),
            pl.BlockSpec((1, 1, SSM_N), lambda b, j: (b, 0, 0)),
        ),
        input_output_aliases={7: 0},
        scratch_shapes=[
            pltpu.VMEM((BRANCH_W // LANES, tb, LANES), F32),
            pltpu.VMEM((tb, BRANCH_W), F32),
            pltpu.VMEM((tb, 2 * SSM_N), F32),
            pltpu.VMEM((SSM_SEGS, 2 * SSM_N), F32),
            pltpu.VMEM((2, SSM_N), F32),
            pltpu.VMEM((tb, BRANCH_W), F32),
        ],
        compiler_params=_params("arbitrary", "arbitrary"),
        name="ssm_prompt",
    )(z_u, bm, cm, a, d, gw, gb, ys)


def _ssm_sample_body(u_ref, s0r_ref, s0i_ref, bm_ref, cm_ref, a_ref, d_ref, gw_ref, gb_ref,
                     y_ref, sre_ref, sim_ref):
    u = u_ref[...]
    ub = u.astype(BF16)
    cw = SSM_BLOCK_N
    ys = []
    for j in range(SSM_BLOCKS):
        cols = slice(j * cw, (j + 1) * cw)
        bu = _ssm_in(ub, bm_ref, j)
        pr, pi = _cmul(a_ref[0:1, cols], a_ref[1:2, cols], s0r_ref[:, cols], s0i_ref[:, cols])
        sr = pr + bu[:, :cw]
        si = pi + bu[:, cw:]
        sre_ref[:, cols] = sr
        sim_ref[:, cols] = si
        ys.append(_ssm_out(sr, si, cm_ref, j))
    y_ref[...] = _ssm_tail(jnp.concatenate(ys, axis=1), u, d_ref, gw_ref, gb_ref).astype(y_ref.dtype)


def _ssm_sample_call(z_u, s0r, s0i, bm, cm, a, d, gw, gb, layer):
    nbatch = z_u.shape[0]
    const = lambda i: (0, 0)
    of_layer = lambda i: (layer, 0, 0)
    rows_w = pl.BlockSpec((nbatch, BRANCH_W), const)
    rows_n = pl.BlockSpec((nbatch, SSM_N), const)
    return pl.pallas_call(
        _ssm_sample_body,
        out_shape=(_sds((nbatch, BRANCH_W), F32), _sds((nbatch, SSM_N), F32), _sds((nbatch, SSM_N), F32)),
        grid=(1,),
        in_specs=[
            rows_w, rows_n, rows_n,
            pl.BlockSpec((None, BRANCH_W, 2 * SSM_BLOCK_N), of_layer),
            pl.BlockSpec((None, 2 * SSM_N, SSM_BLOCK_CH), of_layer),
            pl.BlockSpec((2, SSM_N), const),
            pl.BlockSpec((1, BRANCH_W), const),
            pl.BlockSpec((None, BRANCH_W, BRANCH_W), of_layer),
            pl.BlockSpec((1, BRANCH_W), const),
        ],
        out_specs=(rows_w, rows_n, rows_n),
        compiler_params=_params("arbitrary"),
        name="ssm_sample",
    )(z_u, s0r, s0i, bm, cm, a, d, gw, gb)


def _sg_prompt_body(z_ref, g_ref, w_ref, b_ref, ys_in_ref, y_ref, v_ref):
    del ys_in_ref
    uv = jax.nn.gelu(z_ref[...])
    u = uv[:, :BRANCH_W]
    v = _rms(uv[:, BRANCH_W:], g_ref[...])
    v_ref[0] = v
    t = lax.broadcasted_iota(jnp.int32, (CHUNK, CHUNK), 0)
    s = lax.broadcasted_iota(jnp.int32, (CHUNK, CHUNK), 1)
    causal = t >= s
    vb = v.astype(BF16)
    mixed = []
    for g in range(SG_GROUPS):
        w = jnp.where(causal, w_ref[g], 0.0).astype(BF16)
        mixed.append(_dot(w, vb[:, g * SG_GW:(g + 1) * SG_GW]))
    mixed = jnp.concatenate(mixed, axis=1) + b_ref[...]
    y_ref[...] = (u * mixed).astype(y_ref.dtype)


def _sg_prompt_call(z_c, g, w, b_exp, ys, batch, seq):
    nch = seq // CHUNK
    return pl.pallas_call(
        _sg_prompt_body,
        out_shape=(_sds(ys.shape, ys.dtype), _sds((batch, CHUNK, BRANCH_W), F32)),
        grid=(batch, nch),
        in_specs=[
            pl.BlockSpec((CHUNK, W_SG), lambda b, i: (b * nch + i, 0)),
            pl.BlockSpec((1, BRANCH_W), lambda b, i: (0, 0)),
            pl.BlockSpec((SG_GROUPS, CHUNK, CHUNK), lambda b, i: (0, 0, 0)),
            pl.BlockSpec((CHUNK, BRANCH_W), lambda b, i: (0, 0)),
            pl.BlockSpec(memory_space=pl.ANY),
        ],
        out_specs=(
            pl.BlockSpec((None, CHUNK, BRANCH_W), lambda b, i: (2, b * nch + i, 0)),
            pl.BlockSpec((1, CHUNK, BRANCH_W), lambda b, i: (b, 0, 0)),
        ),
        input_output_aliases={4: 0},
        compiler_params=_params("arbitrary", "arbitrary"),
        name="sg_prompt",
    )(z_c, g, w, b_exp, ys)


def _sg_sample_body(z_ref, g_ref, w00_ref, b_ref, y_ref, v_ref):
    uv = jax.nn.gelu(z_ref[...])
    u = uv[:, :BRANCH_W]
    v = _rms(uv[:, BRANCH_W:], g_ref[...])
    v_ref[...] = v
    w = w00_ref[...].astype(BF16).astype(F32)
    mixed = w * v.astype(BF16).astype(F32) + b_ref[0:1, :]
    y_ref[...] = (u * mixed).astype(y_ref.dtype)


def _sg_sample_call(z_c, g, w00, b_exp):
    nbatch = z_c.shape[0]
    return pl.pallas_call(
        _sg_sample_body,
        out_shape=(_sds((nbatch, BRANCH_W), F32), _sds((nbatch, BRANCH_W), F32)),
        name="sg_sample",
    )(z_c, g, w00, b_exp)


def _l2n(x):
    return x * lax.rsqrt(jnp.sum(x * x, axis=-1, keepdims=True) + EPS)


def _dn_gates(zab, alog_ref, dtb_ref):
    g = -jnp.exp(alog_ref[...]) * jax.nn.softplus(zab + dtb_ref[...])
    beta = jax.nn.sigmoid(zab)
    return g, beta


def _mm3(a, b):
    (ah, al), (bh, bl) = a, b
    return _dot(ah, bh) + (_dot(ah, bl) + _dot(al, bh))


def _unit_lower_inverse_off(a, blk):
    n = a.shape[0]
    r = lax.broadcasted_iota(jnp.int32, (n, n), 0)
    c = lax.broadcasted_iota(jnp.int32, (n, n), 1)

    def lower_left(size):
        return ((r // (2 * size)) == (c // (2 * size))) & ((r // size) % 2 == 1) & ((c // size) % 2 == 0)

    off = -jnp.where(lower_left(1), a, 0.0)
    size = 2
    while size < blk:
        am = jnp.where(lower_left(size), a, 0.0)
        ob = off.astype(BF16)
        p = am + _dot(ob, am.astype(BF16))
        off = off - (p + _dot(p.astype(BF16), ob))
        size *= 2
    return off


def _head_lanes(x, rows_per_head):
    r = lax.broadcasted_iota(jnp.int32, x.shape, 0) // rows_per_head
    return jnp.concatenate([jnp.where(r == h, x, 0.0) for h in range(DN_H)], axis=1)


def _dn_chunk_terms(qkv, zg, g_all, beta_all, row0):
    tc = DN_CHUNK
    n = DN_H * tc
    rows = slice(row0, row0 + tc)

    def stack(x, off):
        return jnp.concatenate([x[rows, off + h * DN_HD:off + (h + 1) * DN_HD] for h in range(DN_H)], axis=0)

    q = _l2n(stack(qkv, 0)) * (DN_HD ** -0.5)
    k = _l2n(stack(qkv, BRANCH_W))
    v = stack(qkv, 2 * BRANCH_W)
    zgs = stack(zg, 0)

    rt = lax.broadcasted_iota(jnp.int32, (tc, tc), 0)
    ct = lax.broadcasted_iota(jnp.int32, (tc, tc), 1)
    ones_tril = jnp.where(rt >= ct, 1.0, 0.0).astype(BF16)
    g = g_all[rows]
    g_hi, g_lo = _split2(g)
    g_lo2 = (g - g_hi.astype(F32) - g_lo.astype(F32)).astype(BF16)
    gc_c = _dot(ones_tril, g_hi) + (_dot(ones_tril, g_lo) + _dot(ones_tril, g_lo2))
    gc = jnp.concatenate([gc_c[:, h:h + 1] for h in range(DN_H)], axis=0)
    beta = jnp.concatenate([beta_all[rows, DN_H + h:DN_H + h + 1] for h in range(DN_H)], axis=0)
    gc_last = jnp.concatenate([jnp.broadcast_to(gc_c[tc - 1:tc, h:h + 1], (tc, 1)) for h in range(DN_H)], axis=0)

    r = lax.broadcasted_iota(jnp.int32, (n, n), 0)
    c = lax.broadcasted_iota(jnp.int32, (n, n), 1)
    same = (r // tc) == (c // tc)
    tril = same & (r >= c)
    stril = same & (r > c)
    gmat = jnp.broadcast_to(gc, (n, n))
    dec = jnp.where(tril, jnp.exp(jnp.where(tril, gmat - gmat.T, 0.0)), 0.0)
    kb = k.astype(BF16)
    a = jnp.where(stril, beta * _dot_nt(kb, kb) * dec, 0.0)
    inv_off = _unit_lower_inverse_off(a, tc)
    eg = jnp.exp(gc)
    rhs = jnp.concatenate([beta * v, (beta * eg) * k], axis=1)
    sol = rhs + _mm3(_split2(inv_off), _split2(rhs))
    u_t, w_k = sol[:, :DN_HD], sol[:, DN_HD:]
    qk = (_dot_nt(q.astype(BF16), kb) * dec).astype(BF16)
    wq = jnp.concatenate([_head_lanes(w_k, tc), _head_lanes(q * eg, tc)], axis=0).astype(BF16)
    ke_t = _head_lanes(k * jnp.exp(gc_last - gc), tc).T.astype(BF16)
    ge = jnp.concatenate([jnp.broadcast_to(jnp.exp(gc_c[tc - 1:tc, h:h + 1]), (DN_HD, 1)) for h in range(DN_H)],
                         axis=0)
    return u_t, wq, qk, ke_t, ge, zgs


def _dn_chunk_step(s, terms, ng):
    u_t, wq, qk, ke_t, ge, zgs = terms
    n = u_t.shape[0]
    ws = _dot(wq, s.astype(BF16))
    u = u_t - ws[:n]
    ub = u.astype(BF16)
    o = ws[n:] + _dot(qk, ub)
    s = ge * s + _dot(ke_t, ub)
    return s, _rms(o, ng) * jax.nn.silu(zgs)


def _dn_prompt_body(x_ref, zg_ref, zab_ref, cw_ref, alog_ref, dtb_ref, ng_ref, ys_in_ref,
                    y_ref, so_ref, co_ref, xin_ref, s_ref):
    del ys_in_ref
    step = pl.program_id(0)
    nb, tb = x_ref.shape[0], x_ref.shape[1]
    tc = DN_CHUNK
    pad = SUBLANES

    @pl.when(step == 0)
    def _():
        xin_ref[:, 0:pad, :] = jnp.zeros((nb, pad, W_DN_QKV), F32)
        s_ref[...] = jnp.zeros_like(s_ref)

    @pl.when(step > 0)
    def _():
        xin_ref[:, 0:pad, :] = xin_ref[:, tb:tb + pad, :]

    terms = []
    for b in range(nb):
        x = x_ref[b]
        xin_ref[b, pad:pad + tb, :] = x
        co_ref[b] = x_ref[b, tb - (DN_CONV - 1):tb, :]
        acc = cw_ref[DN_CONV - 1:DN_CONV, :] * x
        for i in range(DN_CONV - 1):
            lag = DN_CONV - 1 - i
            acc = acc + cw_ref[i:i + 1, :] * xin_ref[b, pad - lag:pad - lag + tb, :]
        qkv = jax.nn.silu(acc)
        g_all, beta_all = _dn_gates(zab_ref[b], alog_ref, dtb_ref)
        zg = zg_ref[b]
        terms.append([_dn_chunk_terms(qkv, zg, g_all, beta_all, ci * tc) for ci in range(tb // tc)])

    for b in range(nb):
        s = s_ref[b]
        for ci, t in enumerate(terms[b]):
            s, o = _dn_chunk_step(s, t, ng_ref[...])
            y_ref[b, ci * tc:(ci + 1) * tc, :] = jnp.concatenate(
                [o[h * tc:(h + 1) * tc] for h in range(DN_H)], axis=1).astype(y_ref.dtype)
        s_ref[b] = s

    @pl.when(step == pl.num_programs(0) - 1)
    def _():
        so_ref[...] = s_ref[...]


def _dn_prompt_call(z_dn, cw, alog, dtb, ng, ys, batch, seq):
    tb = DN_STEP_CHUNKS * DN_CHUNK if seq % (DN_STEP_CHUNKS * DN_CHUNK) == 0 else DN_CHUNK
    z3 = z_dn.reshape(batch, seq, W_DN)
    ys4 = ys.reshape(N_BRANCH, batch, seq, BRANCH_W)
    const = lambda i: (0, 0)
    whole = lambda i: (0, 0, 0)
    y, so, co = pl.pallas_call(
        _dn_prompt_body,
        out_shape=(_sds(ys4.shape, ys4.dtype), _sds((batch, DN_H * DN_HD, DN_HD), F32),
                   _sds((batch, DN_CONV - 1, W_DN_QKV), F32)),
        grid=(seq // tb,),
        in_specs=[
            pl.BlockSpec((batch, tb, W_DN_QKV), lambda i: (0, i, 0)),
            pl.BlockSpec((batch, tb, BRANCH_W), lambda i: (0, i, W_DN_QKV // BRANCH_W)),
            pl.BlockSpec((batch, tb, LANES), lambda i: (0, i, (W_DN_QKV + BRANCH_W) // LANES)),
            pl.BlockSpec((DN_CONV, W_DN_QKV), const),
            pl.BlockSpec((1, LANES), const),
            pl.BlockSpec((1, LANES), const),
            pl.BlockSpec((1, DN_HD), const),
            pl.BlockSpec(memory_space=pl.ANY),
        ],
        out_specs=(
            pl.BlockSpec((None, batch, tb, BRANCH_W), lambda i: (N_BRANCH - 1, 0, i, 0)),
            pl.BlockSpec((batch, DN_H * DN_HD, DN_HD), whole),
            pl.BlockSpec((batch, DN_CONV - 1, W_DN_QKV), whole),
        ),
        input_output_aliases={7: 0},
        scratch_shapes=[pltpu.VMEM((batch, tb + SUBLANES, W_DN_QKV), F32),
                        pltpu.VMEM((batch, DN_H * DN_HD, DN_HD), F32)],
        compiler_params=_params("arbitrary"),
        name="dn_prompt",
    )(z3, z3, z3, cw, alog, dtb, ng, ys4)
    return y.reshape(ys.shape), so.reshape(batch, DN_H, DN_HD, DN_HD), co


def _dn_sample_body(x_ref, zg_ref, zab_ref, conv_ref, s0_ref, cw_ref, alog_ref, dtb_ref, ng_ref,
                    y_ref, so_ref, co_ref):
    nrow = x_ref.shape[0]
    x_all = x_ref[...]
    acc = cw_ref[DN_CONV - 1:DN_CONV, :] * x_all
    for j in range(DN_CONV - 1):
        acc = acc + cw_ref[j:j + 1, :] * conv_ref[:, j, :]
    qkv_all = jax.nn.silu(acc)
    g_all, beta_all = _dn_gates(zab_ref[...], alog_ref, dtb_ref)
    zg_all = zg_ref[...]
    r = lax.broadcasted_iota(jnp.int32, (DN_HD, DN_HD), 0)
    c = lax.broadcasted_iota(jnp.int32, (DN_HD, DN_HD), 1)
    eye = r == c
    row8 = lax.broadcasted_iota(jnp.int32, (SUBLANES, DN_HD), 0)
    ys = []
    for i in range(nrow):
        co_ref[i, 0:DN_CONV - 2, :] = conv_ref[i, 1:DN_CONV - 1, :]
        co_ref[i, DN_CONV - 2:DN_CONV - 1, :] = x_all[i:i + 1, :]
        qkv = qkv_all[i:i + 1, :]
        outs = []
        for h in range(DN_H):
            q = _l2n(qkv[:, h * DN_HD:(h + 1) * DN_HD]) * (DN_HD ** -0.5)
            k = _l2n(qkv[:, BRANCH_W + h * DN_HD:BRANCH_W + (h + 1) * DN_HD])
            v = qkv[:, 2 * BRANCH_W + h * DN_HD:2 * BRANCH_W + (h + 1) * DN_HD]
            eg = jnp.exp(g_all[i:i + 1, h:h + 1])
            beta = beta_all[i:i + 1, DN_H + h:DN_H + h + 1]
            s = s0_ref[i, h]
            sb = s.astype(BF16)
            kq = jnp.where(row8 == 0, jnp.broadcast_to((beta * eg) * k, (SUBLANES, DN_HD)),
                           jnp.where(row8 == 1, jnp.broadcast_to(q * eg, (SUBLANES, DN_HD)), 0.0))
            ks_qs = _dot(kq.astype(BF16), sb)
            u = beta * v - ks_qs[0:1, :]
            qk = jnp.sum(q.astype(BF16).astype(F32) * k.astype(BF16).astype(F32), axis=-1, keepdims=True)
            o = ks_qs[1:2, :] + qk * u
            kdiag = jnp.where(eye, jnp.broadcast_to(k, (DN_HD, DN_HD)), 0.0)
            outer = _dot(kdiag, jnp.broadcast_to(u, (DN_HD, DN_HD)), HIGHEST)
            so_ref[i, h] = eg * s + outer
            o = _rms(o, ng_ref[...]) * jax.nn.silu(zg_all[i:i + 1, h * DN_HD:(h + 1) * DN_HD])
            outs.append(o)
        ys.append(jnp.concatenate(outs, axis=1))
    y_ref[...] = jnp.concatenate(ys, axis=0).astype(y_ref.dtype)


def _dn_sample_call(z_dn, conv0, s0, cw, alog, dtb, ng):
    nbatch = z_dn.shape[0]
    rows = SAMPLE_ROWS
    const = lambda b: (0, 0)
    return pl.pallas_call(
        _dn_sample_body,
        out_shape=(_sds((nbatch, BRANCH_W), F32), _sds((nbatch, DN_H, DN_HD, DN_HD), F32),
                   _sds((nbatch, DN_CONV - 1, W_DN_QKV), F32)),
        grid=(nbatch // rows,),
        in_specs=[
            pl.BlockSpec((rows, W_DN_QKV), lambda b: (b, 0)),
            pl.BlockSpec((rows, BRANCH_W), lambda b: (b, W_DN_QKV // BRANCH_W)),
            pl.BlockSpec((rows, LANES), lambda b: (b, (W_DN_QKV + BRANCH_W) // LANES)),
            pl.BlockSpec((rows, DN_CONV - 1, W_DN_QKV), lambda b: (b, 0, 0)),
            pl.BlockSpec((rows, DN_H, DN_HD, DN_HD), lambda b: (b, 0, 0, 0)),
            pl.BlockSpec((DN_CONV, W_DN_QKV), const),
            pl.BlockSpec((1, LANES), const),
            pl.BlockSpec((1, LANES), const),
            pl.BlockSpec((1, DN_HD), const),
        ],
        out_specs=(
            pl.BlockSpec((rows, BRANCH_W), lambda b: (b, 0)),
            pl.BlockSpec((rows, DN_H, DN_HD, DN_HD), lambda b: (b, 0, 0, 0)),
            pl.BlockSpec((rows, DN_CONV - 1, W_DN_QKV), lambda b: (b, 0, 0)),
        ),
        compiler_params=_params("parallel"),
        name="dn_sample",
    )(z_dn, z_dn, z_dn, conv0, s0, cw, alog, dtb, ng)


def _rope_tables(pos):
    half = ATT_HD // 2
    inv = ROPE_THETA ** (-jnp.arange(half, dtype=F32) / half)
    ang = pos.astype(F32)[:, None] * inv[None, :]
    cos, sin = jnp.cos(ang), jnp.sin(ang)
    reps = LANES // ATT_HD
    return (jnp.tile(jnp.concatenate([cos, cos], axis=1), (1, reps)),
            jnp.tile(jnp.concatenate([-sin, sin], axis=1), (1, reps)))


def _ssm_matrices(lam_re, lam_im, log_dt, b_re, b_im, c_re, c_im):
    dt = jnp.exp(log_dt)[:, None]
    mag = jnp.exp(lam_re * dt)
    a_re, a_im = mag * jnp.cos(lam_im * dt), mag * jnp.sin(lam_im * dt)
    den = lam_re * lam_re + lam_im * lam_im
    f_re = ((a_re - 1.0) * lam_re + a_im * lam_im) / den
    f_im = (a_im * lam_re - (a_re - 1.0) * lam_im) / den
    bb_re = f_re[..., None] * b_re - f_im[..., None] * b_im
    bb_im = f_re[..., None] * b_im + f_im[..., None] * b_re
    gpb = SSM_NG // SSM_BLOCKS
    eye = jnp.eye(gpb, dtype=F32)

    def in_layout(b):
        b = b.reshape(SSM_BLOCKS, gpb, SSM_P, SSM_GROUP)
        return jnp.einsum("jipc,ik->jickp", b, eye).reshape(BRANCH_W, SSM_BLOCK_N)

    def out_layout(c):
        c = c.reshape(SSM_BLOCKS, gpb, SSM_GROUP, SSM_P)
        return jnp.einsum("jicp,ik->jipkc", c, eye).reshape(SSM_N, SSM_BLOCK_CH)

    bm = jnp.concatenate([in_layout(bb_re), in_layout(bb_im)], axis=1).astype(BF16)
    cm = jnp.concatenate([out_layout(c_re), out_layout(-c_im)], axis=0).astype(BF16)
    a = jnp.stack([a_re.reshape(SSM_N), a_im.reshape(SSM_N)], axis=0)
    return bm, cm, a


def _lane_rows(x):
    return jnp.pad(x, ((0, 0), (0, LANES - x.shape[1])))[:, None, :]


def _prepare(p):
    depth = p["w_in"].shape[0]
    w_in = p["w_in"]
    bm, cm, a = jax.vmap(_ssm_matrices)(p["ssm_lam_re"], p["ssm_lam_im"], p["ssm_log_dt"], p["ssm_b_re"],
                                        p["ssm_b_im"], p["ssm_c_re"], p["ssm_c_im"])
    w_t = jnp.swapaxes(w_in, 1, 2)
    w_all = jnp.concatenate([w_t[:, :N_MAIN, :], jnp.zeros((depth, GATE_COL0 - N_MAIN, D_MODEL), w_in.dtype),
                             w_t[:, N_MAIN:, :]], axis=1).astype(BF16)
    stacked = dict(
        w_all=w_all, w_branch=p["w_branch"].astype(BF16), w_out=p["w_out"].astype(BF16),
        w_ff1=p["w_ff1"].astype(BF16), w_ff2=p["w_ff2"].astype(BF16),
        bm=bm, cm=cm, glu_w=p["ssm_glu_w"].astype(BF16),
    )
    per_layer = dict(
        norm1_g=p["norm1_g"][:, None, :], norm2_g=p["norm2_g"][:, None, :],
        qg=jnp.tile(p["att_qn_g"], (1, ATT_HQ))[:, None, :],
        kg=jnp.tile(p["att_kn_g"], (1, ATT_HKV))[:, None, :],
        sink=p["att_sink"],
        sink_b=jnp.broadcast_to(p["att_sink"][:, :, None], (depth, ATT_HQ, LANES)),
        a=a, ssm_d=p["ssm_d"][:, None, :], glu_b=p["ssm_glu_b"][:, None, :],
        sg_g=p["sg_norm_g"][:, None, :], sg_w=p["sg_w"],
        sg_b=jnp.repeat(jnp.swapaxes(p["sg_b"], 1, 2), SG_GW, axis=2),
        sg_w00=jnp.repeat(p["sg_w"][:, :, 0, 0], SG_GW, axis=1)[:, None, :],
        cw=p["dn_conv_w"], alog=_lane_rows(p["dn_a_log"]), dtb=_lane_rows(p["dn_dt_bias"]),
        ng=p["dn_norm_g"][:, None, :],
    )
    return stacked, per_layer


def _mix_and_ffn(x, h, ys, wts, layer, lp, g_next, tm):
    mix = _merge_call(h, ys, wts["w_all"], wts["w_branch"], layer, tm)
    x, h2 = _proj_call(mix, wts["w_out"], layer, x, lp["norm2_g"], tm)
    return _ffn_call(h2, wts["w_ff1"], wts["w_ff2"], layer, x, g_next, tm, FFN_TF)


def _layer(xp, hp, xs, hs, st, wts, layer, lp, g_next, rope_p, rope_s, e, batch, seq):
    tmp = ROWS_PROMPT if xp.shape[0] % ROWS_PROMPT == 0 else xp.shape[0]
    tms = xs.shape[0]
    ck, cv, s0r, s0i, s0d, conv0 = st
    ssm_w = (wts["bm"], wts["cm"], lp["a"], lp["ssm_d"], wts["glu_w"], lp["glu_b"], layer)

    z_att, z_ssm, z_sg, z_dn = _in_proj_call(hp, wts["w_all"], layer, tmp)
    ys, pk, pv = _attn_prompt_call(z_att, rope_p[0], rope_p[1], lp["qg"], lp["kg"], e, lp["sink"], batch, seq)
    tb = 512 if seq % 512 == 0 else seq
    ys, pre, pim = _ssm_prompt_call(z_ssm, *ssm_w, ys, batch, seq, tb)
    ys, pchunk = _sg_prompt_call(z_sg, lp["sg_g"], lp["sg_w"], lp["sg_b"], ys, batch, seq)
    ys, pdelta, pconv = _dn_prompt_call(z_dn, lp["cw"], lp["alog"], lp["dtb"], lp["ng"], ys, batch, seq)
    xp, hp = _mix_and_ffn(xp, hp, ys, wts, layer, lp, g_next, tmp)

    z_att, z_ssm, z_sg, z_dn = _in_proj_call(hs, wts["w_all"], layer, tms)
    ya, sk, sv = _attn_sample_call(z_att, ck, cv, rope_s[0], rope_s[1], lp["qg"], lp["kg"], e, lp["sink_b"])
    yb, sre, sim = _ssm_sample_call(z_ssm, s0r, s0i, *ssm_w)
    yc, schunk = _sg_sample_call(z_sg, lp["sg_g"], lp["sg_w00"], lp["sg_b"])
    yd, sdelta, sconv = _dn_sample_call(z_dn, conv0, s0d, lp["cw"], lp["alog"], lp["dtb"], lp["ng"])
    xs, hs = _mix_and_ffn(xs, hs, jnp.stack([ya, yb, yc, yd]).astype(BF16), wts, layer, lp, g_next, tms)

    new_p = (pk, pv, pre, pim, pdelta, pconv, pchunk)
    new_s = (sk, sv, sre, sim, sdelta, sconv, schunk)
    return xp, hp, xs, hs, new_p, new_s


def kernel(x_prompt, x_sample, cache_k, cache_v, state_ssm_re, state_ssm_im, state_delta, state_conv, norm1_g, w_in, att_qn_g, att_kn_g, att_sink, ssm_lam_re, ssm_lam_im, ssm_log_dt, ssm_b_re, ssm_b_im, ssm_c_re, ssm_c_im, ssm_d, ssm_glu_w, ssm_glu_b, sg_norm_g, sg_w, sg_b, dn_conv_w, dn_a_log, dn_dt_bias, dn_norm_g, w_branch, w_out, norm2_g, w_ff1, w_ff2):
    p = dict(norm1_g=norm1_g, w_in=w_in, att_qn_g=att_qn_g, att_kn_g=att_kn_g, att_sink=att_sink,
             ssm_lam_re=ssm_lam_re, ssm_lam_im=ssm_lam_im, ssm_log_dt=ssm_log_dt, ssm_b_re=ssm_b_re,
             ssm_b_im=ssm_b_im, ssm_c_re=ssm_c_re, ssm_c_im=ssm_c_im, ssm_d=ssm_d, ssm_glu_w=ssm_glu_w,
             ssm_glu_b=ssm_glu_b, sg_norm_g=sg_norm_g, sg_w=sg_w, sg_b=sg_b, dn_conv_w=dn_conv_w, dn_a_log=dn_a_log,
             dn_dt_bias=dn_dt_bias, dn_norm_g=dn_norm_g, w_branch=w_branch, w_out=w_out, norm2_g=norm2_g,
             w_ff1=w_ff1, w_ff2=w_ff2)
    depth = w_in.shape[0]
    batch, seq, d = x_prompt.shape
    nsamp = x_sample.shape[0]
    past = cache_k.shape[2]
    assert x_sample.shape[1] == 1 and past == WINDOW and seq % CHUNK == 0 and d == D_MODEL
    assert nsamp % SAMPLE_ROWS == 0
    nkv = ATT_HKV * ATT_HD

    xp = x_prompt.reshape(batch * seq, d)
    xs = x_sample.reshape(nsamp, d)
    rope_p = _rope_tables(jnp.arange(seq))
    rope_s = _rope_tables(PAST_LEN + jnp.arange(1))
    lane = jnp.arange(BRANCH_W) // ATT_HD
    e = (lane[:, None] == lane[None, :]).astype(BF16)

    wts, prm = _prepare(p)
    ck_all = cache_k.reshape(depth, nsamp, past, nkv)
    cv_all = cache_v.reshape(depth, nsamp, past, nkv)
    s0r_all = state_ssm_re.reshape(depth, nsamp, SSM_N)
    s0i_all = state_ssm_im.reshape(depth, nsamp, SSM_N)

    hp = _rmsnorm_call(xp, prm["norm1_g"][0], ROWS_PROMPT if xp.shape[0] % ROWS_PROMPT == 0 else xp.shape[0])
    hs = _rmsnorm_call(xs, prm["norm1_g"][0], nsamp)
    new_p = [[] for _ in range(7)]
    new_s = [[] for _ in range(7)]
    for l in range(depth):
        lp = {name: v[l] for name, v in prm.items()}
        st = (ck_all[l], cv_all[l], s0r_all[l], s0i_all[l], state_delta[l], state_conv[l])
        g_next = prm["norm1_g"][l + 1] if l + 1 < depth else jnp.ones((1, d), F32)
        xp, hp, xs, hs, sp, ss = _layer(xp, hp, xs, hs, st, wts, l, lp, g_next, rope_p, rope_s, e, batch, seq)
        for i in range(7):
            new_p[i].append(sp[i])
            new_s[i].append(ss[i])

    def stack(parts, shape):
        return jnp.stack(parts, axis=0).reshape((depth,) + shape)

    keep = min(WINDOW, seq)
    return (
        xp.reshape(batch, seq, d), xs.reshape(nsamp, 1, d),
        stack(new_p[0], (batch, keep, ATT_HKV, ATT_HD)), stack(new_p[1], (batch, keep, ATT_HKV, ATT_HD)),
        stack(new_p[2], (batch, SSM_NG, SSM_P)), stack(new_p[3], (batch, SSM_NG, SSM_P)),
        stack(new_p[4], (batch, DN_H, DN_HD, DN_HD)), stack(new_p[5], (batch, DN_CONV - 1, W_DN_QKV)),
        stack(new_p[6], (batch, CHUNK, BRANCH_W)),
        stack(new_s[0], (nsamp, past, ATT_HKV, ATT_HD)), stack(new_s[1], (nsamp, past, ATT_HKV, ATT_HD)),
        stack(new_s[2], (nsamp, SSM_NG, SSM_P)), stack(new_s[3], (nsamp, SSM_NG, SSM_P)),
        stack(new_s[4], (nsamp, DN_H, DN_HD, DN_HD)), stack(new_s[5], (nsamp, DN_CONV - 1, W_DN_QKV)),
        stack(new_s[6], (nsamp, 1, BRANCH_W)),
    )
```

```python
import jax
import jax.numpy as jnp
from jax import lax
from jax.experimental import pallas as pl
from jax.experimental.pallas import tpu as pltpu

F32 = jnp.float32
BF16 = jnp.bfloat16

D_MODEL = 2048
PAST_LEN = 16384
N_BRANCH = 4
BRANCH_W = D_MODEL // N_BRANCH
ATT_HD = 64
ATT_HQ = BRANCH_W // ATT_HD
ATT_HKV = 2
ATT_G = ATT_HQ // ATT_HKV
WINDOW = 128
ROPE_THETA = 10000.0
SSM_GROUP = 16
SSM_NG = BRANCH_W // SSM_GROUP
SSM_P = 64
SSM_N = SSM_NG * SSM_P
SSM_SEGS = 8
CHUNK = 128
SG_GROUPS = 4
SG_GW = BRANCH_W // SG_GROUPS
SG_STEP_CHUNKS = 4
DN_HD = 128
DN_H = BRANCH_W // DN_HD
DN_CONV = 4
DN_CHUNK = 64
DN_STEP_CHUNKS = 2
D_FF = 4 * D_MODEL
EPS = 1e-6
NEG_INF = -1e30

LANES = 128
SUBLANES = 8
V7X_VMEM_LIMIT = 56 * 1024 * 1024

W_ATT = ATT_HQ * ATT_HD + 2 * ATT_HKV * ATT_HD
W_SSM = BRANCH_W
W_SG = 2 * BRANCH_W
W_DN_QKV = 3 * BRANCH_W
W_DN = W_DN_QKV + BRANCH_W + LANES
N_MAIN = W_ATT + W_SSM + W_SG + W_DN_QKV + BRANCH_W + 2 * DN_H
W_MAIN = W_ATT + W_SSM + W_SG + W_DN
GATE_COL0 = 3 * D_MODEL

SSM_BLOCK_CH = LANES
SSM_BLOCKS = BRANCH_W // SSM_BLOCK_CH
SSM_BLOCK_N = SSM_N // SSM_BLOCKS

ROWS_PROMPT = 512
FFN_TF = 1024
SAMPLE_ROWS = 8


def _params(*sem):
    return pltpu.CompilerParams(dimension_semantics=sem, vmem_limit_bytes=V7X_VMEM_LIMIT)


def _sds(shape, dtype):
    return jax.ShapeDtypeStruct(shape, dtype)


def _dot(a, b, precision=None):
    return jnp.dot(a, b, preferred_element_type=F32, precision=precision)


def _dot_nt(a, b, precision=None):
    return lax.dot_general(a, b, (((1,), (1,)), ((), ())), preferred_element_type=F32, precision=precision)


def _rms(x, g):
    ms = jnp.mean(x * x, axis=-1, keepdims=True)
    return x * lax.rsqrt(ms + EPS) * g


def _split2(x):
    hi = x.astype(BF16)
    return hi, (x - hi.astype(F32)).astype(BF16)


def _split_dot(x, e):
    hi, lo = _split2(x)
    return _dot(hi, e) + _dot(lo, e)


def _rmsnorm_body(x_ref, g_ref, o_ref):
    o_ref[...] = _rms(x_ref[...], g_ref[...]).astype(o_ref.dtype)


def _rmsnorm_call(x, g, tm):
    m, d = x.shape
    return pl.pallas_call(
        _rmsnorm_body,
        out_shape=_sds((m, d), BF16),
        grid=(m // tm,),
        in_specs=[pl.BlockSpec((tm, d), lambda i: (i, 0)), pl.BlockSpec((1, d), lambda i: (0, 0))],
        out_specs=pl.BlockSpec((tm, d), lambda i: (i, 0)),
        compiler_params=_params("parallel"),
        name="rmsnorm",
    )(x, g)


def _in_proj_body(h_ref, w_ref, *out_refs):
    h = h_ref[...]
    off = 0
    for o_ref in out_refs:
        width = o_ref.shape[1]
        o_ref[...] = _dot_nt(h, w_ref[off:off + width, :])
        off += width


def _in_proj_call(h, w_all, layer, tm):
    m, k = h.shape
    widths = (W_ATT, W_SSM, W_SG, W_DN)
    row = lambda i: (i, 0)
    return pl.pallas_call(
        _in_proj_body,
        out_shape=tuple(_sds((m, n), F32) for n in widths),
        grid=(m // tm,),
        in_specs=[pl.BlockSpec((tm, k), row),
                  pl.BlockSpec((None, W_MAIN, k), lambda i: (layer, 0, 0), pipeline_mode=pl.Buffered(1))],
        out_specs=tuple(pl.BlockSpec((tm, n), row) for n in widths),
        compiler_params=_params("parallel"),
        name="in_proj",
    )(h, w_all)


def _merge_body(h_ref, y_ref, wg_ref, wb_ref, o_ref, acc_ref):
    n = pl.program_id(1)

    @pl.when(n == 0)
    def _():
        acc_ref[...] = jnp.zeros_like(acc_ref)

    gate = jax.nn.sigmoid(_dot_nt(h_ref[...], wg_ref[...]))
    acc_ref[...] += gate * _dot(y_ref[...], wb_ref[...])

    @pl.when(n == N_BRANCH - 1)
    def _():
        o_ref[...] = acc_ref[...].astype(o_ref.dtype)


def _merge_call(h, ys, w_all, wb, layer, tm):
    m, d = h.shape
    gate0 = GATE_COL0 // d
    return pl.pallas_call(
        _merge_body,
        out_shape=_sds((m, d), BF16),
        grid=(m // tm, N_BRANCH),
        in_specs=[
            pl.BlockSpec((tm, d), lambda i, n: (i, 0)),
            pl.BlockSpec((None, tm, BRANCH_W), lambda i, n: (n, i, 0)),
            pl.BlockSpec((None, d, d), lambda i, n: (layer, gate0 + n, 0)),
            pl.BlockSpec((None, None, BRANCH_W, d), lambda i, n: (layer, n, 0, 0)),
        ],
        out_specs=pl.BlockSpec((tm, d), lambda i, n: (i, 0)),
        scratch_shapes=[pltpu.VMEM((tm, d), F32)],
        compiler_params=_params("parallel", "arbitrary"),
        name="merge",
    )(h, ys, w_all, wb)


def _proj_body(m_ref, w_ref, x_ref, g_ref, xo_ref, ho_ref):
    xn = x_ref[...] + _dot(m_ref[...], w_ref[...])
    xo_ref[...] = xn
    ho_ref[...] = _rms(xn, g_ref[...]).astype(ho_ref.dtype)


def _proj_call(mix, w, layer, x, g, tm):
    m, d = x.shape
    return pl.pallas_call(
        _proj_body,
        out_shape=(_sds((m, d), F32), _sds((m, d), BF16)),
        grid=(m // tm,),
        in_specs=[
            pl.BlockSpec((tm, d), lambda i: (i, 0)),
            pl.BlockSpec((None, d, d), lambda i: (layer, 0, 0)),
            pl.BlockSpec((tm, d), lambda i: (i, 0)),
            pl.BlockSpec((1, d), lambda i: (0, 0)),
        ],
        out_specs=(pl.BlockSpec((tm, d), lambda i: (i, 0)), pl.BlockSpec((tm, d), lambda i: (i, 0))),
        compiler_params=_params("parallel"),
        name="out_proj",
    )(mix, w, x, g)


def _ffn_body(h_ref, w1_ref, w2_ref, x_ref, g_ref, xo_ref, ho_ref):
    f = pl.program_id(1)

    @pl.when(f == 0)
    def _():
        xo_ref[...] = x_ref[...]

    a = _dot(h_ref[...], w1_ref[...])
    a = jnp.square(jnp.maximum(a, 0.0)).astype(BF16)
    xo_ref[...] += _dot(a, w2_ref[...])

    @pl.when(f == pl.num_programs(1) - 1)
    def _():
        ho_ref[...] = _rms(xo_ref[...], g_ref[...]).astype(ho_ref.dtype)


def _ffn_call(h, w1, w2, layer, x, g_next, tm, tf):
    m, d = x.shape
    dff = w1.shape[2]
    return pl.pallas_call(
        _ffn_body,
        out_shape=(_sds((m, d), F32), _sds((m, d), BF16)),
        grid=(m // tm, dff // tf),
        in_specs=[
            pl.BlockSpec((tm, d), lambda i, f: (i, 0)),
            pl.BlockSpec((None, d, tf), lambda i, f: (layer, 0, f)),
            pl.BlockSpec((None, tf, d), lambda i, f: (layer, f, 0)),
            pl.BlockSpec((tm, d), lambda i, f: (i, 0)),
            pl.BlockSpec((1, d), lambda i, f: (0, 0)),
        ],
        out_specs=(pl.BlockSpec((tm, d), lambda i, f: (i, 0)), pl.BlockSpec((tm, d), lambda i, f: (i, 0))),
        compiler_params=_params("parallel", "arbitrary"),
        name="ffn",
    )(h, w1, w2, x, g_next)


def _head_norm_rope(x, g, cos, sin, e):
    w = x.shape[1]
    ss = _split_dot(x * x, e)
    xn = x * lax.rsqrt(ss * (1.0 / ATT_HD) + EPS) * g
    lane = lax.broadcasted_iota(jnp.int32, x.shape, 1)
    first_half = (lane % ATT_HD) < (ATT_HD // 2)
    rot = jnp.where(first_half, pltpu.roll(xn, w - ATT_HD // 2, 1), pltpu.roll(xn, ATT_HD // 2, 1))
    return xn * cos + rot * sin


def _qkv_norm_rope(z, cos, sin, qg_ref, kg_ref, e_ref):
    nq = ATT_HQ * ATT_HD
    nk = ATT_HKV * ATT_HD
    rep = nq // LANES
    q = _head_norm_rope(z[:, :nq], qg_ref[...], jnp.tile(cos, (1, rep)), jnp.tile(sin, (1, rep)), e_ref[...])
    k = _head_norm_rope(z[:, nq:nq + nk], kg_ref[...], cos, sin, e_ref[:nk, :nk])
    return q, k, z[:, nq + nk:]


def _half_placed(x):
    lane = lax.broadcasted_iota(jnp.int32, x.shape, 1)
    lo = lane < ATT_HD
    xr = pltpu.roll(x, ATT_HD, 1)
    zero = jnp.zeros_like(x)
    return ((jnp.where(lo, x, zero), jnp.where(lo, zero, xr)),
            (jnp.where(lo, xr, zero), jnp.where(lo, zero, x)))


def _attn_prompt_body(z_ref, cos_ref, sin_ref, qg_ref, kg_ref, e_ref, sink_ref,
                      y_ref, ko_ref, vo_ref, kprev_ref, vprev_ref):
    i = pl.program_id(1)

    @pl.when(i == 0)
    def _():
        kprev_ref[...] = jnp.zeros_like(kprev_ref)
        vprev_ref[...] = jnp.zeros_like(vprev_ref)

    q, k, v = _qkv_norm_rope(z_ref[...], cos_ref[...], sin_ref[...], qg_ref, kg_ref, e_ref)
    ko_ref[0] = k
    vo_ref[0] = v
    kcat = jnp.concatenate([kprev_ref[...], k], axis=0)
    vcat = jnp.concatenate([vprev_ref[...], v], axis=0)
    kprev_ref[...] = k
    vprev_ref[...] = v
    kparts = _half_placed(kcat)
    vparts = _half_placed(vcat)
    qb = q.astype(BF16)

    rows = 2 * WINDOW
    r = lax.broadcasted_iota(jnp.int32, (rows, rows), 0) % WINDOW
    c = lax.broadcasted_iota(jnp.int32, (rows, rows), 1)
    rel = r - c + WINDOW
    mask = (rel >= 0) & (rel < WINDOW) & ((c >= WINDOW) | (i > 0))
    top = lax.broadcasted_iota(jnp.int32, (rows, 1), 0) < WINDOW

    qst = [jnp.concatenate([qb[:, LANES * (2 * kv):LANES * (2 * kv + 1)],
                            qb[:, LANES * (2 * kv + 1):LANES * (2 * kv + 2)]], axis=0) for kv in range(ATT_HKV)]
    combos = [(kv, parity) for kv in range(ATT_HKV) for parity in range(2)]
    scores = [_dot_nt(qst[kv], kparts[kv][parity].astype(BF16)) * (ATT_HD ** -0.5) for kv, parity in combos]
    scores = [jnp.where(mask, s, NEG_INF) for s in scores]
    sinks = [jnp.where(top, sink_ref[ATT_G * kv + parity], sink_ref[ATT_G * kv + 2 + parity]) for kv, parity in combos]
    maxes = [jnp.maximum(jnp.max(s, axis=-1, keepdims=True), sink) for s, sink in zip(scores, sinks)]
    exps = [jnp.exp(s - mx) for s, mx in zip(scores, maxes)]
    dens = [jnp.sum(ex, axis=-1, keepdims=True) + jnp.exp(sink - mx) for ex, sink, mx in zip(exps, sinks, maxes)]
    probs = [(ex / den).astype(BF16) for ex, den in zip(exps, dens)]
    outs = [_dot(p, vparts[kv][parity].astype(BF16)) for p, (kv, parity) in zip(probs, combos)]
    pair_out = []
    for kv in range(ATT_HKV):
        acc = outs[2 * kv] + outs[2 * kv + 1]
        pair_out += [acc[:WINDOW], acc[WINDOW:]]
    y_ref[...] = jnp.concatenate(pair_out, axis=1).astype(y_ref.dtype)


def _attn_prompt_call(z_att, cos, sin, qg, kg, e, sink, batch, seq):
    nb = seq // WINDOW
    nkv = ATT_HKV * ATT_HD
    return pl.pallas_call(
        _attn_prompt_body,
        out_shape=(_sds((N_BRANCH, batch * seq, BRANCH_W), BF16), _sds((batch, WINDOW, nkv), F32),
                   _sds((batch, WINDOW, nkv), F32)),
        grid=(batch, nb),
        in_specs=[
            pl.BlockSpec((WINDOW, W_ATT), lambda b, i: (b * nb + i, 0)),
            pl.BlockSpec((WINDOW, LANES), lambda b, i: (i, 0)),
            pl.BlockSpec((WINDOW, LANES), lambda b, i: (i, 0)),
            pl.BlockSpec((1, BRANCH_W), lambda b, i: (0, 0)),
            pl.BlockSpec((1, nkv), lambda b, i: (0, 0)),
            pl.BlockSpec((BRANCH_W, BRANCH_W), lambda b, i: (0, 0)),
            pl.BlockSpec(memory_space=pltpu.SMEM),
        ],
        out_specs=(
            pl.BlockSpec((None, WINDOW, BRANCH_W), lambda b, i: (0, b * nb + i, 0)),
            pl.BlockSpec((1, WINDOW, nkv), lambda b, i: (b, 0, 0)),
            pl.BlockSpec((1, WINDOW, nkv), lambda b, i: (b, 0, 0)),
        ),
        scratch_shapes=[pltpu.VMEM((WINDOW, nkv), F32), pltpu.VMEM((WINDOW, nkv), F32)],
        compiler_params=_params("arbitrary", "arbitrary"),
        name="attn_prompt",
    )(z_att, cos, sin, qg, kg, e, sink)


def _attn_sample_body(z_ref, ck_ref, cv_ref, cos_ref, sin_ref, qg_ref, kg_ref, e_ref, sink_ref,
                      y_ref, ko_ref, vo_ref):
    nrow = z_ref.shape[0]
    q_all, k_all, v_all = _qkv_norm_rope(z_ref[...], cos_ref[...], sin_ref[...], qg_ref, kg_ref, e_ref)
    lane = lax.broadcasted_iota(jnp.int32, (1, LANES), 1)
    lo = lane < ATT_HD
    row = lax.broadcasted_iota(jnp.int32, (ATT_HQ, LANES), 0)
    scale = ATT_HD ** -0.5
    sink = sink_ref[:, 0:1]
    rows = range(nrow)
    ks = [k_all[i:i + 1, :] for i in rows]
    vs = [v_all[i:i + 1, :] for i in rows]
    for i in rows:
        ko_ref[i, 0:WINDOW - 1, :] = ck_ref[i, 1:WINDOW, :]
        ko_ref[i, WINDOW - 1:WINDOW, :] = ks[i]
        vo_ref[i, 0:WINDOW - 1, :] = cv_ref[i, 1:WINDOW, :]
        vo_ref[i, WINDOW - 1:WINDOW, :] = vs[i]

    qexps = []
    for i in rows:
        qexp = jnp.zeros((ATT_HQ, LANES), F32)
        for h in range(ATT_HQ):
            pair = q_all[i:i + 1, LANES * (h // 2):LANES * (h // 2 + 1)]
            want_lo = (h // ATT_G) == 0
            have_lo = (h % 2) == 0
            src = pair if want_lo == have_lo else pltpu.roll(pair, ATT_HD, 1)
            placed = jnp.where(lo if want_lo else jnp.logical_not(lo), src, 0.0)
            qexp = jnp.where(row == h, jnp.broadcast_to(placed, (ATT_HQ, LANES)), qexp)
        qexps.append(qexp)
    qbs = [qexp.astype(BF16) for qexp in qexps]
    key = lax.broadcasted_iota(jnp.int32, (ATT_HQ, WINDOW), 1)
    scores = [jnp.where(key >= 1, _dot_nt(qb, ck_ref[i].astype(BF16)) * scale, NEG_INF) for i, qb in zip(rows, qbs)]
    s_news = [jnp.sum(qb.astype(F32) * k.astype(BF16).astype(F32), axis=-1, keepdims=True) * scale
              for qb, k in zip(qbs, ks)]
    maxes = [jnp.maximum(jnp.maximum(jnp.max(s, axis=-1, keepdims=True), sn), sink) for s, sn in zip(scores, s_news)]
    exps = [jnp.exp(s - mx) for s, mx in zip(scores, maxes)]
    ex_news = [jnp.exp(sn - mx) for sn, mx in zip(s_news, maxes)]
    dens = [jnp.sum(ex, axis=-1, keepdims=True) + en + jnp.exp(sink - mx) for ex, en, mx in zip(exps, ex_news, maxes)]
    probs = [(ex / den).astype(BF16) for ex, den in zip(exps, dens)]
    p_news = [(en / den).astype(BF16).astype(F32) for en, den in zip(ex_news, dens)]
    outs = [_dot(p, cv_ref[i].astype(BF16)) + pn * v.astype(BF16).astype(F32)
            for i, p, pn, v in zip(rows, probs, p_news, vs)]

    ys = []
    for o in outs:
        pairs = []
        for j in range(ATT_HQ // 2):
            parts = []
            for par in range(2):
                h = 2 * j + par
                orow = o[h:h + 1, :]
                have_lo = (h // ATT_G) == 0
                want_lo = par == 0
                src = orow if want_lo == have_lo else pltpu.roll(orow, ATT_HD, 1)
                parts.append(jnp.where(lo if want_lo else jnp.logical_not(lo), src, 0.0))
            pairs.append(parts[0] + parts[1])
        ys.append(jnp.concatenate(pairs, axis=1))
    y_ref[...] = jnp.concatenate(ys, axis=0).astype(y_ref.dtype)


def _attn_sample_call(z_att, ck, cv, cos, sin, qg, kg, e, sink_b):
    nbatch = z_att.shape[0]
    nkv = ATT_HKV * ATT_HD
    rows = SAMPLE_ROWS
    const = lambda b: (0, 0)
    return pl.pallas_call(
        _attn_sample_body,
        out_shape=(_sds((nbatch, BRANCH_W), F32), _sds((nbatch, WINDOW, nkv), F32),
                   _sds((nbatch, WINDOW, nkv), F32)),
        grid=(nbatch // rows,),
        in_specs=[
            pl.BlockSpec((rows, W_ATT), lambda b: (b, 0)),
            pl.BlockSpec((rows, WINDOW, nkv), lambda b: (b, 0, 0)),
            pl.BlockSpec((rows, WINDOW, nkv), lambda b: (b, 0, 0)),
            pl.BlockSpec((1, LANES), const),
            pl.BlockSpec((1, LANES), const),
            pl.BlockSpec((1, BRANCH_W), const),
            pl.BlockSpec((1, nkv), const),
            pl.BlockSpec((BRANCH_W, BRANCH_W), const),
            pl.BlockSpec((ATT_HQ, LANES), const),
        ],
        out_specs=(
            pl.BlockSpec((rows, BRANCH_W), lambda b: (b, 0)),
            pl.BlockSpec((rows, WINDOW, nkv), lambda b: (b, 0, 0)),
            pl.BlockSpec((rows, WINDOW, nkv), lambda b: (b, 0, 0)),
        ),
        compiler_params=_params("parallel"),
        name="attn_sample",
    )(z_att, ck, cv, cos, sin, qg, kg, e, sink_b)


def _cmul(ar, ai, br, bi):
    return ar * br - ai * bi, ar * bi + ai * br


def _ssm_in(ub, bm_ref, j):
    blk = slice(j * SSM_BLOCK_CH, (j + 1) * SSM_BLOCK_CH)
    return _dot(ub[:, blk], bm_ref[blk, :])


def _ssm_out(s_re, s_im, cm_ref, j):
    return (_dot(s_re.astype(BF16), cm_ref[j * SSM_BLOCK_N:(j + 1) * SSM_BLOCK_N, :])
            + _dot(s_im.astype(BF16), cm_ref[SSM_N + j * SSM_BLOCK_N:SSM_N + (j + 1) * SSM_BLOCK_N, :]))


def _ssm_tail(y, u, d_ref, gw_ref, gb_ref):
    y = jax.nn.gelu(y + d_ref[...] * u)
    return y * jax.nn.sigmoid(_dot(y.astype(BF16), gw_ref[...]) + gb_ref[...])


def _ssm_prompt_body(u_ref, perm_ref, unperm_ref, bm_ref, cm_ref, a_ref, d_ref, gw_ref, gb_ref, ys_in_ref,
                     y_ref, sre_ref, sim_ref,
                     bu_ref, start_ref, carry_ref):
    del ys_in_ref
    tb = u_ref.shape[0]
    ts = tb // SSM_SEGS
    step_j = pl.program_id(1)

    @pl.when(step_j == 0)
    def _():
        carry_ref[...] = jnp.zeros_like(carry_ref)

    u = u_ref[...]
    u_hi, u_mid = _split2(u)
    u_lo = (u - u_hi.astype(F32) - u_mid.astype(F32)).astype(BF16)
    perm = perm_ref[...]
    up = _dot(perm, u_hi) + (_dot(perm, u_mid) + _dot(perm, u_lo))
    ub = up.astype(BF16)
    cw = SSM_BLOCK_N
    for j in range(SSM_BLOCKS):
        bu = _ssm_in(ub, bm_ref, j)
        bu_ref[:, j * cw:(j + 1) * cw] = bu[:, :cw]
        bu_ref[:, SSM_N + j * cw:SSM_N + (j + 1) * cw] = bu[:, cw:]

    def scan(c0, init_re, init_im, store):
        ar = jnp.broadcast_to(a_ref[0:1, c0:c0 + cw], (SUBLANES, cw))
        ai = jnp.broadcast_to(a_ref[1:2, c0:c0 + cw], (SUBLANES, cw))

        def step(t, st):
            sr, si = st
            r0 = pl.multiple_of(t * SUBLANES, SUBLANES)
            br = bu_ref[pl.ds(r0, SUBLANES), c0:c0 + cw]
            bi = bu_ref[pl.ds(r0, SUBLANES), SSM_N + c0:SSM_N + c0 + cw]
            pr, pi = _cmul(ar, ai, sr, si)
            sr, si = pr + br, pi + bi
            if store:
                bu_ref[pl.ds(r0, SUBLANES), c0:c0 + cw] = sr
                bu_ref[pl.ds(r0, SUBLANES), SSM_N + c0:SSM_N + c0 + cw] = si
            return sr, si

        return lax.fori_loop(0, ts, step, (init_re, init_im))

    zero = jnp.zeros((SUBLANES, cw), F32)
    ys = []
    for j in range(SSM_BLOCKS):
        c0 = j * cw
        er, ei = scan(c0, zero, zero, False)
        pr, pi = a_ref[0:1, c0:c0 + cw], a_ref[1:2, c0:c0 + cw]
        n = 1
        while n < ts:
            pr, pi = _cmul(pr, pi, pr, pi)
            n *= 2
        sr = carry_ref[0:1, c0:c0 + cw]
        si = carry_ref[1:2, c0:c0 + cw]
        for s in range(SSM_SEGS):
            start_ref[s:s + 1, c0:c0 + cw] = sr
            start_ref[s:s + 1, SSM_N + c0:SSM_N + c0 + cw] = si
            qr, qi = _cmul(pr, pi, sr, si)
            sr, si = qr + er[s:s + 1, :], qi + ei[s:s + 1, :]
        carry_ref[0:1, c0:c0 + cw] = sr
        carry_ref[1:2, c0:c0 + cw] = si
        scan(c0, start_ref[:, c0:c0 + cw], start_ref[:, SSM_N + c0:SSM_N + c0 + cw], True)
        ys.append(_ssm_out(bu_ref[:, c0:c0 + cw], bu_ref[:, SSM_N + c0:SSM_N + c0 + cw], cm_ref, j))

    sre_ref[0] = carry_ref[0:1, :]
    sim_ref[0] = carry_ref[1:2, :]
    y = _ssm_tail(jnp.concatenate(ys, axis=1), up, d_ref, gw_ref, gb_ref).astype(BF16)
    y_ref[...] = _dot(unperm_ref[...], y).astype(y_ref.dtype)


def _ssm_prompt_call(z_u, bm, cm, a, d, gw, gb, layer, ys, batch, seq, tb):
    nblk = seq // tb
    const = lambda b, j: (0, 0)
    of_layer = lambda b, j: (layer, 0, 0)
    ts = tb // SSM_SEGS
    src = (jnp.arange(tb) % SSM_SEGS) * ts + jnp.arange(tb) // SSM_SEGS
    perm = (src[:, None] == jnp.arange(tb)[None, :]).astype(BF16)
    return pl.pallas_call(
        _ssm_prompt_body,
        out_shape=(_sds(ys.shape, ys.dtype), _sds((batch, 1, SSM_N), F32), _sds((batch, 1, SSM_N), F32)),
        grid=(batch, nblk),
        in_specs=[
            pl.BlockSpec((tb, BRANCH_W), lambda b, j: (b * nblk + j, 0)),
            pl.BlockSpec((tb, tb), const),
            pl.BlockSpec((tb, tb), const),
            pl.BlockSpec((None, BRANCH_W, 2 * SSM_BLOCK_N), of_layer),
            pl.BlockSpec((None, 2 * SSM_N, SSM_BLOCK_CH), of_layer),
            pl.BlockSpec((2, SSM_N), const),
            pl.BlockSpec((1, BRANCH_W), const),
            pl.BlockSpec((None, BRANCH_W, BRANCH_W), of_layer),
            pl.BlockSpec((1, BRANCH_W), const),
            pl.BlockSpec(memory_space=pl.ANY),
        ],
        out_specs=(
            pl.BlockSpec((None, tb, BRANCH_W), lambda b, j: (1, b * nblk + j, 0)),
            pl.BlockSpec((1, 1, SSM_N), lambda b, j: (b, 0, 0)),
            pl.BlockSpec((1, 1, SSM_N), lambda b, j: (b, 0, 0)),
        ),
        input_output_aliases={9: 0},
        scratch_shapes=[
            pltpu.VMEM((tb, 2 * SSM_N), F32),
            pltpu.VMEM((SSM_SEGS, 2 * SSM_N), F32),
            pltpu.VMEM((2, SSM_N), F32),
        ],
        compiler_params=_params("arbitrary", "arbitrary"),
        name="ssm_prompt",
    )(z_u, perm, perm.T, bm, cm, a, d, gw, gb, ys)


def _ssm_sample_body(u_ref, s0r_ref, s0i_ref, bm_ref, cm_ref, a_ref, d_ref, gw_ref, gb_ref,
                     y_ref, sre_ref, sim_ref):
    u = u_ref[...]
    ub = u.astype(BF16)
    cw = SSM_BLOCK_N
    ys = []
    for j in range(SSM_BLOCKS):
        cols = slice(j * cw, (j + 1) * cw)
        bu = _ssm_in(ub, bm_ref, j)
        pr, pi = _cmul(a_ref[0:1, cols], a_ref[1:2, cols], s0r_ref[:, cols], s0i_ref[:, cols])
        sr = pr + bu[:, :cw]
        si = pi + bu[:, cw:]
        sre_ref[:, cols] = sr
        sim_ref[:, cols] = si
        ys.append(_ssm_out(sr, si, cm_ref, j))
    y_ref[...] = _ssm_tail(jnp.concatenate(ys, axis=1), u, d_ref, gw_ref, gb_ref).astype(y_ref.dtype)


def _ssm_sample_call(z_u, s0r, s0i, bm, cm, a, d, gw, gb, layer):
    nbatch = z_u.shape[0]
    const = lambda i: (0, 0)
    of_layer = lambda i: (layer, 0, 0)
    rows_w = pl.BlockSpec((nbatch, BRANCH_W), const)
    rows_n = pl.BlockSpec((nbatch, SSM_N), const)
    return pl.pallas_call(
        _ssm_sample_body,
        out_shape=(_sds((nbatch, BRANCH_W), F32), _sds((nbatch, SSM_N), F32), _sds((nbatch, SSM_N), F32)),
        grid=(1,),
        in_specs=[
            rows_w, rows_n, rows_n,
            pl.BlockSpec((None, BRANCH_W, 2 * SSM_BLOCK_N), of_layer),
            pl.BlockSpec((None, 2 * SSM_N, SSM_BLOCK_CH), of_layer),
            pl.BlockSpec((2, SSM_N), const),
            pl.BlockSpec((1, BRANCH_W), const),
            pl.BlockSpec((None, BRANCH_W, BRANCH_W), of_layer),
            pl.BlockSpec((1, BRANCH_W), const),
        ],
        out_specs=(rows_w, rows_n, rows_n),
        compiler_params=_params("arbitrary"),
        name="ssm_sample",
    )(z_u, s0r, s0i, bm, cm, a, d, gw, gb)


def _sg_prompt_body(z_ref, g_ref, w_ref, b_ref, ys_in_ref, y_ref, v_ref):
    del ys_in_ref
    nch = z_ref.shape[0] // CHUNK
    uv = jax.nn.gelu(z_ref[...])
    u = uv[:, :BRANCH_W]
    v = _rms(uv[:, BRANCH_W:], g_ref[...])
    v_ref[0] = v[(nch - 1) * CHUNK:, :]
    t = lax.broadcasted_iota(jnp.int32, (CHUNK, CHUNK), 0)
    s = lax.broadcasted_iota(jnp.int32, (CHUNK, CHUNK), 1)
    causal = t >= s
    vb = v.astype(BF16)
    ws = [jnp.where(causal, w_ref[g], 0.0).astype(BF16) for g in range(SG_GROUPS)]
    for ci in range(nch):
        rows = slice(ci * CHUNK, (ci + 1) * CHUNK)
        mixed = jnp.concatenate([_dot(ws[g], vb[rows, g * SG_GW:(g + 1) * SG_GW]) for g in range(SG_GROUPS)], axis=1)
        y_ref[rows, :] = (u[rows, :] * (mixed + b_ref[...])).astype(y_ref.dtype)


def _sg_prompt_call(z_c, g, w, b_exp, ys, batch, seq):
    rows = SG_STEP_CHUNKS * CHUNK if seq % (SG_STEP_CHUNKS * CHUNK) == 0 else CHUNK
    nblk = seq // rows
    return pl.pallas_call(
        _sg_prompt_body,
        out_shape=(_sds(ys.shape, ys.dtype), _sds((batch, CHUNK, BRANCH_W), F32)),
        grid=(batch, nblk),
        in_specs=[
            pl.BlockSpec((rows, W_SG), lambda b, i: (b * nblk + i, 0)),
            pl.BlockSpec((1, BRANCH_W), lambda b, i: (0, 0)),
            pl.BlockSpec((SG_GROUPS, CHUNK, CHUNK), lambda b, i: (0, 0, 0)),
            pl.BlockSpec((CHUNK, BRANCH_W), lambda b, i: (0, 0)),
            pl.BlockSpec(memory_space=pl.ANY),
        ],
        out_specs=(
            pl.BlockSpec((None, rows, BRANCH_W), lambda b, i: (2, b * nblk + i, 0)),
            pl.BlockSpec((1, CHUNK, BRANCH_W), lambda b, i: (b, 0, 0)),
        ),
        input_output_aliases={4: 0},
        compiler_params=_params("arbitrary", "arbitrary"),
        name="sg_prompt",
    )(z_c, g, w, b_exp, ys)


def _sg_sample_body(z_ref, g_ref, w00_ref, b_ref, y_ref, v_ref):
    uv = jax.nn.gelu(z_ref[...])
    u = uv[:, :BRANCH_W]
    v = _rms(uv[:, BRANCH_W:], g_ref[...])
    v_ref[...] = v
    w = w00_ref[...].astype(BF16).astype(F32)
    mixed = w * v.astype(BF16).astype(F32) + b_ref[0:1, :]
    y_ref[...] = (u * mixed).astype(y_ref.dtype)


def _sg_sample_call(z_c, g, w00, b_exp):
    nbatch = z_c.shape[0]
    return pl.pallas_call(
        _sg_sample_body,
        out_shape=(_sds((nbatch, BRANCH_W), F32), _sds((nbatch, BRANCH_W), F32)),
        name="sg_sample",
    )(z_c, g, w00, b_exp)


def _l2n(x):
    return x * lax.rsqrt(jnp.sum(x * x, axis=-1, keepdims=True) + EPS)


def _dn_gates(zab, alog_ref, dtb_ref):
    g = -jnp.exp(alog_ref[...]) * jax.nn.softplus(zab + dtb_ref[...])
    beta = jax.nn.sigmoid(zab)
    return g, beta


def _mm3(a, b):
    (ah, al), (bh, bl) = a, b
    return _dot(ah, bh) + (_dot(ah, bl) + _dot(al, bh))


def _unit_lower_inverse_off(mats, blk):
    n = mats[0].shape[0]
    r = lax.broadcasted_iota(jnp.int32, (n, n), 0)
    c = lax.broadcasted_iota(jnp.int32, (n, n), 1)

    def lower_left(size):
        return ((r // (2 * size)) == (c // (2 * size))) & ((r // size) % 2 == 1) & ((c // size) % 2 == 0)

    first = lower_left(1)
    offs = [-jnp.where(first, a, 0.0) for a in mats]
    size = 2
    while size < blk:
        mask = lower_left(size)
        ams = [jnp.where(mask, a, 0.0) for a in mats]
        obs = [off.astype(BF16) for off in offs]
        ps = [am + _dot(ob, am.astype(BF16)) for am, ob in zip(ams, obs)]
        offs = [off - (p + _dot(p.astype(BF16), ob)) for off, p, ob in zip(offs, ps, obs)]
        size *= 2
    return offs


def _head_lanes(x, rows_per_head):
    r = lax.broadcasted_iota(jnp.int32, x.shape, 0) // rows_per_head
    return jnp.concatenate([jnp.where(r == h, x, 0.0) for h in range(DN_H)], axis=1)


def _dn_chunk_terms(chunks):
    tc = DN_CHUNK
    n = DN_H * tc
    heads = range(DN_H)

    def stack(x, rows, off):
        return jnp.concatenate([x[rows, off + h * DN_HD:off + (h + 1) * DN_HD] for h in heads], axis=0)

    rows = [slice(r0, r0 + tc) for *_, r0 in chunks]
    qs = [_l2n(stack(ch[0], rw, 0)) * (DN_HD ** -0.5) for ch, rw in zip(chunks, rows)]
    ks = [_l2n(stack(ch[0], rw, BRANCH_W)) for ch, rw in zip(chunks, rows)]
    vs = [stack(ch[0], rw, 2 * BRANCH_W) for ch, rw in zip(chunks, rows)]
    zgss = [stack(ch[1], rw, 0) for ch, rw in zip(chunks, rows)]

    rt = lax.broadcasted_iota(jnp.int32, (tc, tc), 0)
    ct = lax.broadcasted_iota(jnp.int32, (tc, tc), 1)
    ones_tril = jnp.where(rt >= ct, 1.0, 0.0).astype(BF16)
    gparts = []
    for ch, rw in zip(chunks, rows):
        g = ch[2][rw]
        g_hi, g_lo = _split2(g)
        gparts.append((g_hi, g_lo, (g - g_hi.astype(F32) - g_lo.astype(F32)).astype(BF16)))
    gc_cs = [_dot(ones_tril, hi) + (_dot(ones_tril, lo) + _dot(ones_tril, lo2)) for hi, lo, lo2 in gparts]
    gcs = [jnp.concatenate([gc_c[:, h:h + 1] for h in heads], axis=0) for gc_c in gc_cs]
    betas = [jnp.concatenate([ch[3][rw, DN_H + h:DN_H + h + 1] for h in heads], axis=0)
             for ch, rw in zip(chunks, rows)]
    gc_lasts = [jnp.concatenate([jnp.broadcast_to(gc_c[tc - 1:tc, h:h + 1], (tc, 1)) for h in heads], axis=0)
                for gc_c in gc_cs]

    r = lax.broadcasted_iota(jnp.int32, (n, n), 0)
    c = lax.broadcasted_iota(jnp.int32, (n, n), 1)
    same = (r // tc) == (c // tc)
    tril = same & (r >= c)
    stril = same & (r > c)
    gmats = [jnp.broadcast_to(gc, (n, n)) for gc in gcs]
    decs = [jnp.where(tril, jnp.exp(jnp.where(tril, gm - gm.T, 0.0)), 0.0) for gm in gmats]
    kbs = [k.astype(BF16) for k in ks]
    kks = [_dot_nt(kb, kb) for kb in kbs]
    mats = [jnp.where(stril, beta * kk * dec, 0.0) for beta, kk, dec in zip(betas, kks, decs)]
    inv_offs = _unit_lower_inverse_off(mats, tc)

    egs = [jnp.exp(gc) for gc in gcs]
    rhss = [jnp.concatenate([beta * v, (beta * eg) * k], axis=1) for beta, eg, k, v in zip(betas, egs, ks, vs)]
    inv_parts = [_split2(x) for x in inv_offs]
    rhs_parts = [_split2(x) for x in rhss]
    sols = [rhs + _mm3(ip, rp) for rhs, ip, rp in zip(rhss, inv_parts, rhs_parts)]
    qks = [(_dot_nt(q.astype(BF16), kb) * dec).astype(BF16) for q, kb, dec in zip(qs, kbs, decs)]
    wqs = [jnp.concatenate([_head_lanes(sol[:, DN_HD:], tc), _head_lanes(q * eg, tc)], axis=0).astype(BF16)
           for sol, q, eg in zip(sols, qs, egs)]
    ke_ts = [_head_lanes(k * jnp.exp(gl - gc), tc).T.astype(BF16) for k, gl, gc in zip(ks, gc_lasts, gcs)]
    ges = [jnp.concatenate([jnp.broadcast_to(jnp.exp(gc_c[tc - 1:tc, h:h + 1]), (DN_HD, 1)) for h in heads], axis=0)
           for gc_c in gc_cs]
    return [(sol[:, :DN_HD], wq, qk, ke_t, ge, zgs)
            for sol, wq, qk, ke_t, ge, zgs in zip(sols, wqs, qks, ke_ts, ges, zgss)]


def _dn_chunk_steps(states, terms, ng):
    n = terms[0][0].shape[0]
    wss = [_dot(t[1], s.astype(BF16)) for s, t in zip(states, terms)]
    ubs = [(t[0] - ws[:n]).astype(BF16) for ws, t in zip(wss, terms)]
    outs = [ws[n:] + _dot(t[2], ub) for ws, ub, t in zip(wss, ubs, terms)]
    states = [t[4] * s + _dot(t[3], ub) for s, ub, t in zip(states, ubs, terms)]
    return states, [_rms(o, ng) * jax.nn.silu(t[5]) for o, t in zip(outs, terms)]


def _dn_prompt_body(x_ref, zg_ref, zab_ref, cw_ref, alog_ref, dtb_ref, ng_ref, ys_in_ref,
                    y_ref, so_ref, co_ref, xin_ref, s_ref):
    del ys_in_ref
    step = pl.program_id(0)
    nb, tb = x_ref.shape[0], x_ref.shape[1]
    tc = DN_CHUNK
    nci = tb // tc
    pad = SUBLANES

    @pl.when(step == 0)
    def _():
        xin_ref[:, 0:pad, :] = jnp.zeros((nb, pad, W_DN_QKV), F32)
        s_ref[...] = jnp.zeros_like(s_ref)

    @pl.when(step > 0)
    def _():
        xin_ref[:, 0:pad, :] = xin_ref[:, tb:tb + pad, :]

    chunks = []
    for b in range(nb):
        x = x_ref[b]
        xin_ref[b, pad:pad + tb, :] = x
        co_ref[b] = x_ref[b, tb - (DN_CONV - 1):tb, :]
        acc = cw_ref[DN_CONV - 1:DN_CONV, :] * x
        for i in range(DN_CONV - 1):
            lag = DN_CONV - 1 - i
            acc = acc + cw_ref[i:i + 1, :] * xin_ref[b, pad - lag:pad - lag + tb, :]
        qkv = jax.nn.silu(acc)
        g_all, beta_all = _dn_gates(zab_ref[b], alog_ref, dtb_ref)
        chunks += [(qkv, zg_ref[b], g_all, beta_all, ci * tc) for ci in range(nci)]
    terms = _dn_chunk_terms(chunks)

    states = [s_ref[b] for b in range(nb)]
    for ci in range(nci):
        states, outs = _dn_chunk_steps(states, [terms[b * nci + ci] for b in range(nb)], ng_ref[...])
        for b in range(nb):
            y_ref[b, ci * tc:(ci + 1) * tc, :] = jnp.concatenate(
                [outs[b][h * tc:(h + 1) * tc] for h in range(DN_H)], axis=1).astype(y_ref.dtype)
    for b in range(nb):
        s_ref[b] = states[b]

    @pl.when(step == pl.num_programs(0) - 1)
    def _():
        so_ref[...] = s_ref[...]


def _dn_prompt_call(z_dn, cw, alog, dtb, ng, ys, batch, seq):
    tb = DN_STEP_CHUNKS * DN_CHUNK if seq % (DN_STEP_CHUNKS * DN_CHUNK) == 0 else DN_CHUNK
    z3 = z_dn.reshape(batch, seq, W_DN)
    ys4 = ys.reshape(N_BRANCH, batch, seq, BRANCH_W)
    const = lambda i: (0, 0)
    whole = lambda i: (0, 0, 0)
    y, so, co = pl.pallas_call(
        _dn_prompt_body,
        out_shape=(_sds(ys4.shape, ys4.dtype), _sds((batch, DN_H * DN_HD, DN_HD), F32),
                   _sds((batch, DN_CONV - 1, W_DN_QKV), F32)),
        grid=(seq // tb,),
        in_specs=[
            pl.BlockSpec((batch, tb, W_DN_QKV), lambda i: (0, i, 0)),
            pl.BlockSpec((batch, tb, BRANCH_W), lambda i: (0, i, W_DN_QKV // BRANCH_W)),
            pl.BlockSpec((batch, tb, LANES), lambda i: (0, i, (W_DN_QKV + BRANCH_W) // LANES)),
            pl.BlockSpec((DN_CONV, W_DN_QKV), const),
            pl.BlockSpec((1, LANES), const),
            pl.BlockSpec((1, LANES), const),
            pl.BlockSpec((1, DN_HD), const),
            pl.BlockSpec(memory_space=pl.ANY),
        ],
        out_specs=(
            pl.BlockSpec((None, batch, tb, BRANCH_W), lambda i: (N_BRANCH - 1, 0, i, 0)),
            pl.BlockSpec((batch, DN_H * DN_HD, DN_HD), whole),
            pl.BlockSpec((batch, DN_CONV - 1, W_DN_QKV), whole),
        ),
        input_output_aliases={7: 0},
        scratch_shapes=[pltpu.VMEM((batch, tb + SUBLANES, W_DN_QKV), F32),
                        pltpu.VMEM((batch, DN_H * DN_HD, DN_HD), F32)],
        compiler_params=_params("arbitrary"),
        name="dn_prompt",
    )(z3, z3, z3, cw, alog, dtb, ng, ys4)
    return y.reshape(ys.shape), so.reshape(batch, DN_H, DN_HD, DN_HD), co


def _dn_sample_body(x_ref, zg_ref, zab_ref, conv_ref, s0_ref, cw_ref, alog_ref, dtb_ref, ng_ref,
                    y_ref, so_ref, co_ref):
    nrow = x_ref.shape[0]
    x_all = x_ref[...]
    acc = cw_ref[DN_CONV - 1:DN_CONV, :] * x_all
    for j in range(DN_CONV - 1):
        acc = acc + cw_ref[j:j + 1, :] * conv_ref[:, j, :]
    qkv_all = jax.nn.silu(acc)
    g_all, beta_all = _dn_gates(zab_ref[...], alog_ref, dtb_ref)
    zg_all = zg_ref[...]
    row8 = lax.broadcasted_iota(jnp.int32, (SUBLANES, DN_HD), 0)
    rows = range(nrow)
    heads = range(DN_H)
    for i in rows:
        co_ref[i, 0:DN_CONV - 2, :] = conv_ref[i, 1:DN_CONV - 1, :]
        co_ref[i, DN_CONV - 2:DN_CONV - 1, :] = x_all[i:i + 1, :]

    qs = [_l2n(qkv_all[:, h * DN_HD:(h + 1) * DN_HD]) * (DN_HD ** -0.5) for h in heads]
    ks = [_l2n(qkv_all[:, BRANCH_W + h * DN_HD:BRANCH_W + (h + 1) * DN_HD]) for h in heads]
    vs = [qkv_all[:, 2 * BRANCH_W + h * DN_HD:2 * BRANCH_W + (h + 1) * DN_HD] for h in heads]
    egs = [jnp.exp(g_all[:, h:h + 1]) for h in heads]
    betas = [beta_all[:, DN_H + h:DN_H + h + 1] for h in heads]
    w_ks = [(beta * eg) * k for beta, eg, k in zip(betas, egs, ks)]
    q_gs = [q * eg for q, eg in zip(qs, egs)]
    qks = [jnp.sum(q.astype(BF16).astype(F32) * k.astype(BF16).astype(F32), axis=-1, keepdims=True)
           for q, k in zip(qs, ks)]
    bvs = [beta * v for beta, v in zip(betas, vs)]

    pairs = [(i, h) for i in rows for h in heads]
    lhs = [jnp.where(row8 == 0, jnp.broadcast_to(w_ks[h][i:i + 1, :], (SUBLANES, DN_HD)),
                     jnp.where(row8 == 1, jnp.broadcast_to(q_gs[h][i:i + 1, :], (SUBLANES, DN_HD)), 0.0)).astype(BF16)
           for i, h in pairs]
    prods = [_dot(l, s0_ref[i, h].astype(BF16)) for l, (i, h) in zip(lhs, pairs)]
    us = [bvs[h][i:i + 1, :] - p[0:1, :] for p, (i, h) in zip(prods, pairs)]
    outs = [p[1:2, :] + qks[h][i:i + 1, :] * u for p, u, (i, h) in zip(prods, us, pairs)]
    kcols = []
    for i in rows:
        k8 = jnp.zeros((SUBLANES, DN_HD), F32)
        for h in heads:
            k8 = jnp.where(row8 == h, jnp.broadcast_to(ks[h][i:i + 1, :], (SUBLANES, DN_HD)), k8)
        kcols.append(k8.T)
    for u, (i, h) in zip(us, pairs):
        so_ref[i, h] = egs[h][i:i + 1, :] * s0_ref[i, h] + kcols[i][:, h:h + 1] * u

    ys = []
    for h in heads:
        o_h = jnp.concatenate([outs[i * DN_H + h] for i in rows], axis=0)
        ys.append(_rms(o_h, ng_ref[...]) * jax.nn.silu(zg_all[:, h * DN_HD:(h + 1) * DN_HD]))
    y_ref[...] = jnp.concatenate(ys, axis=1).astype(y_ref.dtype)


def _dn_sample_call(z_dn, conv0, s0, cw, alog, dtb, ng):
    nbatch = z_dn.shape[0]
    rows = SAMPLE_ROWS
    const = lambda b: (0, 0)
    return pl.pallas_call(
        _dn_sample_body,
        out_shape=(_sds((nbatch, BRANCH_W), F32), _sds((nbatch, DN_H, DN_HD, DN_HD), F32),
                   _sds((nbatch, DN_CONV - 1, W_DN_QKV), F32)),
        grid=(nbatch // rows,),
        in_specs=[
            pl.BlockSpec((rows, W_DN_QKV), lambda b: (b, 0)),
            pl.BlockSpec((rows, BRANCH_W), lambda b: (b, W_DN_QKV // BRANCH_W)),
            pl.BlockSpec((rows, LANES), lambda b: (b, (W_DN_QKV + BRANCH_W) // LANES)),
            pl.BlockSpec((rows, DN_CONV - 1, W_DN_QKV), lambda b: (b, 0, 0)),
            pl.BlockSpec((rows, DN_H, DN_HD, DN_HD), lambda b: (b, 0, 0, 0)),
            pl.BlockSpec((DN_CONV, W_DN_QKV), const),
            pl.BlockSpec((1, LANES), const),
            pl.BlockSpec((1, LANES), const),
            pl.BlockSpec((1, DN_HD), const),
        ],
        out_specs=(
            pl.BlockSpec((rows, BRANCH_W), lambda b: (b, 0)),
            pl.BlockSpec((rows, DN_H, DN_HD, DN_HD), lambda b: (b, 0, 0, 0)),
            pl.BlockSpec((rows, DN_CONV - 1, W_DN_QKV), lambda b: (b, 0, 0)),
        ),
        compiler_params=_params("parallel"),
        name="dn_sample",
    )(z_dn, z_dn, z_dn, conv0, s0, cw, alog, dtb, ng)


def _rope_tables(pos):
    half = ATT_HD // 2
    inv = ROPE_THETA ** (-jnp.arange(half, dtype=F32) / half)
    ang = pos.astype(F32)[:, None] * inv[None, :]
    cos, sin = jnp.cos(ang), jnp.sin(ang)
    reps = LANES // ATT_HD
    return (jnp.tile(jnp.concatenate([cos, cos], axis=1), (1, reps)),
            jnp.tile(jnp.concatenate([-sin, sin], axis=1), (1, reps)))


def _ssm_matrices(lam_re, lam_im, log_dt, b_re, b_im, c_re, c_im):
    dt = jnp.exp(log_dt)[:, None]
    mag = jnp.exp(lam_re * dt)
    a_re, a_im = mag * jnp.cos(lam_im * dt), mag * jnp.sin(lam_im * dt)
    den = lam_re * lam_re + lam_im * lam_im
    f_re = ((a_re - 1.0) * lam_re + a_im * lam_im) / den
    f_im = (a_im * lam_re - (a_re - 1.0) * lam_im) / den
    bb_re = f_re[..., None] * b_re - f_im[..., None] * b_im
    bb_im = f_re[..., None] * b_im + f_im[..., None] * b_re
    gpb = SSM_NG // SSM_BLOCKS
    eye = jnp.eye(gpb, dtype=F32)

    def in_layout(b):
        b = b.reshape(SSM_BLOCKS, gpb, SSM_P, SSM_GROUP)
        return jnp.einsum("jipc,ik->jickp", b, eye).reshape(BRANCH_W, SSM_BLOCK_N)

    def out_layout(c):
        c = c.reshape(SSM_BLOCKS, gpb, SSM_GROUP, SSM_P)
        return jnp.einsum("jicp,ik->jipkc", c, eye).reshape(SSM_N, SSM_BLOCK_CH)

    bm = jnp.concatenate([in_layout(bb_re), in_layout(bb_im)], axis=1).astype(BF16)
    cm = jnp.concatenate([out_layout(c_re), out_layout(-c_im)], axis=0).astype(BF16)
    a = jnp.stack([a_re.reshape(SSM_N), a_im.reshape(SSM_N)], axis=0)
    return bm, cm, a


def _lane_rows(x):
    return jnp.pad(x, ((0, 0), (0, LANES - x.shape[1])))[:, None, :]


def _prepare(p):
    depth = p["w_in"].shape[0]
    w_in = p["w_in"]
    bm, cm, a = jax.vmap(_ssm_matrices)(p["ssm_lam_re"], p["ssm_lam_im"], p["ssm_log_dt"], p["ssm_b_re"],
                                        p["ssm_b_im"], p["ssm_c_re"], p["ssm_c_im"])
    w_t = jnp.swapaxes(w_in, 1, 2)
    w_all = jnp.concatenate([w_t[:, :N_MAIN, :], jnp.zeros((depth, GATE_COL0 - N_MAIN, D_MODEL), w_in.dtype),
                             w_t[:, N_MAIN:, :]], axis=1).astype(BF16)
    stacked = dict(
        w_all=w_all, w_branch=p["w_branch"].astype(BF16), w_out=p["w_out"].astype(BF16),
        w_ff1=p["w_ff1"].astype(BF16), w_ff2=p["w_ff2"].astype(BF16),
        bm=bm, cm=cm, glu_w=p["ssm_glu_w"].astype(BF16),
    )
    per_layer = dict(
        norm1_g=p["norm1_g"][:, None, :], norm2_g=p["norm2_g"][:, None, :],
        qg=jnp.tile(p["att_qn_g"], (1, ATT_HQ))[:, None, :],
        kg=jnp.tile(p["att_kn_g"], (1, ATT_HKV))[:, None, :],
        sink=p["att_sink"],
        sink_b=jnp.broadcast_to(p["att_sink"][:, :, None], (depth, ATT_HQ, LANES)),
        a=a, ssm_d=p["ssm_d"][:, None, :], glu_b=p["ssm_glu_b"][:, None, :],
        sg_g=p["sg_norm_g"][:, None, :], sg_w=p["sg_w"],
        sg_b=jnp.repeat(jnp.swapaxes(p["sg_b"], 1, 2), SG_GW, axis=2),
        sg_w00=jnp.repeat(p["sg_w"][:, :, 0, 0], SG_GW, axis=1)[:, None, :],
        cw=p["dn_conv_w"], alog=_lane_rows(p["dn_a_log"]), dtb=_lane_rows(p["dn_dt_bias"]),
        ng=p["dn_norm_g"][:, None, :],
    )
    return stacked, per_layer


def _mix_and_ffn(x, h, ys, wts, layer, lp, g_next, tm):
    mix = _merge_call(h, ys, wts["w_all"], wts["w_branch"], layer, tm)
    x, h2 = _proj_call(mix, wts["w_out"], layer, x, lp["norm2_g"], tm)
    return _ffn_call(h2, wts["w_ff1"], wts["w_ff2"], layer, x, g_next, tm, FFN_TF)


def _layer(xp, hp, xs, hs, st, wts, layer, lp, g_next, rope_p, rope_s, e, batch, seq):
    tmp = ROWS_PROMPT if xp.shape[0] % ROWS_PROMPT == 0 else xp.shape[0]
    tms = xs.shape[0]
    ck, cv, s0r, s0i, s0d, conv0 = st
    ssm_w = (wts["bm"], wts["cm"], lp["a"], lp["ssm_d"], wts["glu_w"], lp["glu_b"], layer)

    z_att, z_ssm, z_sg, z_dn = _in_proj_call(hp, wts["w_all"], layer, tmp)
    ys, pk, pv = _attn_prompt_call(z_att, rope_p[0], rope_p[1], lp["qg"], lp["kg"], e, lp["sink"], batch, seq)
    tb = 512 if seq % 512 == 0 else seq
    ys, pre, pim = _ssm_prompt_call(z_ssm, *ssm_w, ys, batch, seq, tb)
    ys, pchunk = _sg_prompt_call(z_sg, lp["sg_g"], lp["sg_w"], lp["sg_b"], ys, batch, seq)
    ys, pdelta, pconv = _dn_prompt_call(z_dn, lp["cw"], lp["alog"], lp["dtb"], lp["ng"], ys, batch, seq)
    xp, hp = _mix_and_ffn(xp, hp, ys, wts, layer, lp, g_next, tmp)

    z_att, z_ssm, z_sg, z_dn = _in_proj_call(hs, wts["w_all"], layer, tms)
    ya, sk, sv = _attn_sample_call(z_att, ck, cv, rope_s[0], rope_s[1], lp["qg"], lp["kg"], e, lp["sink_b"])
    yb, sre, sim = _ssm_sample_call(z_ssm, s0r, s0i, *ssm_w)
    yc, schunk = _sg_sample_call(z_sg, lp["sg_g"], lp["sg_w00"], lp["sg_b"])
    yd, sdelta, sconv = _dn_sample_call(z_dn, conv0, s0d, lp["cw"], lp["alog"], lp["dtb"], lp["ng"])
    xs, hs = _mix_and_ffn(xs, hs, jnp.stack([ya, yb, yc, yd]).astype(BF16), wts, layer, lp, g_next, tms)

    new_p = (pk, pv, pre, pim, pdelta, pconv, pchunk)
    new_s = (sk, sv, sre, sim, sdelta, sconv, schunk)
    return xp, hp, xs, hs, new_p, new_s


def kernel(x_prompt, x_sample, cache_k, cache_v, state_ssm_re, state_ssm_im, state_delta, state_conv, norm1_g, w_in, att_qn_g, att_kn_g, att_sink, ssm_lam_re, ssm_lam_im, ssm_log_dt, ssm_b_re, ssm_b_im, ssm_c_re, ssm_c_im, ssm_d, ssm_glu_w, ssm_glu_b, sg_norm_g, sg_w, sg_b, dn_conv_w, dn_a_log, dn_dt_bias, dn_norm_g, w_branch, w_out, norm2_g, w_ff1, w_ff2):
    p = dict(norm1_g=norm1_g, w_in=w_in, att_qn_g=att_qn_g, att_kn_g=att_kn_g, att_sink=att_sink,
             ssm_lam_re=ssm_lam_re, ssm_lam_im=ssm_lam_im, ssm_log_dt=ssm_log_dt, ssm_b_re=ssm_b_re,
             ssm_b_im=ssm_b_im, ssm_c_re=ssm_c_re, ssm_c_im=ssm_c_im, ssm_d=ssm_d, ssm_glu_w=ssm_glu_w,
             ssm_glu_b=ssm_glu_b, sg_norm_g=sg_norm_g, sg_w=sg_w, sg_b=sg_b, dn_conv_w=dn_conv_w, dn_a_log=dn_a_log,
             dn_dt_bias=dn_dt_bias, dn_norm_g=dn_norm_g, w_branch=w_branch, w_out=w_out, norm2_g=norm2_g,
             w_ff1=w_ff1, w_ff2=w_ff2)
    depth = w_in.shape[0]
    batch, seq, d = x_prompt.shape
    nsamp = x_sample.shape[0]
    past = cache_k.shape[2]
    assert x_sample.shape[1] == 1 and past == WINDOW and seq % CHUNK == 0 and d == D_MODEL
    assert nsamp % SAMPLE_ROWS == 0
    nkv = ATT_HKV * ATT_HD

    xp = x_prompt.reshape(batch * seq, d)
    xs = x_sample.reshape(nsamp, d)
    rope_p = _rope_tables(jnp.arange(seq))
    rope_s = _rope_tables(PAST_LEN + jnp.arange(1))
    lane = jnp.arange(BRANCH_W) // ATT_HD
    e = (lane[:, None] == lane[None, :]).astype(BF16)

    wts, prm = _prepare(p)
    ck_all = cache_k.reshape(depth, nsamp, past, nkv)
    cv_all = cache_v.reshape(depth, nsamp, past, nkv)
    s0r_all = state_ssm_re.reshape(depth, nsamp, SSM_N)
    s0i_all = state_ssm_im.reshape(depth, nsamp, SSM_N)

    hp = _rmsnorm_call(xp, prm["norm1_g"][0], ROWS_PROMPT if xp.shape[0] % ROWS_PROMPT == 0 else xp.shape[0])
    hs = _rmsnorm_call(xs, prm["norm1_g"][0], nsamp)
    new_p = [[] for _ in range(7)]
    new_s = [[] for _ in range(7)]
    for l in range(depth):
        lp = {name: v[l] for name, v in prm.items()}
        st = (ck_all[l], cv_all[l], s0r_all[l], s0i_all[l], state_delta[l], state_conv[l])
        g_next = prm["norm1_g"][l + 1] if l + 1 < depth else jnp.ones((1, d), F32)
        xp, hp, xs, hs, sp, ss = _layer(xp, hp, xs, hs, st, wts, l, lp, g_next, rope_p, rope_s, e, batch, seq)
        for i in range(7):
            new_p[i].append(sp[i])
            new_s[i].append(ss[i])

    def stack(parts, shape):
        return jnp.stack(parts, axis=0).reshape((depth,) + shape)

    keep = min(WINDOW, seq)
    return (
        xp.reshape(batch, seq, d), xs.reshape(nsamp, 1, d),
        stack(new_p[0], (batch, keep, ATT_HKV, ATT_HD)), stack(new_p[1], (batch, keep, ATT_HKV, ATT_HD)),
        stack(new_p[2], (batch, SSM_NG, SSM_P)), stack(new_p[3], (batch, SSM_NG, SSM_P)),
        stack(new_p[4], (batch, DN_H, DN_HD, DN_HD)), stack(new_p[5], (batch, DN_CONV - 1, W_DN_QKV)),
        stack(new_p[6], (batch, CHUNK, BRANCH_W)),
        stack(new_s[0], (nsamp, past, ATT_HKV, ATT_HD)), stack(new_s[1], (nsamp, past, ATT_HKV, ATT_HD)),
        stack(new_s[2], (nsamp, SSM_NG, SSM_P)), stack(new_s[3], (nsamp, SSM_NG, SSM_P)),
        stack(new_s[4], (nsamp, DN_H, DN_HD, DN_HD)), stack(new_s[5], (nsamp, DN_CONV - 1, W_DN_QKV)),
        stack(new_s[6], (nsamp, 1, BRANCH_W)),
    )
```

```python
import jax
import jax.numpy as jnp
from jax import lax
from jax.experimental import pallas as pl
from jax.experimental.pallas import tpu as pltpu

F32 = jnp.float32
BF16 = jnp.bfloat16

D_MODEL = 2048
PAST_LEN = 16384
N_BRANCH = 4
BRANCH_W = D_MODEL // N_BRANCH
ATT_HD = 64
ATT_HQ = BRANCH_W // ATT_HD
ATT_HKV = 2
ATT_G = ATT_HQ // ATT_HKV
WINDOW = 128
ROPE_THETA = 10000.0
SSM_GROUP = 16
SSM_NG = BRANCH_W // SSM_GROUP
SSM_P = 64
SSM_N = SSM_NG * SSM_P
SSM_SEGS = 8
CHUNK = 128
SG_GROUPS = 4
SG_GW = BRANCH_W // SG_GROUPS
SG_STEP_CHUNKS = 4
DN_HD = 128
DN_H = BRANCH_W // DN_HD
DN_CONV = 4
DN_CHUNK = 64
DN_STEP_CHUNKS = 2
D_FF = 4 * D_MODEL
EPS = 1e-6
NEG_INF = -1e30

LANES = 128
SUBLANES = 8
V7X_VMEM_LIMIT = 56 * 1024 * 1024

W_ATT = ATT_HQ * ATT_HD + 2 * ATT_HKV * ATT_HD
W_SSM = BRANCH_W
W_SG = 2 * BRANCH_W
W_DN_QKV = 3 * BRANCH_W
W_DN = W_DN_QKV + BRANCH_W + LANES
N_MAIN = W_ATT + W_SSM + W_SG + W_DN_QKV + BRANCH_W + 2 * DN_H
W_MAIN = W_ATT + W_SSM + W_SG + W_DN
GATE_COL0 = 3 * D_MODEL

SSM_BLOCK_CH = LANES
SSM_BLOCKS = BRANCH_W // SSM_BLOCK_CH
SSM_BLOCK_N = SSM_N // SSM_BLOCKS

ROWS_PROMPT = 512
FFN_TF = 1024
SAMPLE_ROWS = 8


def _params(*sem):
    return pltpu.CompilerParams(dimension_semantics=sem, vmem_limit_bytes=V7X_VMEM_LIMIT)


def _sds(shape, dtype):
    return jax.ShapeDtypeStruct(shape, dtype)


def _dot(a, b, precision=None):
    return jnp.dot(a, b, preferred_element_type=F32, precision=precision)


def _dot_nt(a, b, precision=None):
    return lax.dot_general(a, b, (((1,), (1,)), ((), ())), preferred_element_type=F32, precision=precision)


def _rms(x, g):
    ms = jnp.mean(x * x, axis=-1, keepdims=True)
    return x * lax.rsqrt(ms + EPS) * g


def _split2(x):
    hi = x.astype(BF16)
    return hi, (x - hi.astype(F32)).astype(BF16)


def _split_dot(x, e):
    hi, lo = _split2(x)
    return _dot(hi, e) + _dot(lo, e)


def _rmsnorm_body(x_ref, g_ref, o_ref):
    o_ref[...] = _rms(x_ref[...], g_ref[...]).astype(o_ref.dtype)


def _rmsnorm_call(x, g, tm):
    m, d = x.shape
    return pl.pallas_call(
        _rmsnorm_body,
        out_shape=_sds((m, d), BF16),
        grid=(m // tm,),
        in_specs=[pl.BlockSpec((tm, d), lambda i: (i, 0)), pl.BlockSpec((1, d), lambda i: (0, 0))],
        out_specs=pl.BlockSpec((tm, d), lambda i: (i, 0)),
        compiler_params=_params("parallel"),
        name="rmsnorm",
    )(x, g)


def _in_proj_body(h_ref, w_ref, *out_refs):
    h = h_ref[...]
    off = 0
    for o_ref in out_refs:
        width = o_ref.shape[1]
        o_ref[...] = _dot_nt(h, w_ref[off:off + width, :])
        off += width


def _in_proj_call(h, w_all, layer, tm):
    m, k = h.shape
    widths = (W_ATT, W_SSM, W_SG, W_DN)
    row = lambda i: (i, 0)
    return pl.pallas_call(
        _in_proj_body,
        out_shape=tuple(_sds((m, n), F32) for n in widths),
        grid=(m // tm,),
        in_specs=[pl.BlockSpec((tm, k), row),
                  pl.BlockSpec((None, W_MAIN, k), lambda i: (layer, 0, 0), pipeline_mode=pl.Buffered(1))],
        out_specs=tuple(pl.BlockSpec((tm, n), row) for n in widths),
        compiler_params=_params("parallel"),
        name="in_proj",
    )(h, w_all)


def _merge_body(h_ref, y_ref, wg_ref, wb_ref, o_ref, acc_ref):
    n = pl.program_id(1)

    @pl.when(n == 0)
    def _():
        acc_ref[...] = jnp.zeros_like(acc_ref)

    gate = jax.nn.sigmoid(_dot_nt(h_ref[...], wg_ref[...]))
    acc_ref[...] += gate * _dot(y_ref[...], wb_ref[...])

    @pl.when(n == N_BRANCH - 1)
    def _():
        o_ref[...] = acc_ref[...].astype(o_ref.dtype)


def _merge_call(h, ys, w_all, wb, layer, tm):
    m, d = h.shape
    gate0 = GATE_COL0 // d
    return pl.pallas_call(
        _merge_body,
        out_shape=_sds((m, d), BF16),
        grid=(m // tm, N_BRANCH),
        in_specs=[
            pl.BlockSpec((tm, d), lambda i, n: (i, 0)),
            pl.BlockSpec((None, tm, BRANCH_W), lambda i, n: (n, i, 0)),
            pl.BlockSpec((None, d, d), lambda i, n: (layer, gate0 + n, 0)),
            pl.BlockSpec((None, None, BRANCH_W, d), lambda i, n: (layer, n, 0, 0)),
        ],
        out_specs=pl.BlockSpec((tm, d), lambda i, n: (i, 0)),
        scratch_shapes=[pltpu.VMEM((tm, d), F32)],
        compiler_params=_params("parallel", "arbitrary"),
        name="merge",
    )(h, ys, w_all, wb)


def _proj_body(m_ref, w_ref, x_ref, g_ref, xo_ref, ho_ref):
    xn = x_ref[...] + _dot(m_ref[...], w_ref[...])
    xo_ref[...] = xn
    ho_ref[...] = _rms(xn, g_ref[...]).astype(ho_ref.dtype)


def _proj_call(mix, w, layer, x, g, tm):
    m, d = x.shape
    return pl.pallas_call(
        _proj_body,
        out_shape=(_sds((m, d), F32), _sds((m, d), BF16)),
        grid=(m // tm,),
        in_specs=[
            pl.BlockSpec((tm, d), lambda i: (i, 0)),
            pl.BlockSpec((None, d, d), lambda i: (layer, 0, 0)),
            pl.BlockSpec((tm, d), lambda i: (i, 0)),
            pl.BlockSpec((1, d), lambda i: (0, 0)),
        ],
        out_specs=(pl.BlockSpec((tm, d), lambda i: (i, 0)), pl.BlockSpec((tm, d), lambda i: (i, 0))),
        compiler_params=_params("parallel"),
        name="out_proj",
    )(mix, w, x, g)


def _ffn_body(h_ref, w1_ref, w2_ref, x_ref, g_ref, xo_ref, ho_ref):
    f = pl.program_id(1)

    @pl.when(f == 0)
    def _():
        xo_ref[...] = x_ref[...]

    a = _dot(h_ref[...], w1_ref[...])
    a = jnp.square(jnp.maximum(a, 0.0)).astype(BF16)
    xo_ref[...] += _dot(a, w2_ref[...])

    @pl.when(f == pl.num_programs(1) - 1)
    def _():
        ho_ref[...] = _rms(xo_ref[...], g_ref[...]).astype(ho_ref.dtype)


def _ffn_call(h, w1, w2, layer, x, g_next, tm, tf):
    m, d = x.shape
    dff = w1.shape[2]
    return pl.pallas_call(
        _ffn_body,
        out_shape=(_sds((m, d), F32), _sds((m, d), BF16)),
        grid=(m // tm, dff // tf),
        in_specs=[
            pl.BlockSpec((tm, d), lambda i, f: (i, 0)),
            pl.BlockSpec((None, d, tf), lambda i, f: (layer, 0, f)),
            pl.BlockSpec((None, tf, d), lambda i, f: (layer, f, 0)),
            pl.BlockSpec((tm, d), lambda i, f: (i, 0)),
            pl.BlockSpec((1, d), lambda i, f: (0, 0)),
        ],
        out_specs=(pl.BlockSpec((tm, d), lambda i, f: (i, 0)), pl.BlockSpec((tm, d), lambda i, f: (i, 0))),
        compiler_params=_params("parallel", "arbitrary"),
        name="ffn",
    )(h, w1, w2, x, g_next)


def _head_norm_rope(x, g, cos, sin, e):
    w = x.shape[1]
    ss = _split_dot(x * x, e)
    xn = x * lax.rsqrt(ss * (1.0 / ATT_HD) + EPS) * g
    lane = lax.broadcasted_iota(jnp.int32, x.shape, 1)
    first_half = (lane % ATT_HD) < (ATT_HD // 2)
    rot = jnp.where(first_half, pltpu.roll(xn, w - ATT_HD // 2, 1), pltpu.roll(xn, ATT_HD // 2, 1))
    return xn * cos + rot * sin


def _qkv_norm_rope(z, cos, sin, qg_ref, kg_ref, e_ref):
    nq = ATT_HQ * ATT_HD
    nk = ATT_HKV * ATT_HD
    rep = nq // LANES
    q = _head_norm_rope(z[:, :nq], qg_ref[...], jnp.tile(cos, (1, rep)), jnp.tile(sin, (1, rep)), e_ref[...])
    k = _head_norm_rope(z[:, nq:nq + nk], kg_ref[...], cos, sin, e_ref[:nk, :nk])
    return q, k, z[:, nq + nk:]


def _half_placed(x):
    lane = lax.broadcasted_iota(jnp.int32, x.shape, 1)
    lo = lane < ATT_HD
    xr = pltpu.roll(x, ATT_HD, 1)
    zero = jnp.zeros_like(x)
    return ((jnp.where(lo, x, zero), jnp.where(lo, zero, xr)),
            (jnp.where(lo, xr, zero), jnp.where(lo, zero, x)))


def _attn_prompt_body(z_ref, cos_ref, sin_ref, qg_ref, kg_ref, e_ref, sink_ref,
                      y_ref, ko_ref, vo_ref, kprev_ref, vprev_ref):
    i = pl.program_id(1)

    @pl.when(i == 0)
    def _():
        kprev_ref[...] = jnp.zeros_like(kprev_ref)
        vprev_ref[...] = jnp.zeros_like(vprev_ref)

    q, k, v = _qkv_norm_rope(z_ref[...], cos_ref[...], sin_ref[...], qg_ref, kg_ref, e_ref)
    ko_ref[0] = k
    vo_ref[0] = v
    kcat = jnp.concatenate([kprev_ref[...], k], axis=0)
    vcat = jnp.concatenate([vprev_ref[...], v], axis=0)
    kprev_ref[...] = k
    vprev_ref[...] = v
    kparts = _half_placed(kcat)
    vparts = _half_placed(vcat)
    qb = q.astype(BF16)

    rows = 2 * WINDOW
    r = lax.broadcasted_iota(jnp.int32, (rows, rows), 0) % WINDOW
    c = lax.broadcasted_iota(jnp.int32, (rows, rows), 1)
    rel = r - c + WINDOW
    mask = (rel >= 0) & (rel < WINDOW) & ((c >= WINDOW) | (i > 0))
    top = lax.broadcasted_iota(jnp.int32, (rows, 1), 0) < WINDOW

    qst = [jnp.concatenate([qb[:, LANES * (2 * kv):LANES * (2 * kv + 1)],
                            qb[:, LANES * (2 * kv + 1):LANES * (2 * kv + 2)]], axis=0) for kv in range(ATT_HKV)]
    combos = [(kv, parity) for kv in range(ATT_HKV) for parity in range(2)]
    scores = [_dot_nt(qst[kv], kparts[kv][parity].astype(BF16)) * (ATT_HD ** -0.5) for kv, parity in combos]
    scores = [jnp.where(mask, s, NEG_INF) for s in scores]
    sinks = [jnp.where(top, sink_ref[ATT_G * kv + parity], sink_ref[ATT_G * kv + 2 + parity]) for kv, parity in combos]
    maxes = [jnp.maximum(jnp.max(s, axis=-1, keepdims=True), sink) for s, sink in zip(scores, sinks)]
    exps = [jnp.exp(s - mx) for s, mx in zip(scores, maxes)]
    dens = [jnp.sum(ex, axis=-1, keepdims=True) + jnp.exp(sink - mx) for ex, sink, mx in zip(exps, sinks, maxes)]
    probs = [(ex / den).astype(BF16) for ex, den in zip(exps, dens)]
    outs = [_dot(p, vparts[kv][parity].astype(BF16)) for p, (kv, parity) in zip(probs, combos)]
    pair_out = []
    for kv in range(ATT_HKV):
        acc = outs[2 * kv] + outs[2 * kv + 1]
        pair_out += [acc[:WINDOW], acc[WINDOW:]]
    y_ref[...] = jnp.concatenate(pair_out, axis=1).astype(y_ref.dtype)


def _attn_prompt_call(z_att, cos, sin, qg, kg, e, sink, batch, seq):
    nb = seq // WINDOW
    nkv = ATT_HKV * ATT_HD
    return pl.pallas_call(
        _attn_prompt_body,
        out_shape=(_sds((N_BRANCH, batch * seq, BRANCH_W), BF16), _sds((batch, WINDOW, nkv), F32),
                   _sds((batch, WINDOW, nkv), F32)),
        grid=(batch, nb),
        in_specs=[
            pl.BlockSpec((WINDOW, W_ATT), lambda b, i: (b * nb + i, 0)),
            pl.BlockSpec((WINDOW, LANES), lambda b, i: (i, 0)),
            pl.BlockSpec((WINDOW, LANES), lambda b, i: (i, 0)),
            pl.BlockSpec((1, BRANCH_W), lambda b, i: (0, 0)),
            pl.BlockSpec((1, nkv), lambda b, i: (0, 0)),
            pl.BlockSpec((BRANCH_W, BRANCH_W), lambda b, i: (0, 0)),
            pl.BlockSpec(memory_space=pltpu.SMEM),
        ],
        out_specs=(
            pl.BlockSpec((None, WINDOW, BRANCH_W), lambda b, i: (0, b * nb + i, 0)),
            pl.BlockSpec((1, WINDOW, nkv), lambda b, i: (b, 0, 0)),
            pl.BlockSpec((1, WINDOW, nkv), lambda b, i: (b, 0, 0)),
        ),
        scratch_shapes=[pltpu.VMEM((WINDOW, nkv), F32), pltpu.VMEM((WINDOW, nkv), F32)],
        compiler_params=_params("arbitrary", "arbitrary"),
        name="attn_prompt",
    )(z_att, cos, sin, qg, kg, e, sink)


def _attn_sample_body(z_ref, ck_ref, cv_ref, cos_ref, sin_ref, qg_ref, kg_ref, e_ref, sink_ref,
                      y_ref, ko_ref, vo_ref):
    nrow = z_ref.shape[0]
    q_all, k_all, v_all = _qkv_norm_rope(z_ref[...], cos_ref[...], sin_ref[...], qg_ref, kg_ref, e_ref)
    lane = lax.broadcasted_iota(jnp.int32, (1, LANES), 1)
    lo = lane < ATT_HD
    row = lax.broadcasted_iota(jnp.int32, (ATT_HQ, LANES), 0)
    scale = ATT_HD ** -0.5
    sink = sink_ref[:, 0:1]
    rows = range(nrow)
    ks = [k_all[i:i + 1, :] for i in rows]
    vs = [v_all[i:i + 1, :] for i in rows]
    for i in rows:
        ko_ref[i, 0:WINDOW - 1, :] = ck_ref[i, 1:WINDOW, :]
        ko_ref[i, WINDOW - 1:WINDOW, :] = ks[i]
        vo_ref[i, 0:WINDOW - 1, :] = cv_ref[i, 1:WINDOW, :]
        vo_ref[i, WINDOW - 1:WINDOW, :] = vs[i]

    qexps = []
    for i in rows:
        qexp = jnp.zeros((ATT_HQ, LANES), F32)
        for h in range(ATT_HQ):
            pair = q_all[i:i + 1, LANES * (h // 2):LANES * (h // 2 + 1)]
            want_lo = (h // ATT_G) == 0
            have_lo = (h % 2) == 0
            src = pair if want_lo == have_lo else pltpu.roll(pair, ATT_HD, 1)
            placed = jnp.where(lo if want_lo else jnp.logical_not(lo), src, 0.0)
            qexp = jnp.where(row == h, jnp.broadcast_to(placed, (ATT_HQ, LANES)), qexp)
        qexps.append(qexp)
    qbs = [qexp.astype(BF16) for qexp in qexps]
    key = lax.broadcasted_iota(jnp.int32, (ATT_HQ, WINDOW), 1)
    scores = [jnp.where(key >= 1, _dot_nt(qb, ck_ref[i].astype(BF16)) * scale, NEG_INF) for i, qb in zip(rows, qbs)]
    s_news = [jnp.sum(qb.astype(F32) * k.astype(BF16).astype(F32), axis=-1, keepdims=True) * scale
              for qb, k in zip(qbs, ks)]
    maxes = [jnp.maximum(jnp.maximum(jnp.max(s, axis=-1, keepdims=True), sn), sink) for s, sn in zip(scores, s_news)]
    exps = [jnp.exp(s - mx) for s, mx in zip(scores, maxes)]
    ex_news = [jnp.exp(sn - mx) for sn, mx in zip(s_news, maxes)]
    dens = [jnp.sum(ex, axis=-1, keepdims=True) + en + jnp.exp(sink - mx) for ex, en, mx in zip(exps, ex_news, maxes)]
    probs = [(ex / den).astype(BF16) for ex, den in zip(exps, dens)]
    p_news = [(en / den).astype(BF16).astype(F32) for en, den in zip(ex_news, dens)]
    outs = [_dot(p, cv_ref[i].astype(BF16)) + pn * v.astype(BF16).astype(F32)
            for i, p, pn, v in zip(rows, probs, p_news, vs)]

    ys = []
    for o in outs:
        pairs = []
        for j in range(ATT_HQ // 2):
            parts = []
            for par in range(2):
                h = 2 * j + par
                orow = o[h:h + 1, :]
                have_lo = (h // ATT_G) == 0
                want_lo = par == 0
                src = orow if want_lo == have_lo else pltpu.roll(orow, ATT_HD, 1)
                parts.append(jnp.where(lo if want_lo else jnp.logical_not(lo), src, 0.0))
            pairs.append(parts[0] + parts[1])
        ys.append(jnp.concatenate(pairs, axis=1))
    y_ref[...] = jnp.concatenate(ys, axis=0).astype(y_ref.dtype)


def _attn_sample_call(z_att, ck, cv, cos, sin, qg, kg, e, sink_b):
    nbatch = z_att.shape[0]
    nkv = ATT_HKV * ATT_HD
    rows = SAMPLE_ROWS
    const = lambda b: (0, 0)
    return pl.pallas_call(
        _attn_sample_body,
        out_shape=(_sds((nbatch, BRANCH_W), F32), _sds((nbatch, WINDOW, nkv), F32),
                   _sds((nbatch, WINDOW, nkv), F32)),
        grid=(nbatch // rows,),
        in_specs=[
            pl.BlockSpec((rows, W_ATT), lambda b: (b, 0)),
            pl.BlockSpec((rows, WINDOW, nkv), lambda b: (b, 0, 0)),
            pl.BlockSpec((rows, WINDOW, nkv), lambda b: (b, 0, 0)),
            pl.BlockSpec((1, LANES), const),
            pl.BlockSpec((1, LANES), const),
            pl.BlockSpec((1, BRANCH_W), const),
            pl.BlockSpec((1, nkv), const),
            pl.BlockSpec((BRANCH_W, BRANCH_W), const),
            pl.BlockSpec((ATT_HQ, LANES), const),
        ],
        out_specs=(
            pl.BlockSpec((rows, BRANCH_W), lambda b: (b, 0)),
            pl.BlockSpec((rows, WINDOW, nkv), lambda b: (b, 0, 0)),
            pl.BlockSpec((rows, WINDOW, nkv), lambda b: (b, 0, 0)),
        ),
        compiler_params=_params("parallel"),
        name="attn_sample",
    )(z_att, ck, cv, cos, sin, qg, kg, e, sink_b)


def _cmul(ar, ai, br, bi):
    return ar * br - ai * bi, ar * bi + ai * br


def _ssm_in(ub, bm_ref, j):
    blk = slice(j * SSM_BLOCK_CH, (j + 1) * SSM_BLOCK_CH)
    return _dot(ub[:, blk], bm_ref[blk, :])


def _ssm_out(s_re, s_im, cm_ref, j):
    return (_dot(s_re.astype(BF16), cm_ref[j * SSM_BLOCK_N:(j + 1) * SSM_BLOCK_N, :])
            + _dot(s_im.astype(BF16), cm_ref[SSM_N + j * SSM_BLOCK_N:SSM_N + (j + 1) * SSM_BLOCK_N, :]))


def _ssm_tail(y, u, d_ref, gw_ref, gb_ref):
    y = jax.nn.gelu(y + d_ref[...] * u)
    return y * jax.nn.sigmoid(_dot(y.astype(BF16), gw_ref[...]) + gb_ref[...])


def _ssm_prompt_body(u_ref, perm_ref, unperm_ref, bm_ref, cm_ref, a_ref, d_ref, gw_ref, gb_ref, ys_in_ref,
                     y_ref, sre_ref, sim_ref,
                     bu_ref, start_ref, carry_ref):
    del ys_in_ref
    tb = u_ref.shape[0]
    ts = tb // SSM_SEGS
    step_j = pl.program_id(1)

    @pl.when(step_j == 0)
    def _():
        carry_ref[...] = jnp.zeros_like(carry_ref)

    u = u_ref[...]
    u_hi, u_mid = _split2(u)
    u_lo = (u - u_hi.astype(F32) - u_mid.astype(F32)).astype(BF16)
    perm = perm_ref[...]
    up = _dot(perm, u_hi) + (_dot(perm, u_mid) + _dot(perm, u_lo))
    ub = up.astype(BF16)
    cw = SSM_BLOCK_N
    for j in range(SSM_BLOCKS):
        bu = _ssm_in(ub, bm_ref, j)
        bu_ref[:, j * cw:(j + 1) * cw] = bu[:, :cw]
        bu_ref[:, SSM_N + j * cw:SSM_N + (j + 1) * cw] = bu[:, cw:]

    def scan(c0, init_re, init_im, store):
        ar = jnp.broadcast_to(a_ref[0:1, c0:c0 + cw], (SUBLANES, cw))
        ai = jnp.broadcast_to(a_ref[1:2, c0:c0 + cw], (SUBLANES, cw))

        def step(t, st):
            sr, si = st
            r0 = pl.multiple_of(t * SUBLANES, SUBLANES)
            br = bu_ref[pl.ds(r0, SUBLANES), c0:c0 + cw]
            bi = bu_ref[pl.ds(r0, SUBLANES), SSM_N + c0:SSM_N + c0 + cw]
            pr, pi = _cmul(ar, ai, sr, si)
            sr, si = pr + br, pi + bi
            if store:
                bu_ref[pl.ds(r0, SUBLANES), c0:c0 + cw] = sr
                bu_ref[pl.ds(r0, SUBLANES), SSM_N + c0:SSM_N + c0 + cw] = si
            return sr, si

        return lax.fori_loop(0, ts, step, (init_re, init_im))

    zero = jnp.zeros((SUBLANES, cw), F32)
    ys = []
    for j in range(SSM_BLOCKS):
        c0 = j * cw
        er, ei = scan(c0, zero, zero, False)
        pr, pi = a_ref[0:1, c0:c0 + cw], a_ref[1:2, c0:c0 + cw]
        n = 1
        while n < ts:
            pr, pi = _cmul(pr, pi, pr, pi)
            n *= 2
        sr = carry_ref[0:1, c0:c0 + cw]
        si = carry_ref[1:2, c0:c0 + cw]
        for s in range(SSM_SEGS):
            start_ref[s:s + 1, c0:c0 + cw] = sr
            start_ref[s:s + 1, SSM_N + c0:SSM_N + c0 + cw] = si
            qr, qi = _cmul(pr, pi, sr, si)
            sr, si = qr + er[s:s + 1, :], qi + ei[s:s + 1, :]
        carry_ref[0:1, c0:c0 + cw] = sr
        carry_ref[1:2, c0:c0 + cw] = si
        scan(c0, start_ref[:, c0:c0 + cw], start_ref[:, SSM_N + c0:SSM_N + c0 + cw], True)
        ys.append(_ssm_out(bu_ref[:, c0:c0 + cw], bu_ref[:, SSM_N + c0:SSM_N + c0 + cw], cm_ref, j))

    sre_ref[0] = carry_ref[0:1, :]
    sim_ref[0] = carry_ref[1:2, :]
    y = _ssm_tail(jnp.concatenate(ys, axis=1), up, d_ref, gw_ref, gb_ref).astype(BF16)
    y_ref[...] = _dot(unperm_ref[...], y).astype(y_ref.dtype)


def _ssm_prompt_call(z_u, bm, cm, a, d, gw, gb, layer, ys, batch, seq, tb):
    nblk = seq // tb
    const = lambda b, j: (0, 0)
    of_layer = lambda b, j: (layer, 0, 0)
    ts = tb // SSM_SEGS
    src = (jnp.arange(tb) % SSM_SEGS) * ts + jnp.arange(tb) // SSM_SEGS
    perm = (src[:, None] == jnp.arange(tb)[None, :]).astype(BF16)
    return pl.pallas_call(
        _ssm_prompt_body,
        out_shape=(_sds(ys.shape, ys.dtype), _sds((batch, 1, SSM_N), F32), _sds((batch, 1, SSM_N), F32)),
        grid=(batch, nblk),
        in_specs=[
            pl.BlockSpec((tb, BRANCH_W), lambda b, j: (b * nblk + j, 0)),
            pl.BlockSpec((tb, tb), const),
            pl.BlockSpec((tb, tb), const),
            pl.BlockSpec((None, BRANCH_W, 2 * SSM_BLOCK_N), of_layer),
            pl.BlockSpec((None, 2 * SSM_N, SSM_BLOCK_CH), of_layer),
            pl.BlockSpec((2, SSM_N), const),
            pl.BlockSpec((1, BRANCH_W), const),
            pl.BlockSpec((None, BRANCH_W, BRANCH_W), of_layer),
            pl.BlockSpec((1, BRANCH_W), const),
            pl.BlockSpec(memory_space=pl.ANY),
        ],
        out_specs=(
            pl.BlockSpec((None, tb, BRANCH_W), lambda b, j: (1, b * nblk + j, 0)),
            pl.BlockSpec((1, 1, SSM_N), lambda b, j: (b, 0, 0)),
            pl.BlockSpec((1, 1, SSM_N), lambda b, j: (b, 0, 0)),
        ),
        input_output_aliases={9: 0},
        scratch_shapes=[
            pltpu.VMEM((tb, 2 * SSM_N), F32),
            pltpu.VMEM((SSM_SEGS, 2 * SSM_N), F32),
            pltpu.VMEM((2, SSM_N), F32),
        ],
        compiler_params=_params("arbitrary", "arbitrary"),
        name="ssm_prompt",
    )(z_u, perm, perm.T, bm, cm, a, d, gw, gb, ys)


def _ssm_sample_body(u_ref, s0r_ref, s0i_ref, bm_ref, cm_ref, a_ref, d_ref, gw_ref, gb_ref,
                     y_ref, sre_ref, sim_ref):
    u = u_ref[...]
    ub = u.astype(BF16)
    cw = SSM_BLOCK_N
    ys = []
    for j in range(SSM_BLOCKS):
        cols = slice(j * cw, (j + 1) * cw)
        bu = _ssm_in(ub, bm_ref, j)
        pr, pi = _cmul(a_ref[0:1, cols], a_ref[1:2, cols], s0r_ref[:, cols], s0i_ref[:, cols])
        sr = pr + bu[:, :cw]
        si = pi + bu[:, cw:]
        sre_ref[:, cols] = sr
        sim_ref[:, cols] = si
        ys.append(_ssm_out(sr, si, cm_ref, j))
    y_ref[...] = _ssm_tail(jnp.concatenate(ys, axis=1), u, d_ref, gw_ref, gb_ref).astype(y_ref.dtype)


def _ssm_sample_call(z_u, s0r, s0i, bm, cm, a, d, gw, gb, layer):
    nbatch = z_u.shape[0]
    const = lambda i: (0, 0)
    of_layer = lambda i: (layer, 0, 0)
    rows_w = pl.BlockSpec((nbatch, BRANCH_W), const)
    rows_n = pl.BlockSpec((nbatch, SSM_N), const)
    return pl.pallas_call(
        _ssm_sample_body,
        out_shape=(_sds((nbatch, BRANCH_W), F32), _sds((nbatch, SSM_N), F32), _sds((nbatch, SSM_N), F32)),
        grid=(1,),
        in_specs=[
            rows_w, rows_n, rows_n,
            pl.BlockSpec((None, BRANCH_W, 2 * SSM_BLOCK_N), of_layer),
            pl.BlockSpec((None, 2 * SSM_N, SSM_BLOCK_CH), of_layer),
            pl.BlockSpec((2, SSM_N), const),
            pl.BlockSpec((1, BRANCH_W), const),
            pl.BlockSpec((None, BRANCH_W, BRANCH_W), of_layer),
            pl.BlockSpec((1, BRANCH_W), const),
        ],
        out_specs=(rows_w, rows_n, rows_n),
        compiler_params=_params("arbitrary"),
        name="ssm_sample",
    )(z_u, s0r, s0i, bm, cm, a, d, gw, gb)


def _sg_prompt_body(z_ref, g_ref, w_ref, b_ref, ys_in_ref, y_ref, v_ref):
    del ys_in_ref
    nch = z_ref.shape[0] // CHUNK
    uv = jax.nn.gelu(z_ref[...])
    u = uv[:, :BRANCH_W]
    v = _rms(uv[:, BRANCH_W:], g_ref[...])
    v_ref[0] = v[(nch - 1) * CHUNK:, :]
    t = lax.broadcasted_iota(jnp.int32, (CHUNK, CHUNK), 0)
    s = lax.broadcasted_iota(jnp.int32, (CHUNK, CHUNK), 1)
    causal = t >= s
    vb = v.astype(BF16)
    ws = [jnp.where(causal, w_ref[g], 0.0).astype(BF16) for g in range(SG_GROUPS)]
    for ci in range(nch):
        rows = slice(ci * CHUNK, (ci + 1) * CHUNK)
        mixed = jnp.concatenate([_dot(ws[g], vb[rows, g * SG_GW:(g + 1) * SG_GW]) for g in range(SG_GROUPS)], axis=1)
        y_ref[rows, :] = (u[rows, :] * (mixed + b_ref[...])).astype(y_ref.dtype)


def _sg_prompt_call(z_c, g, w, b_exp, ys, batch, seq):
    rows = SG_STEP_CHUNKS * CHUNK if seq % (SG_STEP_CHUNKS * CHUNK) == 0 else CHUNK
    nblk = seq // rows
    return pl.pallas_call(
        _sg_prompt_body,
        out_shape=(_sds(ys.shape, ys.dtype), _sds((batch, CHUNK, BRANCH_W), F32)),
        grid=(batch, nblk),
        in_specs=[
            pl.BlockSpec((rows, W_SG), lambda b, i: (b * nblk + i, 0)),
            pl.BlockSpec((1, BRANCH_W), lambda b, i: (0, 0)),
            pl.BlockSpec((SG_GROUPS, CHUNK, CHUNK), lambda b, i: (0, 0, 0)),
            pl.BlockSpec((CHUNK, BRANCH_W), lambda b, i: (0, 0)),
            pl.BlockSpec(memory_space=pl.ANY),
        ],
        out_specs=(
            pl.BlockSpec((None, rows, BRANCH_W), lambda b, i: (2, b * nblk + i, 0)),
            pl.BlockSpec((1, CHUNK, BRANCH_W), lambda b, i: (b, 0, 0)),
        ),
        input_output_aliases={4: 0},
        compiler_params=_params("arbitrary", "arbitrary"),
        name="sg_prompt",
    )(z_c, g, w, b_exp, ys)


def _sg_sample_body(z_ref, g_ref, w00_ref, b_ref, y_ref, v_ref):
    uv = jax.nn.gelu(z_ref[...])
    u = uv[:, :BRANCH_W]
    v = _rms(uv[:, BRANCH_W:], g_ref[...])
    v_ref[...] = v
    w = w00_ref[...].astype(BF16).astype(F32)
    mixed = w * v.astype(BF16).astype(F32) + b_ref[0:1, :]
    y_ref[...] = (u * mixed).astype(y_ref.dtype)


def _sg_sample_call(z_c, g, w00, b_exp):
    nbatch = z_c.shape[0]
    return pl.pallas_call(
        _sg_sample_body,
        out_shape=(_sds((nbatch, BRANCH_W), F32), _sds((nbatch, BRANCH_W), F32)),
        name="sg_sample",
    )(z_c, g, w00, b_exp)


def _l2n(x):
    return x * lax.rsqrt(jnp.sum(x * x, axis=-1, keepdims=True) + EPS)


def _dn_gates(zab, alog_ref, dtb_ref):
    g = -jnp.exp(alog_ref[...]) * jax.nn.softplus(zab + dtb_ref[...])
    beta = jax.nn.sigmoid(zab)
    return g, beta


def _mm3(a, b):
    (ah, al), (bh, bl) = a, b
    return _dot(ah, bh) + (_dot(ah, bl) + _dot(al, bh))


def _unit_lower_inverse_off(mats, blk):
    n = mats[0].shape[0]
    r = lax.broadcasted_iota(jnp.int32, (n, n), 0)
    c = lax.broadcasted_iota(jnp.int32, (n, n), 1)

    def lower_left(size):
        return ((r // (2 * size)) == (c // (2 * size))) & ((r // size) % 2 == 1) & ((c // size) % 2 == 0)

    first = lower_left(1)
    offs = [-jnp.where(first, a, 0.0) for a in mats]
    size = 2
    while size < blk:
        mask = lower_left(size)
        ams = [jnp.where(mask, a, 0.0) for a in mats]
        obs = [off.astype(BF16) for off in offs]
        ps = [am + _dot(ob, am.astype(BF16)) for am, ob in zip(ams, obs)]
        offs = [off - (p + _dot(p.astype(BF16), ob)) for off, p, ob in zip(offs, ps, obs)]
        size *= 2
    return offs


def _head_lanes(x, rows_per_head):
    r = lax.broadcasted_iota(jnp.int32, x.shape, 0) // rows_per_head
    return jnp.concatenate([jnp.where(r == h, x, 0.0) for h in range(DN_H)], axis=1)


def _dn_chunk_terms(chunks):
    tc = DN_CHUNK
    n = DN_H * tc
    heads = range(DN_H)

    def stack(x, rows, off):
        return jnp.concatenate([x[rows, off + h * DN_HD:off + (h + 1) * DN_HD] for h in heads], axis=0)

    rows = [slice(r0, r0 + tc) for *_, r0 in chunks]
    qs = [_l2n(stack(ch[0], rw, 0)) * (DN_HD ** -0.5) for ch, rw in zip(chunks, rows)]
    ks = [_l2n(stack(ch[0], rw, BRANCH_W)) for ch, rw in zip(chunks, rows)]
    vs = [stack(ch[0], rw, 2 * BRANCH_W) for ch, rw in zip(chunks, rows)]
    zgss = [stack(ch[1], rw, 0) for ch, rw in zip(chunks, rows)]

    rt = lax.broadcasted_iota(jnp.int32, (tc, tc), 0)
    ct = lax.broadcasted_iota(jnp.int32, (tc, tc), 1)
    ones_tril = jnp.where(rt >= ct, 1.0, 0.0).astype(BF16)
    gparts = []
    for ch, rw in zip(chunks, rows):
        g = ch[2][rw]
        g_hi, g_lo = _split2(g)
        gparts.append((g_hi, g_lo, (g - g_hi.astype(F32) - g_lo.astype(F32)).astype(BF16)))
    gc_cs = [_dot(ones_tril, hi) + (_dot(ones_tril, lo) + _dot(ones_tril, lo2)) for hi, lo, lo2 in gparts]
    gcs = [jnp.concatenate([gc_c[:, h:h + 1] for h in heads], axis=0) for gc_c in gc_cs]
    betas = [jnp.concatenate([ch[3][rw, DN_H + h:DN_H + h + 1] for h in heads], axis=0)
             for ch, rw in zip(chunks, rows)]
    gc_lasts = [jnp.concatenate([jnp.broadcast_to(gc_c[tc - 1:tc, h:h + 1], (tc, 1)) for h in heads], axis=0)
                for gc_c in gc_cs]

    r = lax.broadcasted_iota(jnp.int32, (n, n), 0)
    c = lax.broadcasted_iota(jnp.int32, (n, n), 1)
    same = (r // tc) == (c // tc)
    tril = same & (r >= c)
    stril = same & (r > c)
    gmats = [jnp.broadcast_to(gc, (n, n)) for gc in gcs]
    decs = [jnp.where(tril, jnp.exp(jnp.where(tril, gm - gm.T, 0.0)), 0.0) for gm in gmats]
    kbs = [k.astype(BF16) for k in ks]
    kks = [_dot_nt(kb, kb) for kb in kbs]
    mats = [jnp.where(stril, beta * kk * dec, 0.0) for beta, kk, dec in zip(betas, kks, decs)]
    inv_offs = _unit_lower_inverse_off(mats, tc)

    egs = [jnp.exp(gc) for gc in gcs]
    rhss = [jnp.concatenate([beta * v, (beta * eg) * k], axis=1) for beta, eg, k, v in zip(betas, egs, ks, vs)]
    inv_parts = [_split2(x) for x in inv_offs]
    rhs_parts = [_split2(x) for x in rhss]
    sols = [rhs + _mm3(ip, rp) for rhs, ip, rp in zip(rhss, inv_parts, rhs_parts)]
    qks = [(_dot_nt(q.astype(BF16), kb) * dec).astype(BF16) for q, kb, dec in zip(qs, kbs, decs)]
    wqs = [jnp.concatenate([_head_lanes(sol[:, DN_HD:], tc), _head_lanes(q * eg, tc)], axis=0).astype(BF16)
           for sol, q, eg in zip(sols, qs, egs)]
    ke_ts = [_head_lanes(k * jnp.exp(gl - gc), tc).T.astype(BF16) for k, gl, gc in zip(ks, gc_lasts, gcs)]
    ges = [jnp.concatenate([jnp.broadcast_to(jnp.exp(gc_c[tc - 1:tc, h:h + 1]), (DN_HD, 1)) for h in heads], axis=0)
           for gc_c in gc_cs]
    return [(sol[:, :DN_HD], wq, qk, ke_t, ge, zgs)
            for sol, wq, qk, ke_t, ge, zgs in zip(sols, wqs, qks, ke_ts, ges, zgss)]


def _dn_chunk_steps(states, terms, ng):
    n = terms[0][0].shape[0]
    wss = [_dot(t[1], s.astype(BF16)) for s, t in zip(states, terms)]
    ubs = [(t[0] - ws[:n]).astype(BF16) for ws, t in zip(wss, terms)]
    outs = [ws[n:] + _dot(t[2], ub) for ws, ub, t in zip(wss, ubs, terms)]
    states = [t[4] * s + _dot(t[3], ub) for s, ub, t in zip(states, ubs, terms)]
    return states, [_rms(o, ng) * jax.nn.silu(t[5]) for o, t in zip(outs, terms)]


def _dn_prompt_body(x_ref, zg_ref, zab_ref, cw_ref, alog_ref, dtb_ref, ng_ref, ys_in_ref,
                    y_ref, so_ref, co_ref, xin_ref, s_ref):
    del ys_in_ref
    step = pl.program_id(0)
    nb, tb = x_ref.shape[0], x_ref.shape[1]
    tc = DN_CHUNK
    nci = tb // tc
    pad = SUBLANES

    @pl.when(step == 0)
    def _():
        xin_ref[:, 0:pad, :] = jnp.zeros((nb, pad, W_DN_QKV), F32)
        s_ref[...] = jnp.zeros_like(s_ref)

    @pl.when(step > 0)
    def _():
        xin_ref[:, 0:pad, :] = xin_ref[:, tb:tb + pad, :]

    chunks = []
    for b in range(nb):
        x = x_ref[b]
        xin_ref[b, pad:pad + tb, :] = x
        co_ref[b] = x_ref[b, tb - (DN_CONV - 1):tb, :]
        acc = cw_ref[DN_CONV - 1:DN_CONV, :] * x
        for i in range(DN_CONV - 1):
            lag = DN_CONV - 1 - i
            acc = acc + cw_ref[i:i + 1, :] * xin_ref[b, pad - lag:pad - lag + tb, :]
        qkv = jax.nn.silu(acc)
        g_all, beta_all = _dn_gates(zab_ref[b], alog_ref, dtb_ref)
        chunks += [(qkv, zg_ref[b], g_all, beta_all, ci * tc) for ci in range(nci)]
    terms = _dn_chunk_terms(chunks)

    states = [s_ref[b] for b in range(nb)]
    for ci in range(nci):
        states, outs = _dn_chunk_steps(states, [terms[b * nci + ci] for b in range(nb)], ng_ref[...])
        for b in range(nb):
            y_ref[b, ci * tc:(ci + 1) * tc, :] = jnp.concatenate(
                [outs[b][h * tc:(h + 1) * tc] for h in range(DN_H)], axis=1).astype(y_ref.dtype)
    for b in range(nb):
        s_ref[b] = states[b]

    @pl.when(step == pl.num_programs(0) - 1)
    def _():
        so_ref[...] = s_ref[...]


def _dn_prompt_call(z_dn, cw, alog, dtb, ng, ys, batch, seq):
    tb = DN_STEP_CHUNKS * DN_CHUNK if seq % (DN_STEP_CHUNKS * DN_CHUNK) == 0 else DN_CHUNK
    z3 = z_dn.reshape(batch, seq, W_DN)
    ys4 = ys.reshape(N_BRANCH, batch, seq, BRANCH_W)
    const = lambda i: (0, 0)
    whole = lambda i: (0, 0, 0)
    y, so, co = pl.pallas_call(
        _dn_prompt_body,
        out_shape=(_sds(ys4.shape, ys4.dtype), _sds((batch, DN_H * DN_HD, DN_HD), F32),
                   _sds((batch, DN_CONV - 1, W_DN_QKV), F32)),
        grid=(seq // tb,),
        in_specs=[
            pl.BlockSpec((batch, tb, W_DN_QKV), lambda i: (0, i, 0)),
            pl.BlockSpec((batch, tb, BRANCH_W), lambda i: (0, i, W_DN_QKV // BRANCH_W)),
            pl.BlockSpec((batch, tb, LANES), lambda i: (0, i, (W_DN_QKV + BRANCH_W) // LANES)),
            pl.BlockSpec((DN_CONV, W_DN_QKV), const),
            pl.BlockSpec((1, LANES), const),
            pl.BlockSpec((1, LANES), const),
            pl.BlockSpec((1, DN_HD), const),
            pl.BlockSpec(memory_space=pl.ANY),
        ],
        out_specs=(
            pl.BlockSpec((None, batch, tb, BRANCH_W), lambda i: (N_BRANCH - 1, 0, i, 0)),
            pl.BlockSpec((batch, DN_H * DN_HD, DN_HD), whole),
            pl.BlockSpec((batch, DN_CONV - 1, W_DN_QKV), whole),
        ),
        input_output_aliases={7: 0},
        scratch_shapes=[pltpu.VMEM((batch, tb + SUBLANES, W_DN_QKV), F32),
                        pltpu.VMEM((batch, DN_H * DN_HD, DN_HD), F32)],
        compiler_params=_params("arbitrary"),
        name="dn_prompt",
    )(z3, z3, z3, cw, alog, dtb, ng, ys4)
    return y.reshape(ys.shape), so.reshape(batch, DN_H, DN_HD, DN_HD), co


def _dn_sample_body(x_ref, zg_ref, zab_ref, conv_ref, s0_ref, cw_ref, alog_ref, dtb_ref, ng_ref,
                    y_ref, so_ref, co_ref):
    nrow = x_ref.shape[0]
    x_all = x_ref[...]
    acc = cw_ref[DN_CONV - 1:DN_CONV, :] * x_all
    for j in range(DN_CONV - 1):
        acc = acc + cw_ref[j:j + 1, :] * conv_ref[:, j, :]
    qkv_all = jax.nn.silu(acc)
    g_all, beta_all = _dn_gates(zab_ref[...], alog_ref, dtb_ref)
    zg_all = zg_ref[...]
    row8 = lax.broadcasted_iota(jnp.int32, (SUBLANES, DN_HD), 0)
    rows = range(nrow)
    heads = range(DN_H)
    for i in rows:
        co_ref[i, 0:DN_CONV - 2, :] = conv_ref[i, 1:DN_CONV - 1, :]
        co_ref[i, DN_CONV - 2:DN_CONV - 1, :] = x_all[i:i + 1, :]

    qs = [_l2n(qkv_all[:, h * DN_HD:(h + 1) * DN_HD]) * (DN_HD ** -0.5) for h in heads]
    ks = [_l2n(qkv_all[:, BRANCH_W + h * DN_HD:BRANCH_W + (h + 1) * DN_HD]) for h in heads]
    vs = [qkv_all[:, 2 * BRANCH_W + h * DN_HD:2 * BRANCH_W + (h + 1) * DN_HD] for h in heads]
    egs = [jnp.exp(g_all[:, h:h + 1]) for h in heads]
    betas = [beta_all[:, DN_H + h:DN_H + h + 1] for h in heads]
    w_ks = [(beta * eg) * k for beta, eg, k in zip(betas, egs, ks)]
    q_gs = [q * eg for q, eg in zip(qs, egs)]
    qks = [jnp.sum(q.astype(BF16).astype(F32) * k.astype(BF16).astype(F32), axis=-1, keepdims=True)
           for q, k in zip(qs, ks)]
    bvs = [beta * v for beta, v in zip(betas, vs)]

    pairs = [(i, h) for i in rows for h in heads]
    lhs = [jnp.where(row8 == 0, jnp.broadcast_to(w_ks[h][i:i + 1, :], (SUBLANES, DN_HD)),
                     jnp.where(row8 == 1, jnp.broadcast_to(q_gs[h][i:i + 1, :], (SUBLANES, DN_HD)), 0.0)).astype(BF16)
           for i, h in pairs]
    prods = [_dot(l, s0_ref[i, h].astype(BF16)) for l, (i, h) in zip(lhs, pairs)]
    us = [bvs[h][i:i + 1, :] - p[0:1, :] for p, (i, h) in zip(prods, pairs)]
    outs = [p[1:2, :] + qks[h][i:i + 1, :] * u for p, u, (i, h) in zip(prods, us, pairs)]
    kcols = []
    for i in rows:
        k8 = jnp.zeros((SUBLANES, DN_HD), F32)
        for h in heads:
            k8 = jnp.where(row8 == h, jnp.broadcast_to(ks[h][i:i + 1, :], (SUBLANES, DN_HD)), k8)
        kcols.append(k8.T)
    for u, (i, h) in zip(us, pairs):
        so_ref[i, h] = egs[h][i:i + 1, :] * s0_ref[i, h] + kcols[i][:, h:h + 1] * u

    ys = []
    for h in heads:
        o_h = jnp.concatenate([outs[i * DN_H + h] for i in rows], axis=0)
        ys.append(_rms(o_h, ng_ref[...]) * jax.nn.silu(zg_all[:, h * DN_HD:(h + 1) * DN_HD]))
    y_ref[...] = jnp.concatenate(ys, axis=1).astype(y_ref.dtype)


def _dn_sample_call(z_dn, conv0, s0, cw, alog, dtb, ng):
    nbatch = z_dn.shape[0]
    rows = SAMPLE_ROWS
    const = lambda b: (0, 0)
    return pl.pallas_call(
        _dn_sample_body,
        out_shape=(_sds((nbatch, BRANCH_W), F32), _sds((nbatch, DN_H, DN_HD, DN_HD), F32),
                   _sds((nbatch, DN_CONV - 1, W_DN_QKV), F32)),
        grid=(nbatch // rows,),
        in_specs=[
            pl.BlockSpec((rows, W_DN_QKV), lambda b: (b, 0)),
            pl.BlockSpec((rows, BRANCH_W), lambda b: (b, W_DN_QKV // BRANCH_W)),
            pl.BlockSpec((rows, LANES), lambda b: (b, (W_DN_QKV + BRANCH_W) // LANES)),
            pl.BlockSpec((rows, DN_CONV - 1, W_DN_QKV), lambda b: (b, 0, 0)),
            pl.BlockSpec((rows, DN_H, DN_HD, DN_HD), lambda b: (b, 0, 0, 0)),
            pl.BlockSpec((DN_CONV, W_DN_QKV), const),
            pl.BlockSpec((1, LANES), const),
            pl.BlockSpec((1, LANES), const),
            pl.BlockSpec((1, DN_HD), const),
        ],
        out_specs=(
            pl.BlockSpec((rows, BRANCH_W), lambda b: (b, 0)),
            pl.BlockSpec((rows, DN_H, DN_HD, DN_HD), lambda b: (b, 0, 0, 0)),
            pl.BlockSpec((rows, DN_CONV - 1, W_DN_QKV), lambda b: (b, 0, 0)),
        ),
        compiler_params=_params("parallel"),
        name="dn_sample",
    )(z_dn, z_dn, z_dn, conv0, s0, cw, alog, dtb, ng)


def _rope_tables(pos):
    half = ATT_HD // 2
    inv = ROPE_THETA ** (-jnp.arange(half, dtype=F32) / half)
    ang = pos.astype(F32)[:, None] * inv[None, :]
    cos, sin = jnp.cos(ang), jnp.sin(ang)
    reps = LANES // ATT_HD
    return (jnp.tile(jnp.concatenate([cos, cos], axis=1), (1, reps)),
            jnp.tile(jnp.concatenate([-sin, sin], axis=1), (1, reps)))


def _ssm_matrices(lam_re, lam_im, log_dt, b_re, b_im, c_re, c_im):
    dt = jnp.exp(log_dt)[:, None]
    mag = jnp.exp(lam_re * dt)
    a_re, a_im = mag * jnp.cos(lam_im * dt), mag * jnp.sin(lam_im * dt)
    den = lam_re * lam_re + lam_im * lam_im
    f_re = ((a_re - 1.0) * lam_re + a_im * lam_im) / den
    f_im = (a_im * lam_re - (a_re - 1.0) * lam_im) / den
    bb_re = f_re[..., None] * b_re - f_im[..., None] * b_im
    bb_im = f_re[..., None] * b_im + f_im[..., None] * b_re
    gpb = SSM_NG // SSM_BLOCKS
    eye = jnp.eye(gpb, dtype=F32)

    def in_layout(b):
        b = b.reshape(SSM_BLOCKS, gpb, SSM_P, SSM_GROUP)
        return jnp.einsum("jipc,ik->jickp", b, eye).reshape(BRANCH_W, SSM_BLOCK_N)

    def out_layout(c):
        c = c.reshape(SSM_BLOCKS, gpb, SSM_GROUP, SSM_P)
        return jnp.einsum("jicp,ik->jipkc", c, eye).reshape(SSM_N, SSM_BLOCK_CH)

    bm = jnp.concatenate([in_layout(bb_re), in_layout(bb_im)], axis=1).astype(BF16)
    cm = jnp.concatenate([out_layout(c_re), out_layout(-c_im)], axis=0).astype(BF16)
    a = jnp.stack([a_re.reshape(SSM_N), a_im.reshape(SSM_N)], axis=0)
    return bm, cm, a


def _w_main_body(w_ref, o_ref):
    rb = o_ref.shape[0]
    row = pl.program_id(1) * rb + lax.broadcasted_iota(jnp.int32, (rb, 1), 0)
    o_ref[...] = jnp.where(row < N_MAIN, w_ref[...], 0.0).astype(o_ref.dtype)


def _w_gate_body(w_ref, buf_ref, o_ref):
    del buf_ref
    o_ref[...] = w_ref[0].astype(o_ref.dtype)


def _w_in_layout_call(w_t):
    depth, n_in, d = w_t.shape
    rows = GATE_COL0 + N_BRANCH * d
    rb_main = W_MAIN // 7
    assert W_MAIN % rb_main == 0 and rb_main % (2 * SUBLANES) == 0 and n_in - N_MAIN == N_BRANCH * d
    buf = pl.pallas_call(
        _w_main_body,
        out_shape=_sds((depth, rows, d), BF16),
        grid=(depth, W_MAIN // rb_main),
        in_specs=[pl.BlockSpec((None, rb_main, d), lambda l, i: (l, i, 0))],
        out_specs=pl.BlockSpec((None, rb_main, d), lambda l, i: (l, i, 0)),
        compiler_params=_params("parallel", "parallel"),
        name="w_main_layout",
    )(w_t)
    rb = 512
    return pl.pallas_call(
        _w_gate_body,
        out_shape=_sds((depth, rows, d), BF16),
        grid=(depth, N_BRANCH * d // rb),
        in_specs=[
            pl.BlockSpec((pl.Element(1), pl.Element(rb), pl.Element(d)),
                         lambda l, i: (l, pl.multiple_of(N_MAIN + i * rb, SUBLANES), 0)),
            pl.BlockSpec(memory_space=pl.ANY),
        ],
        out_specs=pl.BlockSpec((None, rb, d), lambda l, i: (l, GATE_COL0 // rb + i, 0)),
        input_output_aliases={1: 0},
        compiler_params=_params("parallel", "parallel"),
        name="w_gate_layout",
    )(w_t, buf)


def _lane_rows(x):
    return jnp.pad(x, ((0, 0), (0, LANES - x.shape[1])))[:, None, :]


def _prepare(p):
    depth = p["w_in"].shape[0]
    w_in = p["w_in"]
    bm, cm, a = jax.vmap(_ssm_matrices)(p["ssm_lam_re"], p["ssm_lam_im"], p["ssm_log_dt"], p["ssm_b_re"],
                                        p["ssm_b_im"], p["ssm_c_re"], p["ssm_c_im"])
    w_all = _w_in_layout_call(jnp.swapaxes(w_in, 1, 2))
    stacked = dict(
        w_all=w_all, w_branch=p["w_branch"].astype(BF16), w_out=p["w_out"].astype(BF16),
        w_ff1=p["w_ff1"].astype(BF16), w_ff2=p["w_ff2"].astype(BF16),
        bm=bm, cm=cm, glu_w=p["ssm_glu_w"].astype(BF16),
    )
    per_layer = dict(
        norm1_g=p["norm1_g"][:, None, :], norm2_g=p["norm2_g"][:, None, :],
        qg=jnp.tile(p["att_qn_g"], (1, ATT_HQ))[:, None, :],
        kg=jnp.tile(p["att_kn_g"], (1, ATT_HKV))[:, None, :],
        sink=p["att_sink"],
        sink_b=jnp.broadcast_to(p["att_sink"][:, :, None], (depth, ATT_HQ, LANES)),
        a=a, ssm_d=p["ssm_d"][:, None, :], glu_b=p["ssm_glu_b"][:, None, :],
        sg_g=p["sg_norm_g"][:, None, :], sg_w=p["sg_w"],
        sg_b=jnp.repeat(jnp.swapaxes(p["sg_b"], 1, 2), SG_GW, axis=2),
        sg_w00=jnp.repeat(p["sg_w"][:, :, 0, 0], SG_GW, axis=1)[:, None, :],
        cw=p["dn_conv_w"], alog=_lane_rows(p["dn_a_log"]), dtb=_lane_rows(p["dn_dt_bias"]),
        ng=p["dn_norm_g"][:, None, :],
    )
    return stacked, per_layer


def _mix_and_ffn(x, h, ys, wts, layer, lp, g_next, tm):
    mix = _merge_call(h, ys, wts["w_all"], wts["w_branch"], layer, tm)
    x, h2 = _proj_call(mix, wts["w_out"], layer, x, lp["norm2_g"], tm)
    return _ffn_call(h2, wts["w_ff1"], wts["w_ff2"], layer, x, g_next, tm, FFN_TF)


def _layer(xp, hp, xs, hs, st, wts, layer, lp, g_next, rope_p, rope_s, e, batch, seq):
    tmp = ROWS_PROMPT if xp.shape[0] % ROWS_PROMPT == 0 else xp.shape[0]
    tms = xs.shape[0]
    ck, cv, s0r, s0i, s0d, conv0 = st
    ssm_w = (wts["bm"], wts["cm"], lp["a"], lp["ssm_d"], wts["glu_w"], lp["glu_b"], layer)

    z_att, z_ssm, z_sg, z_dn = _in_proj_call(hp, wts["w_all"], layer, tmp)
    ys, pk, pv = _attn_prompt_call(z_att, rope_p[0], rope_p[1], lp["qg"], lp["kg"], e, lp["sink"], batch, seq)
    tb = 512 if seq % 512 == 0 else seq
    ys, pre, pim = _ssm_prompt_call(z_ssm, *ssm_w, ys, batch, seq, tb)
    ys, pchunk = _sg_prompt_call(z_sg, lp["sg_g"], lp["sg_w"], lp["sg_b"], ys, batch, seq)
    ys, pdelta, pconv = _dn_prompt_call(z_dn, lp["cw"], lp["alog"], lp["dtb"], lp["ng"], ys, batch, seq)
    xp, hp = _mix_and_ffn(xp, hp, ys, wts, layer, lp, g_next, tmp)

    z_att, z_ssm, z_sg, z_dn = _in_proj_call(hs, wts["w_all"], layer, tms)
    ya, sk, sv = _attn_sample_call(z_att, ck, cv, rope_s[0], rope_s[1], lp["qg"], lp["kg"], e, lp["sink_b"])
    yb, sre, sim = _ssm_sample_call(z_ssm, s0r, s0i, *ssm_w)
    yc, schunk = _sg_sample_call(z_sg, lp["sg_g"], lp["sg_w00"], lp["sg_b"])
    yd, sdelta, sconv = _dn_sample_call(z_dn, conv0, s0d, lp["cw"], lp["alog"], lp["dtb"], lp["ng"])
    xs, hs = _mix_and_ffn(xs, hs, jnp.stack([ya, yb, yc, yd]).astype(BF16), wts, layer, lp, g_next, tms)

    new_p = (pk, pv, pre, pim, pdelta, pconv, pchunk)
    new_s = (sk, sv, sre, sim, sdelta, sconv, schunk)
    return xp, hp, xs, hs, new_p, new_s


def kernel(x_prompt, x_sample, cache_k, cache_v, state_ssm_re, state_ssm_im, state_delta, state_conv, norm1_g, w_in, att_qn_g, att_kn_g, att_sink, ssm_lam_re, ssm_lam_im, ssm_log_dt, ssm_b_re, ssm_b_im, ssm_c_re, ssm_c_im, ssm_d, ssm_glu_w, ssm_glu_b, sg_norm_g, sg_w, sg_b, dn_conv_w, dn_a_log, dn_dt_bias, dn_norm_g, w_branch, w_out, norm2_g, w_ff1, w_ff2):
    p = dict(norm1_g=norm1_g, w_in=w_in, att_qn_g=att_qn_g, att_kn_g=att_kn_g, att_sink=att_sink,
             ssm_lam_re=ssm_lam_re, ssm_lam_im=ssm_lam_im, ssm_log_dt=ssm_log_dt, ssm_b_re=ssm_b_re,
             ssm_b_im=ssm_b_im, ssm_c_re=ssm_c_re, ssm_c_im=ssm_c_im, ssm_d=ssm_d, ssm_glu_w=ssm_glu_w,
             ssm_glu_b=ssm_glu_b, sg_norm_g=sg_norm_g, sg_w=sg_w, sg_b=sg_b, dn_conv_w=dn_conv_w, dn_a_log=dn_a_log,
             dn_dt_bias=dn_dt_bias, dn_norm_g=dn_norm_g, w_branch=w_branch, w_out=w_out, norm2_g=norm2_g,
             w_ff1=w_ff1, w_ff2=w_ff2)
    depth = w_in.shape[0]
    batch, seq, d = x_prompt.shape
    nsamp = x_sample.shape[0]
    past = cache_k.shape[2]
    assert x_sample.shape[1] == 1 and past == WINDOW and seq % CHUNK == 0 and d == D_MODEL
    assert nsamp % SAMPLE_ROWS == 0
    nkv = ATT_HKV * ATT_HD

    xp = x_prompt.reshape(batch * seq, d)
    xs = x_sample.reshape(nsamp, d)
    rope_p = _rope_tables(jnp.arange(seq))
    rope_s = _rope_tables(PAST_LEN + jnp.arange(1))
    lane = jnp.arange(BRANCH_W) // ATT_HD
    e = (lane[:, None] == lane[None, :]).astype(BF16)

    wts, prm = _prepare(p)
    ck_all = cache_k.reshape(depth, nsamp, past, nkv)
    cv_all = cache_v.reshape(depth, nsamp, past, nkv)
    s0r_all = state_ssm_re.reshape(depth, nsamp, SSM_N)
    s0i_all = state_ssm_im.reshape(depth, nsamp, SSM_N)

    hp = _rmsnorm_call(xp, prm["norm1_g"][0], ROWS_PROMPT if xp.shape[0] % ROWS_PROMPT == 0 else xp.shape[0])
    hs = _rmsnorm_call(xs, prm["norm1_g"][0], nsamp)
    new_p = [[] for _ in range(7)]
    new_s = [[] for _ in range(7)]
    for l in range(depth):
        lp = {name: v[l] for name, v in prm.items()}
        st = (ck_all[l], cv_all[l], s0r_all[l], s0i_all[l], state_delta[l], state_conv[l])
        g_next = prm["norm1_g"][l + 1] if l + 1 < depth else jnp.ones((1, d), F32)
        xp, hp, xs, hs, sp, ss = _layer(xp, hp, xs, hs, st, wts, l, lp, g_next, rope_p, rope_s, e, batch, seq)
        for i in range(7):
            new_p[i].append(sp[i])
            new_s[i].append(ss[i])

    def stack(parts, shape):
        return jnp.stack(parts, axis=0).reshape((depth,) + shape)

    keep = min(WINDOW, seq)
    return (
        xp.reshape(batch, seq, d), xs.reshape(nsamp, 1, d),
        stack(new_p[0], (batch, keep, ATT_HKV, ATT_HD)), stack(new_p[1], (batch, keep, ATT_HKV, ATT_HD)),
        stack(new_p[2], (batch, SSM_NG, SSM_P)), stack(new_p[3], (batch, SSM_NG, SSM_P)),
        stack(new_p[4], (batch, DN_H, DN_HD, DN_HD)), stack(new_p[5], (batch, DN_CONV - 1, W_DN_QKV)),
        stack(new_p[6], (batch, CHUNK, BRANCH_W)),
        stack(new_s[0], (nsamp, past, ATT_HKV, ATT_HD)), stack(new_s[1], (nsamp, past, ATT_HKV, ATT_HD)),
        stack(new_s[2], (nsamp, SSM_NG, SSM_P)), stack(new_s[3], (nsamp, SSM_NG, SSM_P)),
        stack(new_s[4], (nsamp, DN_H, DN_HD, DN_HD)), stack(new_s[5], (nsamp, DN_CONV - 1, W_DN_QKV)),
        stack(new_s[6], (nsamp, 1, BRANCH_W)),
    )
```

```python
import jax
import jax.numpy as jnp
from jax import lax
from jax.experimental import pallas as pl
from jax.experimental.pallas import tpu as pltpu

F32 = jnp.float32
BF16 = jnp.bfloat16

D_MODEL = 2048
PAST_LEN = 16384
N_BRANCH = 4
BRANCH_W = D_MODEL // N_BRANCH
ATT_HD = 64
ATT_HQ = BRANCH_W // ATT_HD
ATT_HKV = 2
ATT_G = ATT_HQ // ATT_HKV
WINDOW = 128
ROPE_THETA = 10000.0
SSM_GROUP = 16
SSM_NG = BRANCH_W // SSM_GROUP
SSM_P = 64
SSM_N = SSM_NG * SSM_P
SSM_SEGS = 8
CHUNK = 128
SG_GROUPS = 4
SG_GW = BRANCH_W // SG_GROUPS
SG_STEP_CHUNKS = 4
DN_HD = 128
DN_H = BRANCH_W // DN_HD
DN_CONV = 4
DN_CHUNK = 64
DN_STEP_CHUNKS = 2
D_FF = 4 * D_MODEL
EPS = 1e-6
NEG_INF = -1e30

LANES = 128
SUBLANES = 8
V7X_VMEM_LIMIT = 56 * 1024 * 1024

W_ATT = ATT_HQ * ATT_HD + 2 * ATT_HKV * ATT_HD
W_SSM = BRANCH_W
W_SG = 2 * BRANCH_W
W_DN_QKV = 3 * BRANCH_W
W_DN = W_DN_QKV + BRANCH_W + LANES
N_MAIN = W_ATT + W_SSM + W_SG + W_DN_QKV + BRANCH_W + 2 * DN_H
W_MAIN = W_ATT + W_SSM + W_SG + W_DN
GATE_COL0 = 3 * D_MODEL

SSM_BLOCK_CH = LANES
SSM_BLOCKS = BRANCH_W // SSM_BLOCK_CH
SSM_BLOCK_N = SSM_N // SSM_BLOCKS

ROWS_PROMPT = 512
FFN_TF = 1024
SAMPLE_ROWS = 8


def _params(*sem):
    return pltpu.CompilerParams(dimension_semantics=sem, vmem_limit_bytes=V7X_VMEM_LIMIT)


def _sds(shape, dtype):
    return jax.ShapeDtypeStruct(shape, dtype)


def _dot(a, b, precision=None):
    return jnp.dot(a, b, preferred_element_type=F32, precision=precision)


def _dot_nt(a, b, precision=None):
    return lax.dot_general(a, b, (((1,), (1,)), ((), ())), preferred_element_type=F32, precision=precision)


def _rms(x, g):
    ms = jnp.mean(x * x, axis=-1, keepdims=True)
    return x * lax.rsqrt(ms + EPS) * g


def _split2(x):
    hi = x.astype(BF16)
    return hi, (x - hi.astype(F32)).astype(BF16)


def _split_dot(x, e):
    hi, lo = _split2(x)
    return _dot(hi, e) + _dot(lo, e)


def _rmsnorm_body(x_ref, g_ref, o_ref):
    o_ref[...] = _rms(x_ref[...], g_ref[...]).astype(o_ref.dtype)


def _rmsnorm_call(x, g, tm):
    m, d = x.shape
    return pl.pallas_call(
        _rmsnorm_body,
        out_shape=_sds((m, d), BF16),
        grid=(m // tm,),
        in_specs=[pl.BlockSpec((tm, d), lambda i: (i, 0)), pl.BlockSpec((1, d), lambda i: (0, 0))],
        out_specs=pl.BlockSpec((tm, d), lambda i: (i, 0)),
        compiler_params=_params("parallel"),
        name="rmsnorm",
    )(x, g)


def _in_proj_body(h_ref, w_ref, *out_refs):
    h = h_ref[...]
    off = 0
    for o_ref in out_refs:
        width = o_ref.shape[1]
        o_ref[...] = _dot_nt(h, w_ref[off:off + width, :])
        off += width


def _in_proj_call(h, w_all, layer, tm):
    m, k = h.shape
    widths = (W_ATT, W_SSM, W_SG, W_DN)
    row = lambda i: (i, 0)
    return pl.pallas_call(
        _in_proj_body,
        out_shape=tuple(_sds((m, n), F32) for n in widths),
        grid=(m // tm,),
        in_specs=[pl.BlockSpec((tm, k), row),
                  pl.BlockSpec((None, W_MAIN, k), lambda i: (layer, 0, 0), pipeline_mode=pl.Buffered(1))],
        out_specs=tuple(pl.BlockSpec((tm, n), row) for n in widths),
        compiler_params=_params("parallel"),
        name="in_proj",
    )(h, w_all)


def _merge_body(h_ref, y_ref, wg_ref, wb_ref, o_ref, acc_ref):
    n = pl.program_id(1)

    @pl.when(n == 0)
    def _():
        acc_ref[...] = jnp.zeros_like(acc_ref)

    gate = jax.nn.sigmoid(_dot_nt(h_ref[...], wg_ref[...]))
    acc_ref[...] += gate * _dot(y_ref[...], wb_ref[...])

    @pl.when(n == N_BRANCH - 1)
    def _():
        o_ref[...] = acc_ref[...].astype(o_ref.dtype)


def _merge_call(h, ys, w_all, wb, layer, tm):
    m, d = h.shape
    gate0 = GATE_COL0 // d
    return pl.pallas_call(
        _merge_body,
        out_shape=_sds((m, d), BF16),
        grid=(m // tm, N_BRANCH),
        in_specs=[
            pl.BlockSpec((tm, d), lambda i, n: (i, 0)),
            pl.BlockSpec((None, tm, BRANCH_W), lambda i, n: (n, i, 0)),
            pl.BlockSpec((None, d, d), lambda i, n: (layer, gate0 + n, 0)),
            pl.BlockSpec((None, None, BRANCH_W, d), lambda i, n: (layer, n, 0, 0)),
        ],
        out_specs=pl.BlockSpec((tm, d), lambda i, n: (i, 0)),
        scratch_shapes=[pltpu.VMEM((tm, d), F32)],
        compiler_params=_params("parallel", "arbitrary"),
        name="merge",
    )(h, ys, w_all, wb)


def _proj_body(m_ref, w_ref, x_ref, g_ref, xo_ref, ho_ref):
    xn = x_ref[...] + _dot(m_ref[...], w_ref[...])
    xo_ref[...] = xn
    ho_ref[...] = _rms(xn, g_ref[...]).astype(ho_ref.dtype)


def _proj_call(mix, w, layer, x, g, tm):
    m, d = x.shape
    return pl.pallas_call(
        _proj_body,
        out_shape=(_sds((m, d), F32), _sds((m, d), BF16)),
        grid=(m // tm,),
        in_specs=[
            pl.BlockSpec((tm, d), lambda i: (i, 0)),
            pl.BlockSpec((None, d, d), lambda i: (layer, 0, 0)),
            pl.BlockSpec((tm, d), lambda i: (i, 0)),
            pl.BlockSpec((1, d), lambda i: (0, 0)),
        ],
        out_specs=(pl.BlockSpec((tm, d), lambda i: (i, 0)), pl.BlockSpec((tm, d), lambda i: (i, 0))),
        compiler_params=_params("parallel"),
        name="out_proj",
    )(mix, w, x, g)


def _ffn_body(h_ref, w1_ref, w2_ref, x_ref, g_ref, xo_ref, ho_ref):
    f = pl.program_id(1)

    @pl.when(f == 0)
    def _():
        xo_ref[...] = x_ref[...]

    a = _dot(h_ref[...], w1_ref[...])
    a = jnp.square(jnp.maximum(a, 0.0)).astype(BF16)
    xo_ref[...] += _dot(a, w2_ref[...])

    @pl.when(f == pl.num_programs(1) - 1)
    def _():
        ho_ref[...] = _rms(xo_ref[...], g_ref[...]).astype(ho_ref.dtype)


def _ffn_call(h, w1, w2, layer, x, g_next, tm, tf):
    m, d = x.shape
    dff = w1.shape[2]
    return pl.pallas_call(
        _ffn_body,
        out_shape=(_sds((m, d), F32), _sds((m, d), BF16)),
        grid=(m // tm, dff // tf),
        in_specs=[
            pl.BlockSpec((tm, d), lambda i, f: (i, 0)),
            pl.BlockSpec((None, d, tf), lambda i, f: (layer, 0, f)),
            pl.BlockSpec((None, tf, d), lambda i, f: (layer, f, 0)),
            pl.BlockSpec((tm, d), lambda i, f: (i, 0)),
            pl.BlockSpec((1, d), lambda i, f: (0, 0)),
        ],
        out_specs=(pl.BlockSpec((tm, d), lambda i, f: (i, 0)), pl.BlockSpec((tm, d), lambda i, f: (i, 0))),
        compiler_params=_params("parallel", "arbitrary"),
        name="ffn",
    )(h, w1, w2, x, g_next)


def _head_norm_rope(x, g, cos, sin, e):
    w = x.shape[1]
    ss = _split_dot(x * x, e)
    xn = x * lax.rsqrt(ss * (1.0 / ATT_HD) + EPS) * g
    lane = lax.broadcasted_iota(jnp.int32, x.shape, 1)
    first_half = (lane % ATT_HD) < (ATT_HD // 2)
    rot = jnp.where(first_half, pltpu.roll(xn, w - ATT_HD // 2, 1), pltpu.roll(xn, ATT_HD // 2, 1))
    return xn * cos + rot * sin


def _qkv_norm_rope(z, cos, sin, qg_ref, kg_ref, e_ref):
    nq = ATT_HQ * ATT_HD
    nk = ATT_HKV * ATT_HD
    rep = nq // LANES
    q = _head_norm_rope(z[:, :nq], qg_ref[...], jnp.tile(cos, (1, rep)), jnp.tile(sin, (1, rep)), e_ref[...])
    k = _head_norm_rope(z[:, nq:nq + nk], kg_ref[...], cos, sin, e_ref[:nk, :nk])
    return q, k, z[:, nq + nk:]


def _half_placed(x):
    lane = lax.broadcasted_iota(jnp.int32, x.shape, 1)
    lo = lane < ATT_HD
    xr = pltpu.roll(x, ATT_HD, 1)
    zero = jnp.zeros_like(x)
    return ((jnp.where(lo, x, zero), jnp.where(lo, zero, xr)),
            (jnp.where(lo, xr, zero), jnp.where(lo, zero, x)))


def _attn_prompt_body(z_ref, cos_ref, sin_ref, qg_ref, kg_ref, e_ref, sink_ref,
                      y_ref, ko_ref, vo_ref, kprev_ref, vprev_ref):
    i = pl.program_id(0)
    nb = z_ref.shape[0]

    @pl.when(i == 0)
    def _():
        kprev_ref[...] = jnp.zeros_like(kprev_ref)
        vprev_ref[...] = jnp.zeros_like(vprev_ref)

    cos = jnp.concatenate([cos_ref[...]] * nb, axis=0)
    sin = jnp.concatenate([sin_ref[...]] * nb, axis=0)
    z = jnp.concatenate([z_ref[b] for b in range(nb)], axis=0)
    q_all, k_all, v_all = _qkv_norm_rope(z, cos, sin, qg_ref, kg_ref, e_ref)
    qb_all = q_all.astype(BF16)
    kparts, vparts, qst = [], [], []
    for b in range(nb):
        rows_b = slice(b * WINDOW, (b + 1) * WINDOW)
        k, v = k_all[rows_b], v_all[rows_b]
        ko_ref[b] = k
        vo_ref[b] = v
        kparts.append(_half_placed(jnp.concatenate([kprev_ref[b], k], axis=0)))
        vparts.append(_half_placed(jnp.concatenate([vprev_ref[b], v], axis=0)))
        kprev_ref[b] = k
        vprev_ref[b] = v
        qst.append([jnp.concatenate([qb_all[rows_b, LANES * (2 * kv):LANES * (2 * kv + 1)],
                                     qb_all[rows_b, LANES * (2 * kv + 1):LANES * (2 * kv + 2)]], axis=0)
                    for kv in range(ATT_HKV)])

    rows = 2 * WINDOW
    r = lax.broadcasted_iota(jnp.int32, (rows, rows), 0) % WINDOW
    c = lax.broadcasted_iota(jnp.int32, (rows, rows), 1)
    rel = r - c + WINDOW
    mask = (rel >= 0) & (rel < WINDOW) & ((c >= WINDOW) | (i > 0))
    top = lax.broadcasted_iota(jnp.int32, (rows, 1), 0) < WINDOW

    combos = [(b, kv, parity) for b in range(nb) for kv in range(ATT_HKV) for parity in range(2)]
    scores = [_dot_nt(qst[b][kv], kparts[b][kv][parity].astype(BF16)) * (ATT_HD ** -0.5) for b, kv, parity in combos]
    scores = [jnp.where(mask, s, NEG_INF) for s in scores]
    sinks = [jnp.where(top, sink_ref[ATT_G * kv + parity], sink_ref[ATT_G * kv + 2 + parity])
             for _, kv, parity in combos]
    maxes = [jnp.maximum(jnp.max(s, axis=-1, keepdims=True), sink) for s, sink in zip(scores, sinks)]
    exps = [jnp.exp(s - mx) for s, mx in zip(scores, maxes)]
    dens = [jnp.sum(ex, axis=-1, keepdims=True) + jnp.exp(sink - mx) for ex, sink, mx in zip(exps, sinks, maxes)]
    probs = [(ex / den).astype(BF16) for ex, den in zip(exps, dens)]
    outs = [_dot(p, vparts[b][kv][parity].astype(BF16)) for p, (b, kv, parity) in zip(probs, combos)]
    for b in range(nb):
        pair_out = []
        for kv in range(ATT_HKV):
            acc = outs[(b * ATT_HKV + kv) * 2] + outs[(b * ATT_HKV + kv) * 2 + 1]
            pair_out += [acc[:WINDOW], acc[WINDOW:]]
        y_ref[b] = jnp.concatenate(pair_out, axis=1).astype(y_ref.dtype)


def _attn_prompt_call(z_att, cos, sin, qg, kg, e, sink, batch, seq):
    nb = seq // WINDOW
    nkv = ATT_HKV * ATT_HD
    const = lambda i: (0, 0)
    whole = lambda i: (0, 0, 0)
    ys, ko, vo = pl.pallas_call(
        _attn_prompt_body,
        out_shape=(_sds((N_BRANCH, batch, seq, BRANCH_W), BF16), _sds((batch, WINDOW, nkv), F32),
                   _sds((batch, WINDOW, nkv), F32)),
        grid=(nb,),
        in_specs=[
            pl.BlockSpec((batch, WINDOW, W_ATT), lambda i: (0, i, 0)),
            pl.BlockSpec((WINDOW, LANES), lambda i: (i, 0)),
            pl.BlockSpec((WINDOW, LANES), lambda i: (i, 0)),
            pl.BlockSpec((1, BRANCH_W), const),
            pl.BlockSpec((1, nkv), const),
            pl.BlockSpec((BRANCH_W, BRANCH_W), const),
            pl.BlockSpec(memory_space=pltpu.SMEM),
        ],
        out_specs=(
            pl.BlockSpec((None, batch, WINDOW, BRANCH_W), lambda i: (0, 0, i, 0)),
            pl.BlockSpec((batch, WINDOW, nkv), whole),
            pl.BlockSpec((batch, WINDOW, nkv), whole),
        ),
        scratch_shapes=[pltpu.VMEM((batch, WINDOW, nkv), F32), pltpu.VMEM((batch, WINDOW, nkv), F32)],
        compiler_params=_params("arbitrary"),
        name="attn_prompt",
    )(z_att.reshape(batch, seq, W_ATT), cos, sin, qg, kg, e, sink)
    return ys.reshape(N_BRANCH, batch * seq, BRANCH_W), ko, vo


def _attn_sample_body(z_ref, ck_ref, cv_ref, cos_ref, sin_ref, qg_ref, kg_ref, e_ref, sink_ref,
                      y_ref, ko_ref, vo_ref):
    nrow = z_ref.shape[0]
    q_all, k_all, v_all = _qkv_norm_rope(z_ref[...], cos_ref[...], sin_ref[...], qg_ref, kg_ref, e_ref)
    lane = lax.broadcasted_iota(jnp.int32, (1, LANES), 1)
    lo = lane < ATT_HD
    row = lax.broadcasted_iota(jnp.int32, (ATT_HQ, LANES), 0)
    scale = ATT_HD ** -0.5
    sink = sink_ref[:, 0:1]
    rows = range(nrow)
    ks = [k_all[i:i + 1, :] for i in rows]
    vs = [v_all[i:i + 1, :] for i in rows]
    for i in rows:
        ko_ref[i, 0:WINDOW - 1, :] = ck_ref[i, 1:WINDOW, :]
        ko_ref[i, WINDOW - 1:WINDOW, :] = ks[i]
        vo_ref[i, 0:WINDOW - 1, :] = cv_ref[i, 1:WINDOW, :]
        vo_ref[i, WINDOW - 1:WINDOW, :] = vs[i]

    qexps = []
    for i in rows:
        qexp = jnp.zeros((ATT_HQ, LANES), F32)
        for h in range(ATT_HQ):
            pair = q_all[i:i + 1, LANES * (h // 2):LANES * (h // 2 + 1)]
            want_lo = (h // ATT_G) == 0
            have_lo = (h % 2) == 0
            src = pair if want_lo == have_lo else pltpu.roll(pair, ATT_HD, 1)
            placed = jnp.where(lo if want_lo else jnp.logical_not(lo), src, 0.0)
            qexp = jnp.where(row == h, jnp.broadcast_to(placed, (ATT_HQ, LANES)), qexp)
        qexps.append(qexp)
    qbs = [qexp.astype(BF16) for qexp in qexps]
    key = lax.broadcasted_iota(jnp.int32, (ATT_HQ, WINDOW), 1)
    scores = [jnp.where(key >= 1, _dot_nt(qb, ck_ref[i].astype(BF16)) * scale, NEG_INF) for i, qb in zip(rows, qbs)]
    s_news = [jnp.sum(qb.astype(F32) * k.astype(BF16).astype(F32), axis=-1, keepdims=True) * scale
              for qb, k in zip(qbs, ks)]
    maxes = [jnp.maximum(jnp.maximum(jnp.max(s, axis=-1, keepdims=True), sn), sink) for s, sn in zip(scores, s_news)]
    exps = [jnp.exp(s - mx) for s, mx in zip(scores, maxes)]
    ex_news = [jnp.exp(sn - mx) for sn, mx in zip(s_news, maxes)]
    dens = [jnp.sum(ex, axis=-1, keepdims=True) + en + jnp.exp(sink - mx) for ex, en, mx in zip(exps, ex_news, maxes)]
    probs = [(ex / den).astype(BF16) for ex, den in zip(exps, dens)]
    p_news = [(en / den).astype(BF16).astype(F32) for en, den in zip(ex_news, dens)]
    outs = [_dot(p, cv_ref[i].astype(BF16)) + pn * v.astype(BF16).astype(F32)
            for i, p, pn, v in zip(rows, probs, p_news, vs)]

    ys = []
    for o in outs:
        pairs = []
        for j in range(ATT_HQ // 2):
            parts = []
            for par in range(2):
                h = 2 * j + par
                orow = o[h:h + 1, :]
                have_lo = (h // ATT_G) == 0
                want_lo = par == 0
                src = orow if want_lo == have_lo else pltpu.roll(orow, ATT_HD, 1)
                parts.append(jnp.where(lo if want_lo else jnp.logical_not(lo), src, 0.0))
            pairs.append(parts[0] + parts[1])
        ys.append(jnp.concatenate(pairs, axis=1))
    y_ref[...] = jnp.concatenate(ys, axis=0).astype(y_ref.dtype)


def _attn_sample_call(z_att, ck, cv, cos, sin, qg, kg, e, sink_b):
    nbatch = z_att.shape[0]
    nkv = ATT_HKV * ATT_HD
    rows = SAMPLE_ROWS
    const = lambda b: (0, 0)
    return pl.pallas_call(
        _attn_sample_body,
        out_shape=(_sds((nbatch, BRANCH_W), F32), _sds((nbatch, WINDOW, nkv), F32),
                   _sds((nbatch, WINDOW, nkv), F32)),
        grid=(nbatch // rows,),
        in_specs=[
            pl.BlockSpec((rows, W_ATT), lambda b: (b, 0)),
            pl.BlockSpec((rows, WINDOW, nkv), lambda b: (b, 0, 0)),
            pl.BlockSpec((rows, WINDOW, nkv), lambda b: (b, 0, 0)),
            pl.BlockSpec((1, LANES), const),
            pl.BlockSpec((1, LANES), const),
            pl.BlockSpec((1, BRANCH_W), const),
            pl.BlockSpec((1, nkv), const),
            pl.BlockSpec((BRANCH_W, BRANCH_W), const),
            pl.BlockSpec((ATT_HQ, LANES), const),
        ],
        out_specs=(
            pl.BlockSpec((rows, BRANCH_W), lambda b: (b, 0)),
            pl.BlockSpec((rows, WINDOW, nkv), lambda b: (b, 0, 0)),
            pl.BlockSpec((rows, WINDOW, nkv), lambda b: (b, 0, 0)),
        ),
        compiler_params=_params("parallel"),
        name="attn_sample",
    )(z_att, ck, cv, cos, sin, qg, kg, e, sink_b)


def _cmul(ar, ai, br, bi):
    return ar * br - ai * bi, ar * bi + ai * br


def _ssm_in(ub, bm_ref, j):
    blk = slice(j * SSM_BLOCK_CH, (j + 1) * SSM_BLOCK_CH)
    return _dot(ub[:, blk], bm_ref[blk, :])


def _ssm_out(s_re, s_im, cm_ref, j):
    return (_dot(s_re.astype(BF16), cm_ref[j * SSM_BLOCK_N:(j + 1) * SSM_BLOCK_N, :])
            + _dot(s_im.astype(BF16), cm_ref[SSM_N + j * SSM_BLOCK_N:SSM_N + (j + 1) * SSM_BLOCK_N, :]))


def _ssm_tail(y, u, d_ref, gw_ref, gb_ref):
    y = jax.nn.gelu(y + d_ref[...] * u)
    return y * jax.nn.sigmoid(_dot(y.astype(BF16), gw_ref[...]) + gb_ref[...])


def _ssm_prompt_body(u_ref, perm_ref, unperm_ref, bm_ref, cm_ref, a_ref, d_ref, gw_ref, gb_ref, ys_in_ref,
                     y_ref, sre_ref, sim_ref,
                     bu_ref, start_ref, carry_ref):
    del ys_in_ref
    tb = u_ref.shape[0]
    ts = tb // SSM_SEGS
    step_j = pl.program_id(1)

    @pl.when(step_j == 0)
    def _():
        carry_ref[...] = jnp.zeros_like(carry_ref)

    u = u_ref[...]
    u_hi, u_mid = _split2(u)
    u_lo = (u - u_hi.astype(F32) - u_mid.astype(F32)).astype(BF16)
    perm = perm_ref[...]
    up = _dot(perm, u_hi) + (_dot(perm, u_mid) + _dot(perm, u_lo))
    ub = up.astype(BF16)
    cw = SSM_BLOCK_N
    for j in range(SSM_BLOCKS):
        bu = _ssm_in(ub, bm_ref, j)
        bu_ref[:, j * cw:(j + 1) * cw] = bu[:, :cw]
        bu_ref[:, SSM_N + j * cw:SSM_N + (j + 1) * cw] = bu[:, cw:]

    def scan(c0, init_re, init_im, store):
        ar = jnp.broadcast_to(a_ref[0:1, c0:c0 + cw], (SUBLANES, cw))
        ai = jnp.broadcast_to(a_ref[1:2, c0:c0 + cw], (SUBLANES, cw))

        def step(t, st):
            sr, si = st
            r0 = pl.multiple_of(t * SUBLANES, SUBLANES)
            br = bu_ref[pl.ds(r0, SUBLANES), c0:c0 + cw]
            bi = bu_ref[pl.ds(r0, SUBLANES), SSM_N + c0:SSM_N + c0 + cw]
            pr, pi = _cmul(ar, ai, sr, si)
            sr, si = pr + br, pi + bi
            if store:
                bu_ref[pl.ds(r0, SUBLANES), c0:c0 + cw] = sr
                bu_ref[pl.ds(r0, SUBLANES), SSM_N + c0:SSM_N + c0 + cw] = si
            return sr, si

        return lax.fori_loop(0, ts, step, (init_re, init_im), unroll=2)

    zero = jnp.zeros((SUBLANES, cw), F32)
    ys = []
    for j in range(SSM_BLOCKS):
        c0 = j * cw
        er, ei = scan(c0, zero, zero, False)
        pr, pi = a_ref[0:1, c0:c0 + cw], a_ref[1:2, c0:c0 + cw]
        n = 1
        while n < ts:
            pr, pi = _cmul(pr, pi, pr, pi)
            n *= 2
        sr = carry_ref[0:1, c0:c0 + cw]
        si = carry_ref[1:2, c0:c0 + cw]
        for s in range(SSM_SEGS):
            start_ref[s:s + 1, c0:c0 + cw] = sr
            start_ref[s:s + 1, SSM_N + c0:SSM_N + c0 + cw] = si
            qr, qi = _cmul(pr, pi, sr, si)
            sr, si = qr + er[s:s + 1, :], qi + ei[s:s + 1, :]
        carry_ref[0:1, c0:c0 + cw] = sr
        carry_ref[1:2, c0:c0 + cw] = si
        scan(c0, start_ref[:, c0:c0 + cw], start_ref[:, SSM_N + c0:SSM_N + c0 + cw], True)
        ys.append(_ssm_out(bu_ref[:, c0:c0 + cw], bu_ref[:, SSM_N + c0:SSM_N + c0 + cw], cm_ref, j))

    sre_ref[0] = carry_ref[0:1, :]
    sim_ref[0] = carry_ref[1:2, :]
    y = _ssm_tail(jnp.concatenate(ys, axis=1), up, d_ref, gw_ref, gb_ref).astype(BF16)
    y_ref[...] = _dot(unperm_ref[...], y).astype(y_ref.dtype)


def _ssm_prompt_call(z_u, bm, cm, a, d, gw, gb, layer, ys, batch, seq, tb):
    nblk = seq // tb
    const = lambda b, j: (0, 0)
    of_layer = lambda b, j: (layer, 0, 0)
    ts = tb // SSM_SEGS
    src = (jnp.arange(tb) % SSM_SEGS) * ts + jnp.arange(tb) // SSM_SEGS
    perm = (src[:, None] == jnp.arange(tb)[None, :]).astype(BF16)
    return pl.pallas_call(
        _ssm_prompt_body,
        out_shape=(_sds(ys.shape, ys.dtype), _sds((batch, 1, SSM_N), F32), _sds((batch, 1, SSM_N), F32)),
        grid=(batch, nblk),
        in_specs=[
            pl.BlockSpec((tb, BRANCH_W), lambda b, j: (b * nblk + j, 0)),
            pl.BlockSpec((tb, tb), const),
            pl.BlockSpec((tb, tb), const),
            pl.BlockSpec((None, BRANCH_W, 2 * SSM_BLOCK_N), of_layer),
            pl.BlockSpec((None, 2 * SSM_N, SSM_BLOCK_CH), of_layer),
            pl.BlockSpec((2, SSM_N), const),
            pl.BlockSpec((1, BRANCH_W), const),
            pl.BlockSpec((None, BRANCH_W, BRANCH_W), of_layer),
            pl.BlockSpec((1, BRANCH_W), const),
            pl.BlockSpec(memory_space=pl.ANY),
        ],
        out_specs=(
            pl.BlockSpec((None, tb, BRANCH_W), lambda b, j: (1, b * nblk + j, 0)),
            pl.BlockSpec((1, 1, SSM_N), lambda b, j: (b, 0, 0)),
            pl.BlockSpec((1, 1, SSM_N), lambda b, j: (b, 0, 0)),
        ),
        input_output_aliases={9: 0},
        scratch_shapes=[
            pltpu.VMEM((tb, 2 * SSM_N), F32),
            pltpu.VMEM((SSM_SEGS, 2 * SSM_N), F32),
            pltpu.VMEM((2, SSM_N), F32),
        ],
        compiler_params=_params("arbitrary", "arbitrary"),
        name="ssm_prompt",
    )(z_u, perm, perm.T, bm, cm, a, d, gw, gb, ys)


def _ssm_sample_body(u_ref, s0r_ref, s0i_ref, bm_ref, cm_ref, a_ref, d_ref, gw_ref, gb_ref,
                     y_ref, sre_ref, sim_ref):
    u = u_ref[...]
    ub = u.astype(BF16)
    cw = SSM_BLOCK_N
    ys = []
    for j in range(SSM_BLOCKS):
        cols = slice(j * cw, (j + 1) * cw)
        bu = _ssm_in(ub, bm_ref, j)
        pr, pi = _cmul(a_ref[0:1, cols], a_ref[1:2, cols], s0r_ref[:, cols], s0i_ref[:, cols])
        sr = pr + bu[:, :cw]
        si = pi + bu[:, cw:]
        sre_ref[:, cols] = sr
        sim_ref[:, cols] = si
        ys.append(_ssm_out(sr, si, cm_ref, j))
    y_ref[...] = _ssm_tail(jnp.concatenate(ys, axis=1), u, d_ref, gw_ref, gb_ref).astype(y_ref.dtype)


def _ssm_sample_call(z_u, s0r, s0i, bm, cm, a, d, gw, gb, layer):
    nbatch = z_u.shape[0]
    const = lambda i: (0, 0)
    of_layer = lambda i: (layer, 0, 0)
    rows_w = pl.BlockSpec((nbatch, BRANCH_W), const)
    rows_n = pl.BlockSpec((nbatch, SSM_N), const)
    return pl.pallas_call(
        _ssm_sample_body,
        out_shape=(_sds((nbatch, BRANCH_W), F32), _sds((nbatch, SSM_N), F32), _sds((nbatch, SSM_N), F32)),
        grid=(1,),
        in_specs=[
            rows_w, rows_n, rows_n,
            pl.BlockSpec((None, BRANCH_W, 2 * SSM_BLOCK_N), of_layer),
            pl.BlockSpec((None, 2 * SSM_N, SSM_BLOCK_CH), of_layer),
            pl.BlockSpec((2, SSM_N), const),
            pl.BlockSpec((1, BRANCH_W), const),
            pl.BlockSpec((None, BRANCH_W, BRANCH_W), of_layer),
            pl.BlockSpec((1, BRANCH_W), const),
        ],
        out_specs=(rows_w, rows_n, rows_n),
        compiler_params=_params("arbitrary"),
        name="ssm_sample",
    )(z_u, s0r, s0i, bm, cm, a, d, gw, gb)


def _sg_prompt_body(z_ref, g_ref, w_ref, b_ref, ys_in_ref, y_ref, v_ref):
    del ys_in_ref
    nch = z_ref.shape[0] // CHUNK
    uv = jax.nn.gelu(z_ref[...])
    u = uv[:, :BRANCH_W]
    v = _rms(uv[:, BRANCH_W:], g_ref[...])
    v_ref[0] = v[(nch - 1) * CHUNK:, :]
    t = lax.broadcasted_iota(jnp.int32, (CHUNK, CHUNK), 0)
    s = lax.broadcasted_iota(jnp.int32, (CHUNK, CHUNK), 1)
    causal = t >= s
    vb = v.astype(BF16)
    ws = [jnp.where(causal, w_ref[g], 0.0).astype(BF16) for g in range(SG_GROUPS)]
    for ci in range(nch):
        rows = slice(ci * CHUNK, (ci + 1) * CHUNK)
        mixed = jnp.concatenate([_dot(ws[g], vb[rows, g * SG_GW:(g + 1) * SG_GW]) for g in range(SG_GROUPS)], axis=1)
        y_ref[rows, :] = (u[rows, :] * (mixed + b_ref[...])).astype(y_ref.dtype)


def _sg_prompt_call(z_c, g, w, b_exp, ys, batch, seq):
    rows = SG_STEP_CHUNKS * CHUNK if seq % (SG_STEP_CHUNKS * CHUNK) == 0 else CHUNK
    nblk = seq // rows
    return pl.pallas_call(
        _sg_prompt_body,
        out_shape=(_sds(ys.shape, ys.dtype), _sds((batch, CHUNK, BRANCH_W), F32)),
        grid=(batch, nblk),
        in_specs=[
            pl.BlockSpec((rows, W_SG), lambda b, i: (b * nblk + i, 0)),
            pl.BlockSpec((1, BRANCH_W), lambda b, i: (0, 0)),
            pl.BlockSpec((SG_GROUPS, CHUNK, CHUNK), lambda b, i: (0, 0, 0)),
            pl.BlockSpec((CHUNK, BRANCH_W), lambda b, i: (0, 0)),
            pl.BlockSpec(memory_space=pl.ANY),
        ],
        out_specs=(
            pl.BlockSpec((None, rows, BRANCH_W), lambda b, i: (2, b * nblk + i, 0)),
            pl.BlockSpec((1, CHUNK, BRANCH_W), lambda b, i: (b, 0, 0)),
        ),
        input_output_aliases={4: 0},
        compiler_params=_params("arbitrary", "arbitrary"),
        name="sg_prompt",
    )(z_c, g, w, b_exp, ys)


def _sg_sample_body(z_ref, g_ref, w00_ref, b_ref, y_ref, v_ref):
    uv = jax.nn.gelu(z_ref[...])
    u = uv[:, :BRANCH_W]
    v = _rms(uv[:, BRANCH_W:], g_ref[...])
    v_ref[...] = v
    w = w00_ref[...].astype(BF16).astype(F32)
    mixed = w * v.astype(BF16).astype(F32) + b_ref[0:1, :]
    y_ref[...] = (u * mixed).astype(y_ref.dtype)


def _sg_sample_call(z_c, g, w00, b_exp):
    nbatch = z_c.shape[0]
    return pl.pallas_call(
        _sg_sample_body,
        out_shape=(_sds((nbatch, BRANCH_W), F32), _sds((nbatch, BRANCH_W), F32)),
        name="sg_sample",
    )(z_c, g, w00, b_exp)


def _l2n(x):
    return x * lax.rsqrt(jnp.sum(x * x, axis=-1, keepdims=True) + EPS)


def _dn_gates(zab, alog_ref, dtb_ref):
    g = -jnp.exp(alog_ref[...]) * jax.nn.softplus(zab + dtb_ref[...])
    beta = jax.nn.sigmoid(zab)
    return g, beta


def _mm3(a, b):
    (ah, al), (bh, bl) = a, b
    return _dot(ah, bh) + (_dot(ah, bl) + _dot(al, bh))


def _unit_lower_inverse_off(mats, blk):
    n = mats[0].shape[0]
    r = lax.broadcasted_iota(jnp.int32, (n, n), 0)
    c = lax.broadcasted_iota(jnp.int32, (n, n), 1)

    def lower_left(size):
        return ((r // (2 * size)) == (c // (2 * size))) & ((r // size) % 2 == 1) & ((c // size) % 2 == 0)

    first = lower_left(1)
    offs = [-jnp.where(first, a, 0.0) for a in mats]
    size = 2
    while size < blk:
        mask = lower_left(size)
        ams = [jnp.where(mask, a, 0.0) for a in mats]
        obs = [off.astype(BF16) for off in offs]
        ps = [am + _dot(ob, am.astype(BF16)) for am, ob in zip(ams, obs)]
        offs = [off - (p + _dot(p.astype(BF16), ob)) for off, p, ob in zip(offs, ps, obs)]
        size *= 2
    return offs


def _head_lanes(x, rows_per_head):
    r = lax.broadcasted_iota(jnp.int32, x.shape, 0) // rows_per_head
    return jnp.concatenate([jnp.where(r == h, x, 0.0) for h in range(DN_H)], axis=1)


def _dn_chunk_terms(chunks):
    tc = DN_CHUNK
    n = DN_H * tc
    heads = range(DN_H)

    def stack(x, rows, off):
        return jnp.concatenate([x[rows, off + h * DN_HD:off + (h + 1) * DN_HD] for h in heads], axis=0)

    rows = [slice(r0, r0 + tc) for *_, r0 in chunks]
    qs = [_l2n(stack(ch[0], rw, 0)) * (DN_HD ** -0.5) for ch, rw in zip(chunks, rows)]
    ks = [_l2n(stack(ch[0], rw, BRANCH_W)) for ch, rw in zip(chunks, rows)]
    vs = [stack(ch[0], rw, 2 * BRANCH_W) for ch, rw in zip(chunks, rows)]
    zgss = [stack(ch[1], rw, 0) for ch, rw in zip(chunks, rows)]

    rt = lax.broadcasted_iota(jnp.int32, (tc, tc), 0)
    ct = lax.broadcasted_iota(jnp.int32, (tc, tc), 1)
    ones_tril = jnp.where(rt >= ct, 1.0, 0.0).astype(BF16)
    gparts = []
    for ch, rw in zip(chunks, rows):
        g = ch[2][rw]
        g_hi, g_lo = _split2(g)
        gparts.append((g_hi, g_lo, (g - g_hi.astype(F32) - g_lo.astype(F32)).astype(BF16)))
    gc_cs = [_dot(ones_tril, hi) + (_dot(ones_tril, lo) + _dot(ones_tril, lo2)) for hi, lo, lo2 in gparts]
    gcs = [jnp.concatenate([gc_c[:, h:h + 1] for h in heads], axis=0) for gc_c in gc_cs]
    betas = [jnp.concatenate([ch[3][rw, DN_H + h:DN_H + h + 1] for h in heads], axis=0)
             for ch, rw in zip(chunks, rows)]
    gc_lasts = [jnp.concatenate([jnp.broadcast_to(gc_c[tc - 1:tc, h:h + 1], (tc, 1)) for h in heads], axis=0)
                for gc_c in gc_cs]

    r = lax.broadcasted_iota(jnp.int32, (n, n), 0)
    c = lax.broadcasted_iota(jnp.int32, (n, n), 1)
    same = (r // tc) == (c // tc)
    tril = same & (r >= c)
    stril = same & (r > c)
    gmats = [jnp.broadcast_to(gc, (n, n)) for gc in gcs]
    decs = [jnp.where(tril, jnp.exp(jnp.where(tril, gm - gm.T, 0.0)), 0.0) for gm in gmats]
    kbs = [k.astype(BF16) for k in ks]
    kks = [_dot_nt(kb, kb) for kb in kbs]
    mats = [jnp.where(stril, beta * kk * dec, 0.0) for beta, kk, dec in zip(betas, kks, decs)]
    inv_offs = _unit_lower_inverse_off(mats, tc)

    egs = [jnp.exp(gc) for gc in gcs]
    rhss = [jnp.concatenate([beta * v, (beta * eg) * k], axis=1) for beta, eg, k, v in zip(betas, egs, ks, vs)]
    inv_parts = [_split2(x) for x in inv_offs]
    rhs_parts = [_split2(x) for x in rhss]
    sols = [rhs + _mm3(ip, rp) for rhs, ip, rp in zip(rhss, inv_parts, rhs_parts)]
    qks = [(_dot_nt(q.astype(BF16), kb) * dec).astype(BF16) for q, kb, dec in zip(qs, kbs, decs)]
    wqs = [jnp.concatenate([_head_lanes(sol[:, DN_HD:], tc), _head_lanes(q * eg, tc)], axis=0).astype(BF16)
           for sol, q, eg in zip(sols, qs, egs)]
    ke_ts = [_head_lanes(k * jnp.exp(gl - gc), tc).T.astype(BF16) for k, gl, gc in zip(ks, gc_lasts, gcs)]
    ges = [jnp.concatenate([jnp.broadcast_to(jnp.exp(gc_c[tc - 1:tc, h:h + 1]), (DN_HD, 1)) for h in heads], axis=0)
           for gc_c in gc_cs]
    return [(sol[:, :DN_HD], wq, qk, ke_t, ge, zgs)
            for sol, wq, qk, ke_t, ge, zgs in zip(sols, wqs, qks, ke_ts, ges, zgss)]


def _dn_chunk_steps(states, terms, ng):
    n = terms[0][0].shape[0]
    wss = [_dot(t[1], s.astype(BF16)) for s, t in zip(states, terms)]
    ubs = [(t[0] - ws[:n]).astype(BF16) for ws, t in zip(wss, terms)]
    outs = [ws[n:] + _dot(t[2], ub) for ws, ub, t in zip(wss, ubs, terms)]
    states = [t[4] * s + _dot(t[3], ub) for s, ub, t in zip(states, ubs, terms)]
    return states, [_rms(o, ng) * jax.nn.silu(t[5]) for o, t in zip(outs, terms)]


def _dn_prompt_body(x_ref, zg_ref, zab_ref, cw_ref, alog_ref, dtb_ref, ng_ref, ys_in_ref,
                    y_ref, so_ref, co_ref, xin_ref, s_ref):
    del ys_in_ref
    step = pl.program_id(0)
    nb, tb = x_ref.shape[0], x_ref.shape[1]
    tc = DN_CHUNK
    nci = tb // tc
    pad = SUBLANES

    @pl.when(step == 0)
    def _():
        xin_ref[:, 0:pad, :] = jnp.zeros((nb, pad, W_DN_QKV), F32)
        s_ref[...] = jnp.zeros_like(s_ref)

    @pl.when(step > 0)
    def _():
        xin_ref[:, 0:pad, :] = xin_ref[:, tb:tb + pad, :]

    chunks = []
    for b in range(nb):
        x = x_ref[b]
        xin_ref[b, pad:pad + tb, :] = x
        co_ref[b] = x_ref[b, tb - (DN_CONV - 1):tb, :]
        acc = cw_ref[DN_CONV - 1:DN_CONV, :] * x
        for i in range(DN_CONV - 1):
            lag = DN_CONV - 1 - i
            acc = acc + cw_ref[i:i + 1, :] * xin_ref[b, pad - lag:pad - lag + tb, :]
        qkv = jax.nn.silu(acc)
        g_all, beta_all = _dn_gates(zab_ref[b], alog_ref, dtb_ref)
        chunks += [(qkv, zg_ref[b], g_all, beta_all, ci * tc) for ci in range(nci)]
    terms = _dn_chunk_terms(chunks)

    states = [s_ref[b] for b in range(nb)]
    for ci in range(nci):
        states, outs = _dn_chunk_steps(states, [terms[b * nci + ci] for b in range(nb)], ng_ref[...])
        for b in range(nb):
            y_ref[b, ci * tc:(ci + 1) * tc, :] = jnp.concatenate(
                [outs[b][h * tc:(h + 1) * tc] for h in range(DN_H)], axis=1).astype(y_ref.dtype)
    for b in range(nb):
        s_ref[b] = states[b]

    @pl.when(step == pl.num_programs(0) - 1)
    def _():
        so_ref[...] = s_ref[...]


def _dn_prompt_call(z_dn, cw, alog, dtb, ng, ys, batch, seq):
    tb = DN_STEP_CHUNKS * DN_CHUNK if seq % (DN_STEP_CHUNKS * DN_CHUNK) == 0 else DN_CHUNK
    z3 = z_dn.reshape(batch, seq, W_DN)
    ys4 = ys.reshape(N_BRANCH, batch, seq, BRANCH_W)
    const = lambda i: (0, 0)
    whole = lambda i: (0, 0, 0)
    y, so, co = pl.pallas_call(
        _dn_prompt_body,
        out_shape=(_sds(ys4.shape, ys4.dtype), _sds((batch, DN_H * DN_HD, DN_HD), F32),
                   _sds((batch, DN_CONV - 1, W_DN_QKV), F32)),
        grid=(seq // tb,),
        in_specs=[
            pl.BlockSpec((batch, tb, W_DN_QKV), lambda i: (0, i, 0)),
            pl.BlockSpec((batch, tb, BRANCH_W), lambda i: (0, i, W_DN_QKV // BRANCH_W)),
            pl.BlockSpec((batch, tb, LANES), lambda i: (0, i, (W_DN_QKV + BRANCH_W) // LANES)),
            pl.BlockSpec((DN_CONV, W_DN_QKV), const),
            pl.BlockSpec((1, LANES), const),
            pl.BlockSpec((1, LANES), const),
            pl.BlockSpec((1, DN_HD), const),
            pl.BlockSpec(memory_space=pl.ANY),
        ],
        out_specs=(
            pl.BlockSpec((None, batch, tb, BRANCH_W), lambda i: (N_BRANCH - 1, 0, i, 0)),
            pl.BlockSpec((batch, DN_H * DN_HD, DN_HD), whole),
            pl.BlockSpec((batch, DN_CONV - 1, W_DN_QKV), whole),
        ),
        input_output_aliases={7: 0},
        scratch_shapes=[pltpu.VMEM((batch, tb + SUBLANES, W_DN_QKV), F32),
                        pltpu.VMEM((batch, DN_H * DN_HD, DN_HD), F32)],
        compiler_params=_params("arbitrary"),
        name="dn_prompt",
    )(z3, z3, z3, cw, alog, dtb, ng, ys4)
    return y.reshape(ys.shape), so.reshape(batch, DN_H, DN_HD, DN_HD), co


def _dn_sample_body(x_ref, zg_ref, zab_ref, conv_ref, s0_ref, cw_ref, alog_ref, dtb_ref, ng_ref, *rest):
    y_ref, so_ref, co_ref = rest[-3:]
    nrow = x_ref.shape[0]
    x_all = x_ref[...]
    acc = cw_ref[DN_CONV - 1:DN_CONV, :] * x_all
    for j in range(DN_CONV - 1):
        acc = acc + cw_ref[j:j + 1, :] * conv_ref[:, j, :]
    qkv_all = jax.nn.silu(acc)
    g_all, beta_all = _dn_gates(zab_ref[...], alog_ref, dtb_ref)
    zg_all = zg_ref[...]
    row8 = lax.broadcasted_iota(jnp.int32, (SUBLANES, DN_HD), 0)
    rows = range(nrow)
    heads = range(DN_H)
    for i in rows:
        co_ref[i, 0:DN_CONV - 2, :] = conv_ref[i, 1:DN_CONV - 1, :]
        co_ref[i, DN_CONV - 2:DN_CONV - 1, :] = x_all[i:i + 1, :]

    qs = [_l2n(qkv_all[:, h * DN_HD:(h + 1) * DN_HD]) * (DN_HD ** -0.5) for h in heads]
    ks = [_l2n(qkv_all[:, BRANCH_W + h * DN_HD:BRANCH_W + (h + 1) * DN_HD]) for h in heads]
    vs = [qkv_all[:, 2 * BRANCH_W + h * DN_HD:2 * BRANCH_W + (h + 1) * DN_HD] for h in heads]
    egs = [jnp.exp(g_all[:, h:h + 1]) for h in heads]
    betas = [beta_all[:, DN_H + h:DN_H + h + 1] for h in heads]
    w_ks = [(beta * eg) * k for beta, eg, k in zip(betas, egs, ks)]
    q_gs = [q * eg for q, eg in zip(qs, egs)]
    qks = [jnp.sum(q.astype(BF16).astype(F32) * k.astype(BF16).astype(F32), axis=-1, keepdims=True)
           for q, k in zip(qs, ks)]
    bvs = [beta * v for beta, v in zip(betas, vs)]

    pairs = [(i, h) for i in rows for h in heads]
    lhs = [jnp.where(row8 == 0, jnp.broadcast_to(w_ks[h][i:i + 1, :], (SUBLANES, DN_HD)),
                     jnp.where(row8 == 1, jnp.broadcast_to(q_gs[h][i:i + 1, :], (SUBLANES, DN_HD)), 0.0)).astype(BF16)
           for i, h in pairs]
    prods = [_dot(l, s0_ref[i, h].astype(BF16)) for l, (i, h) in zip(lhs, pairs)]
    us = [bvs[h][i:i + 1, :] - p[0:1, :] for p, (i, h) in zip(prods, pairs)]
    outs = [p[1:2, :] + qks[h][i:i + 1, :] * u for p, u, (i, h) in zip(prods, us, pairs)]
    kcols = []
    for i in rows:
        k8 = jnp.zeros((SUBLANES, DN_HD), F32)
        for h in heads:
            k8 = jnp.where(row8 == h, jnp.broadcast_to(ks[h][i:i + 1, :], (SUBLANES, DN_HD)), k8)
        kcols.append(k8.T)
    for u, (i, h) in zip(us, pairs):
        so_ref[i, h] = egs[h][i:i + 1, :] * s0_ref[i, h] + kcols[i][:, h:h + 1] * u

    ys = []
    for h in heads:
        o_h = jnp.concatenate([outs[i * DN_H + h] for i in rows], axis=0)
        ys.append(_rms(o_h, ng_ref[...]) * jax.nn.silu(zg_all[:, h * DN_HD:(h + 1) * DN_HD]))
    y_ref[...] = jnp.concatenate(ys, axis=1).astype(y_ref.dtype)


def _dn_sample_call(z_dn, conv0, s0_all, states_out, layer, cw, alog, dtb, ng):
    nbatch = z_dn.shape[0]
    rows = SAMPLE_ROWS
    const = lambda b: (0, 0)
    state_spec = pl.BlockSpec((None, rows, DN_H, DN_HD, DN_HD), lambda b: (layer, b, 0, 0, 0))
    chained = states_out is not None
    return pl.pallas_call(
        _dn_sample_body,
        out_shape=(_sds((nbatch, BRANCH_W), F32), _sds(s0_all.shape, F32),
                   _sds((nbatch, DN_CONV - 1, W_DN_QKV), F32)),
        grid=(nbatch // rows,),
        in_specs=[
            pl.BlockSpec((rows, W_DN_QKV), lambda b: (b, 0)),
            pl.BlockSpec((rows, BRANCH_W), lambda b: (b, W_DN_QKV // BRANCH_W)),
            pl.BlockSpec((rows, LANES), lambda b: (b, (W_DN_QKV + BRANCH_W) // LANES)),
            pl.BlockSpec((rows, DN_CONV - 1, W_DN_QKV), lambda b: (b, 0, 0)),
            state_spec,
            pl.BlockSpec((DN_CONV, W_DN_QKV), const),
            pl.BlockSpec((1, LANES), const),
            pl.BlockSpec((1, LANES), const),
            pl.BlockSpec((1, DN_HD), const),
        ] + ([pl.BlockSpec(memory_space=pl.ANY)] if chained else []),
        out_specs=(
            pl.BlockSpec((rows, BRANCH_W), lambda b: (b, 0)),
            state_spec,
            pl.BlockSpec((rows, DN_CONV - 1, W_DN_QKV), lambda b: (b, 0, 0)),
        ),
        input_output_aliases={9: 1} if chained else {},
        compiler_params=_params("parallel"),
        name="dn_sample",
    )(z_dn, z_dn, z_dn, conv0, s0_all, cw, alog, dtb, ng, *([states_out] if chained else []))


def _rope_tables(pos):
    half = ATT_HD // 2
    inv = ROPE_THETA ** (-jnp.arange(half, dtype=F32) / half)
    ang = pos.astype(F32)[:, None] * inv[None, :]
    cos, sin = jnp.cos(ang), jnp.sin(ang)
    reps = LANES // ATT_HD
    return (jnp.tile(jnp.concatenate([cos, cos], axis=1), (1, reps)),
            jnp.tile(jnp.concatenate([-sin, sin], axis=1), (1, reps)))


def _ssm_matrices(lam_re, lam_im, log_dt, b_re, b_im, c_re, c_im):
    dt = jnp.exp(log_dt)[:, None]
    mag = jnp.exp(lam_re * dt)
    a_re, a_im = mag * jnp.cos(lam_im * dt), mag * jnp.sin(lam_im * dt)
    den = lam_re * lam_re + lam_im * lam_im
    f_re = ((a_re - 1.0) * lam_re + a_im * lam_im) / den
    f_im = (a_im * lam_re - (a_re - 1.0) * lam_im) / den
    bb_re = f_re[..., None] * b_re - f_im[..., None] * b_im
    bb_im = f_re[..., None] * b_im + f_im[..., None] * b_re
    gpb = SSM_NG // SSM_BLOCKS
    eye = jnp.eye(gpb, dtype=F32)

    def in_layout(b):
        b = b.reshape(SSM_BLOCKS, gpb, SSM_P, SSM_GROUP)
        return jnp.einsum("jipc,ik->jickp", b, eye).reshape(BRANCH_W, SSM_BLOCK_N)

    def out_layout(c):
        c = c.reshape(SSM_BLOCKS, gpb, SSM_GROUP, SSM_P)
        return jnp.einsum("jicp,ik->jipkc", c, eye).reshape(SSM_N, SSM_BLOCK_CH)

    bm = jnp.concatenate([in_layout(bb_re), in_layout(bb_im)], axis=1).astype(BF16)
    cm = jnp.concatenate([out_layout(c_re), out_layout(-c_im)], axis=0).astype(BF16)
    a = jnp.stack([a_re.reshape(SSM_N), a_im.reshape(SSM_N)], axis=0)
    return bm, cm, a


def _w_main_body(w_ref, o_ref):
    rb = o_ref.shape[0]
    row = pl.program_id(1) * rb + lax.broadcasted_iota(jnp.int32, (rb, 1), 0)
    o_ref[...] = jnp.where(row < N_MAIN, w_ref[...], 0.0).astype(o_ref.dtype)


def _w_gate_body(w_ref, buf_ref, o_ref):
    del buf_ref
    o_ref[...] = w_ref[0].astype(o_ref.dtype)


def _w_in_layout_call(w_t):
    depth, n_in, d = w_t.shape
    rows = GATE_COL0 + N_BRANCH * d
    rb_main = W_MAIN // 7
    assert W_MAIN % rb_main == 0 and rb_main % (2 * SUBLANES) == 0 and n_in - N_MAIN == N_BRANCH * d
    buf = pl.pallas_call(
        _w_main_body,
        out_shape=_sds((depth, rows, d), BF16),
        grid=(depth, W_MAIN // rb_main),
        in_specs=[pl.BlockSpec((None, rb_main, d), lambda l, i: (l, i, 0))],
        out_specs=pl.BlockSpec((None, rb_main, d), lambda l, i: (l, i, 0)),
        compiler_params=_params("parallel", "parallel"),
        name="w_main_layout",
    )(w_t)
    rb = 512
    return pl.pallas_call(
        _w_gate_body,
        out_shape=_sds((depth, rows, d), BF16),
        grid=(depth, N_BRANCH * d // rb),
        in_specs=[
            pl.BlockSpec((pl.Element(1), pl.Element(rb), pl.Element(d)),
                         lambda l, i: (l, pl.multiple_of(N_MAIN + i * rb, SUBLANES), 0)),
            pl.BlockSpec(memory_space=pl.ANY),
        ],
        out_specs=pl.BlockSpec((None, rb, d), lambda l, i: (l, GATE_COL0 // rb + i, 0)),
        input_output_aliases={1: 0},
        compiler_params=_params("parallel", "parallel"),
        name="w_gate_layout",
    )(w_t, buf)


def _lane_rows(x):
    return jnp.pad(x, ((0, 0), (0, LANES - x.shape[1])))[:, None, :]


def _prepare(p):
    depth = p["w_in"].shape[0]
    w_in = p["w_in"]
    bm, cm, a = jax.vmap(_ssm_matrices)(p["ssm_lam_re"], p["ssm_lam_im"], p["ssm_log_dt"], p["ssm_b_re"],
                                        p["ssm_b_im"], p["ssm_c_re"], p["ssm_c_im"])
    w_all = _w_in_layout_call(jnp.swapaxes(w_in, 1, 2))
    stacked = dict(
        w_all=w_all, w_branch=p["w_branch"].astype(BF16), w_out=p["w_out"].astype(BF16),
        w_ff1=p["w_ff1"].astype(BF16), w_ff2=p["w_ff2"].astype(BF16),
        bm=bm, cm=cm, glu_w=p["ssm_glu_w"].astype(BF16),
    )
    per_layer = dict(
        norm1_g=p["norm1_g"][:, None, :], norm2_g=p["norm2_g"][:, None, :],
        qg=jnp.tile(p["att_qn_g"], (1, ATT_HQ))[:, None, :],
        kg=jnp.tile(p["att_kn_g"], (1, ATT_HKV))[:, None, :],
        sink=p["att_sink"],
        sink_b=jnp.broadcast_to(p["att_sink"][:, :, None], (depth, ATT_HQ, LANES)),
        a=a, ssm_d=p["ssm_d"][:, None, :], glu_b=p["ssm_glu_b"][:, None, :],
        sg_g=p["sg_norm_g"][:, None, :], sg_w=p["sg_w"],
        sg_b=jnp.repeat(jnp.swapaxes(p["sg_b"], 1, 2), SG_GW, axis=2),
        sg_w00=jnp.repeat(p["sg_w"][:, :, 0, 0], SG_GW, axis=1)[:, None, :],
        cw=p["dn_conv_w"], alog=_lane_rows(p["dn_a_log"]), dtb=_lane_rows(p["dn_dt_bias"]),
        ng=p["dn_norm_g"][:, None, :],
    )
    return stacked, per_layer


def _mix_and_ffn(x, h, ys, wts, layer, lp, g_next, tm):
    mix = _merge_call(h, ys, wts["w_all"], wts["w_branch"], layer, tm)
    x, h2 = _proj_call(mix, wts["w_out"], layer, x, lp["norm2_g"], tm)
    return _ffn_call(h2, wts["w_ff1"], wts["w_ff2"], layer, x, g_next, tm, FFN_TF)


def _layer(xp, hp, xs, hs, st, wts, layer, lp, g_next, rope_p, rope_s, e, batch, seq):
    tmp = ROWS_PROMPT if xp.shape[0] % ROWS_PROMPT == 0 else xp.shape[0]
    tms = xs.shape[0]
    ck, cv, s0r, s0i, s0d_all, sdelta, conv0 = st
    ssm_w = (wts["bm"], wts["cm"], lp["a"], lp["ssm_d"], wts["glu_w"], lp["glu_b"], layer)

    z_att, z_ssm, z_sg, z_dn = _in_proj_call(hp, wts["w_all"], layer, tmp)
    ys, pk, pv = _attn_prompt_call(z_att, rope_p[0], rope_p[1], lp["qg"], lp["kg"], e, lp["sink"], batch, seq)
    tb = 512 if seq % 512 == 0 else seq
    ys, pre, pim = _ssm_prompt_call(z_ssm, *ssm_w, ys, batch, seq, tb)
    ys, pchunk = _sg_prompt_call(z_sg, lp["sg_g"], lp["sg_w"], lp["sg_b"], ys, batch, seq)
    ys, pdelta, pconv = _dn_prompt_call(z_dn, lp["cw"], lp["alog"], lp["dtb"], lp["ng"], ys, batch, seq)
    xp, hp = _mix_and_ffn(xp, hp, ys, wts, layer, lp, g_next, tmp)

    z_att, z_ssm, z_sg, z_dn = _in_proj_call(hs, wts["w_all"], layer, tms)
    ya, sk, sv = _attn_sample_call(z_att, ck, cv, rope_s[0], rope_s[1], lp["qg"], lp["kg"], e, lp["sink_b"])
    yb, sre, sim = _ssm_sample_call(z_ssm, s0r, s0i, *ssm_w)
    yc, schunk = _sg_sample_call(z_sg, lp["sg_g"], lp["sg_w00"], lp["sg_b"])
    yd, sdelta, sconv = _dn_sample_call(z_dn, conv0, s0d_all, sdelta, layer, lp["cw"], lp["alog"], lp["dtb"], lp["ng"])
    xs, hs = _mix_and_ffn(xs, hs, jnp.stack([ya, yb, yc, yd]).astype(BF16), wts, layer, lp, g_next, tms)

    new_p = (pk, pv, pre, pim, pdelta, pconv, pchunk)
    new_s = (sk, sv, sre, sim, sdelta, sconv, schunk)
    return xp, hp, xs, hs, new_p, new_s


def kernel(x_prompt, x_sample, cache_k, cache_v, state_ssm_re, state_ssm_im, state_delta, state_conv, norm1_g, w_in, att_qn_g, att_kn_g, att_sink, ssm_lam_re, ssm_lam_im, ssm_log_dt, ssm_b_re, ssm_b_im, ssm_c_re, ssm_c_im, ssm_d, ssm_glu_w, ssm_glu_b, sg_norm_g, sg_w, sg_b, dn_conv_w, dn_a_log, dn_dt_bias, dn_norm_g, w_branch, w_out, norm2_g, w_ff1, w_ff2):
    p = dict(norm1_g=norm1_g, w_in=w_in, att_qn_g=att_qn_g, att_kn_g=att_kn_g, att_sink=att_sink,
             ssm_lam_re=ssm_lam_re, ssm_lam_im=ssm_lam_im, ssm_log_dt=ssm_log_dt, ssm_b_re=ssm_b_re,
             ssm_b_im=ssm_b_im, ssm_c_re=ssm_c_re, ssm_c_im=ssm_c_im, ssm_d=ssm_d, ssm_glu_w=ssm_glu_w,
             ssm_glu_b=ssm_glu_b, sg_norm_g=sg_norm_g, sg_w=sg_w, sg_b=sg_b, dn_conv_w=dn_conv_w, dn_a_log=dn_a_log,
             dn_dt_bias=dn_dt_bias, dn_norm_g=dn_norm_g, w_branch=w_branch, w_out=w_out, norm2_g=norm2_g,
             w_ff1=w_ff1, w_ff2=w_ff2)
    depth = w_in.shape[0]
    batch, seq, d = x_prompt.shape
    nsamp = x_sample.shape[0]
    past = cache_k.shape[2]
    assert x_sample.shape[1] == 1 and past == WINDOW and seq % CHUNK == 0 and d == D_MODEL
    assert nsamp % SAMPLE_ROWS == 0
    nkv = ATT_HKV * ATT_HD

    xp = x_prompt.reshape(batch * seq, d)
    xs = x_sample.reshape(nsamp, d)
    rope_p = _rope_tables(jnp.arange(seq))
    rope_s = _rope_tables(PAST_LEN + jnp.arange(1))
    lane = jnp.arange(BRANCH_W) // ATT_HD
    e = (lane[:, None] == lane[None, :]).astype(BF16)

    wts, prm = _prepare(p)
    ck_all = cache_k.reshape(depth, nsamp, past, nkv)
    cv_all = cache_v.reshape(depth, nsamp, past, nkv)
    s0r_all = state_ssm_re.reshape(depth, nsamp, SSM_N)
    s0i_all = state_ssm_im.reshape(depth, nsamp, SSM_N)

    hp = _rmsnorm_call(xp, prm["norm1_g"][0], ROWS_PROMPT if xp.shape[0] % ROWS_PROMPT == 0 else xp.shape[0])
    hs = _rmsnorm_call(xs, prm["norm1_g"][0], nsamp)
    new_p = [[] for _ in range(7)]
    new_s = [[] for _ in range(7)]
    s_delta = None
    for l in range(depth):
        lp = {name: v[l] for name, v in prm.items()}
        st = (ck_all[l], cv_all[l], s0r_all[l], s0i_all[l], state_delta, s_delta, state_conv[l])
        g_next = prm["norm1_g"][l + 1] if l + 1 < depth else jnp.ones((1, d), F32)
        xp, hp, xs, hs, sp, ss = _layer(xp, hp, xs, hs, st, wts, l, lp, g_next, rope_p, rope_s, e, batch, seq)
        s_delta = ss[4]
        for i in range(7):
            new_p[i].append(sp[i])
            new_s[i].append(ss[i])

    def stack(parts, shape):
        return jnp.stack(parts, axis=0).reshape((depth,) + shape)

    keep = min(WINDOW, seq)
    return (
        xp.reshape(batch, seq, d), xs.reshape(nsamp, 1, d),
        stack(new_p[0], (batch, keep, ATT_HKV, ATT_HD)), stack(new_p[1], (batch, keep, ATT_HKV, ATT_HD)),
        stack(new_p[2], (batch, SSM_NG, SSM_P)), stack(new_p[3], (batch, SSM_NG, SSM_P)),
        stack(new_p[4], (batch, DN_H, DN_HD, DN_HD)), stack(new_p[5], (batch, DN_CONV - 1, W_DN_QKV)),
        stack(new_p[6], (batch, CHUNK, BRANCH_W)),
        stack(new_s[0], (nsamp, past, ATT_HKV, ATT_HD)), stack(new_s[1], (nsamp, past, ATT_HKV, ATT_HD)),
        stack(new_s[2], (nsamp, SSM_NG, SSM_P)), stack(new_s[3], (nsamp, SSM_NG, SSM_P)),
        s_delta, stack(new_s[5], (nsamp, DN_CONV - 1, W_DN_QKV)),
        stack(new_s[6], (nsamp, 1, BRANCH_W)),
    )
```

```python
import jax
import jax.numpy as jnp
from jax import lax
from jax.experimental import pallas as pl
from jax.experimental.pallas import tpu as pltpu

F32 = jnp.float32
BF16 = jnp.bfloat16

D_MODEL = 2048
PAST_LEN = 16384
N_BRANCH = 4
BRANCH_W = D_MODEL // N_BRANCH
ATT_HD = 64
ATT_HQ = BRANCH_W // ATT_HD
ATT_HKV = 2
ATT_G = ATT_HQ // ATT_HKV
WINDOW = 128
ROPE_THETA = 10000.0
SSM_GROUP = 16
SSM_NG = BRANCH_W // SSM_GROUP
SSM_P = 64
SSM_N = SSM_NG * SSM_P
SSM_SEGS = 8
CHUNK = 128
SG_GROUPS = 4
SG_GW = BRANCH_W // SG_GROUPS
DN_HD = 128
DN_H = BRANCH_W // DN_HD
DN_CONV = 4
DN_CHUNK = 64
DN_STEP_CHUNKS = 2
D_FF = 4 * D_MODEL
EPS = 1e-6
NEG_INF = -1e30

LANES = 128
SUBLANES = 8
V7X_VMEM_LIMIT = 56 * 1024 * 1024

W_ATT = ATT_HQ * ATT_HD + 2 * ATT_HKV * ATT_HD
W_SSM = BRANCH_W
W_SG = 2 * BRANCH_W
W_DN_QKV = 3 * BRANCH_W
W_DN = W_DN_QKV + BRANCH_W + LANES
N_MAIN = W_ATT + W_SSM + W_SG + W_DN_QKV + BRANCH_W + 2 * DN_H
W_MAIN = W_ATT + W_SSM + W_SG + W_DN
GATE_COL0 = 3 * D_MODEL

SSM_BLOCK_CH = LANES
SSM_BLOCKS = BRANCH_W // SSM_BLOCK_CH
SSM_BLOCK_N = SSM_N // SSM_BLOCKS

ROWS_PROMPT = 512
FFN_TF = 1024
SAMPLE_ROWS = 8


def _params(*sem):
    return pltpu.CompilerParams(dimension_semantics=sem, vmem_limit_bytes=V7X_VMEM_LIMIT)


def _sds(shape, dtype):
    return jax.ShapeDtypeStruct(shape, dtype)


def _dot(a, b, precision=None):
    return jnp.dot(a, b, preferred_element_type=F32, precision=precision)


def _dot_nt(a, b, precision=None):
    return lax.dot_general(a, b, (((1,), (1,)), ((), ())), preferred_element_type=F32, precision=precision)


def _rms(x, g):
    ms = jnp.mean(x * x, axis=-1, keepdims=True)
    return x * lax.rsqrt(ms + EPS) * g


def _split2(x):
    hi = x.astype(BF16)
    return hi, (x - hi.astype(F32)).astype(BF16)


def _split_dot(x, e):
    hi, lo = _split2(x)
    return _dot(hi, e) + _dot(lo, e)


def _rmsnorm_body(x_ref, g_ref, o_ref):
    o_ref[...] = _rms(x_ref[...], g_ref[...]).astype(o_ref.dtype)


def _rmsnorm_call(x, g, tm):
    m, d = x.shape
    return pl.pallas_call(
        _rmsnorm_body,
        out_shape=_sds((m, d), BF16),
        grid=(m // tm,),
        in_specs=[pl.BlockSpec((tm, d), lambda i: (i, 0)), pl.BlockSpec((1, d), lambda i: (0, 0))],
        out_specs=pl.BlockSpec((tm, d), lambda i: (i, 0)),
        compiler_params=_params("parallel"),
        name="rmsnorm",
    )(x, g)


def _in_proj_body(h_ref, w_ref, *out_refs):
    h = h_ref[...]
    off = 0
    for o_ref in out_refs:
        width = o_ref.shape[1]
        o_ref[...] = _dot_nt(h, w_ref[off:off + width, :])
        off += width


def _in_proj_call(h, w_all, layer, tm):
    m, k = h.shape
    widths = (W_ATT, W_SSM, W_SG, W_DN)
    row = lambda i: (i, 0)
    return pl.pallas_call(
        _in_proj_body,
        out_shape=tuple(_sds((m, n), F32) for n in widths),
        grid=(m // tm,),
        in_specs=[pl.BlockSpec((tm, k), row),
                  pl.BlockSpec((None, W_MAIN, k), lambda i: (layer, 0, 0), pipeline_mode=pl.Buffered(1))],
        out_specs=tuple(pl.BlockSpec((tm, n), row) for n in widths),
        compiler_params=_params("parallel"),
        name="in_proj",
    )(h, w_all)


def _in_proj_sg_body(h_ref, w_ref, g_ref, sgw_ref, sgb_ref, za_ref, zb_ref, zd_ref, y_ref, v_ref):
    h = h_ref[...]
    za_ref[...] = _dot_nt(h, w_ref[0:W_ATT, :])
    zb_ref[...] = _dot_nt(h, w_ref[W_ATT:W_ATT + W_SSM, :])
    zd_ref[...] = _dot_nt(h, w_ref[W_ATT + W_SSM + W_SG:W_MAIN, :])
    y, v_last = _spatial_gate(_dot_nt(h, w_ref[W_ATT + W_SSM:W_ATT + W_SSM + W_SG, :]), g_ref, sgw_ref, sgb_ref)
    y_ref[...] = y.astype(y_ref.dtype)
    v_ref[0] = v_last


def _in_proj_sg_call(h, w_all, layer, sg_g, sg_w, sg_b, batch, seq, tm):
    m, k = h.shape
    assert seq % tm == 0 and tm % CHUNK == 0
    tiles = seq // tm
    row = lambda i: (i, 0)
    const = lambda i: (0, 0)
    return pl.pallas_call(
        _in_proj_sg_body,
        out_shape=(_sds((m, W_ATT), F32), _sds((m, W_SSM), F32), _sds((m, W_DN), F32),
                   _sds((N_BRANCH, m, BRANCH_W), BF16), _sds((batch, CHUNK, BRANCH_W), F32)),
        grid=(m // tm,),
        in_specs=[pl.BlockSpec((tm, k), row),
                  pl.BlockSpec((None, W_MAIN, k), lambda i: (layer, 0, 0), pipeline_mode=pl.Buffered(1)),
                  pl.BlockSpec((1, BRANCH_W), const),
                  pl.BlockSpec((SG_GROUPS, CHUNK, CHUNK), lambda i: (0, 0, 0)),
                  pl.BlockSpec((CHUNK, BRANCH_W), const)],
        out_specs=(pl.BlockSpec((tm, W_ATT), row), pl.BlockSpec((tm, W_SSM), row), pl.BlockSpec((tm, W_DN), row),
                   pl.BlockSpec((None, tm, BRANCH_W), lambda i: (2, i, 0)),
                   pl.BlockSpec((1, CHUNK, BRANCH_W), lambda i: (i // tiles, 0, 0))),
        compiler_params=_params("arbitrary"),
        name="in_proj_sg",
    )(h, w_all, sg_g, sg_w, sg_b)


def _merge_body(h_ref, y_ref, wg_ref, wb_ref, o_ref, acc_ref):
    n = pl.program_id(1)

    @pl.when(n == 0)
    def _():
        acc_ref[...] = jnp.zeros_like(acc_ref)

    gate = jax.nn.sigmoid(_dot_nt(h_ref[...], wg_ref[...]))
    acc_ref[...] += gate * _dot(y_ref[...], wb_ref[...])

    @pl.when(n == N_BRANCH - 1)
    def _():
        o_ref[...] = acc_ref[...].astype(o_ref.dtype)


def _merge_call(h, ys, w_all, wb, layer, tm):
    m, d = h.shape
    gate0 = GATE_COL0 // d
    return pl.pallas_call(
        _merge_body,
        out_shape=_sds((m, d), BF16),
        grid=(m // tm, N_BRANCH),
        in_specs=[
            pl.BlockSpec((tm, d), lambda i, n: (i, 0)),
            pl.BlockSpec((None, tm, BRANCH_W), lambda i, n: (n, i, 0)),
            pl.BlockSpec((None, d, d), lambda i, n: (layer, gate0 + n, 0)),
            pl.BlockSpec((None, None, BRANCH_W, d), lambda i, n: (layer, n, 0, 0)),
        ],
        out_specs=pl.BlockSpec((tm, d), lambda i, n: (i, 0)),
        scratch_shapes=[pltpu.VMEM((tm, d), F32)],
        compiler_params=_params("parallel", "arbitrary"),
        name="merge",
    )(h, ys, w_all, wb)


def _proj_body(m_ref, w_ref, x_ref, g_ref, xo_ref, ho_ref):
    xn = x_ref[...] + _dot(m_ref[...], w_ref[...])
    xo_ref[...] = xn
    ho_ref[...] = _rms(xn, g_ref[...]).astype(ho_ref.dtype)


def _proj_call(mix, w, layer, x, g, tm):
    m, d = x.shape
    return pl.pallas_call(
        _proj_body,
        out_shape=(_sds((m, d), F32), _sds((m, d), BF16)),
        grid=(m // tm,),
        in_specs=[
            pl.BlockSpec((tm, d), lambda i: (i, 0)),
            pl.BlockSpec((None, d, d), lambda i: (layer, 0, 0)),
            pl.BlockSpec((tm, d), lambda i: (i, 0)),
            pl.BlockSpec((1, d), lambda i: (0, 0)),
        ],
        out_specs=(pl.BlockSpec((tm, d), lambda i: (i, 0)), pl.BlockSpec((tm, d), lambda i: (i, 0))),
        compiler_params=_params("parallel"),
        name="out_proj",
    )(mix, w, x, g)


def _ffn_body(h_ref, w1_ref, w2_ref, x_ref, g_ref, xo_ref, ho_ref):
    f = pl.program_id(1)

    @pl.when(f == 0)
    def _():
        xo_ref[...] = x_ref[...]

    a = _dot(h_ref[...], w1_ref[...])
    a = jnp.square(jnp.maximum(a, 0.0)).astype(BF16)
    xo_ref[...] += _dot(a, w2_ref[...])

    @pl.when(f == pl.num_programs(1) - 1)
    def _():
        ho_ref[...] = _rms(xo_ref[...], g_ref[...]).astype(ho_ref.dtype)


def _ffn_call(h, w1, w2, layer, x, g_next, tm, tf):
    m, d = x.shape
    dff = w1.shape[2]
    return pl.pallas_call(
        _ffn_body,
        out_shape=(_sds((m, d), F32), _sds((m, d), BF16)),
        grid=(m // tm, dff // tf),
        in_specs=[
            pl.BlockSpec((tm, d), lambda i, f: (i, 0)),
            pl.BlockSpec((None, d, tf), lambda i, f: (layer, 0, f)),
            pl.BlockSpec((None, tf, d), lambda i, f: (layer, f, 0)),
            pl.BlockSpec((tm, d), lambda i, f: (i, 0)),
            pl.BlockSpec((1, d), lambda i, f: (0, 0)),
        ],
        out_specs=(pl.BlockSpec((tm, d), lambda i, f: (i, 0)), pl.BlockSpec((tm, d), lambda i, f: (i, 0))),
        compiler_params=_params("parallel", "arbitrary"),
        name="ffn",
    )(h, w1, w2, x, g_next)


def _head_norm_rope(x, g, cos, sin, e):
    w = x.shape[1]
    ss = _split_dot(x * x, e)
    xn = x * lax.rsqrt(ss * (1.0 / ATT_HD) + EPS) * g
    lane = lax.broadcasted_iota(jnp.int32, x.shape, 1)
    first_half = (lane % ATT_HD) < (ATT_HD // 2)
    rot = jnp.where(first_half, pltpu.roll(xn, w - ATT_HD // 2, 1), pltpu.roll(xn, ATT_HD // 2, 1))
    return xn * cos + rot * sin


def _qkv_norm_rope(z, cos, sin, qg_ref, kg_ref, e_ref):
    nq = ATT_HQ * ATT_HD
    nk = ATT_HKV * ATT_HD
    rep = nq // LANES
    q = _head_norm_rope(z[:, :nq], qg_ref[...], jnp.tile(cos, (1, rep)), jnp.tile(sin, (1, rep)), e_ref[...])
    k = _head_norm_rope(z[:, nq:nq + nk], kg_ref[...], cos, sin, e_ref[:nk, :nk])
    return q, k, z[:, nq + nk:]


def _half_placed(x):
    lane = lax.broadcasted_iota(jnp.int32, x.shape, 1)
    lo = lane < ATT_HD
    xr = pltpu.roll(x, ATT_HD, 1)
    zero = jnp.zeros_like(x)
    return ((jnp.where(lo, x, zero), jnp.where(lo, zero, xr)),
            (jnp.where(lo, xr, zero), jnp.where(lo, zero, x)))


def _attn_prompt_body(z_ref, cos_ref, sin_ref, qg_ref, kg_ref, e_ref, sink_ref, ys_in_ref,
                      y_ref, ko_ref, vo_ref, kprev_ref, vprev_ref):
    del ys_in_ref
    i = pl.program_id(0)
    nb = z_ref.shape[0]

    @pl.when(i == 0)
    def _():
        kprev_ref[...] = jnp.zeros_like(kprev_ref)
        vprev_ref[...] = jnp.zeros_like(vprev_ref)

    cos = jnp.concatenate([cos_ref[...]] * nb, axis=0)
    sin = jnp.concatenate([sin_ref[...]] * nb, axis=0)
    z = jnp.concatenate([z_ref[b] for b in range(nb)], axis=0)
    q_all, k_all, v_all = _qkv_norm_rope(z, cos, sin, qg_ref, kg_ref, e_ref)
    qb_all = q_all.astype(BF16)
    kparts, vparts, qst = [], [], []
    for b in range(nb):
        rows_b = slice(b * WINDOW, (b + 1) * WINDOW)
        k, v = k_all[rows_b], v_all[rows_b]
        ko_ref[b] = k
        vo_ref[b] = v
        kparts.append(_half_placed(jnp.concatenate([kprev_ref[b], k], axis=0)))
        vparts.append(_half_placed(jnp.concatenate([vprev_ref[b], v], axis=0)))
        kprev_ref[b] = k
        vprev_ref[b] = v
        qst.append([jnp.concatenate([qb_all[rows_b, LANES * (2 * kv):LANES * (2 * kv + 1)],
                                     qb_all[rows_b, LANES * (2 * kv + 1):LANES * (2 * kv + 2)]], axis=0)
                    for kv in range(ATT_HKV)])

    rows = 2 * WINDOW
    r = lax.broadcasted_iota(jnp.int32, (rows, rows), 0) % WINDOW
    c = lax.broadcasted_iota(jnp.int32, (rows, rows), 1)
    rel = r - c + WINDOW
    mask = (rel >= 0) & (rel < WINDOW) & ((c >= WINDOW) | (i > 0))
    top = lax.broadcasted_iota(jnp.int32, (rows, 1), 0) < WINDOW

    combos = [(b, kv, parity) for b in range(nb) for kv in range(ATT_HKV) for parity in range(2)]
    scores = [_dot_nt(qst[b][kv], kparts[b][kv][parity].astype(BF16)) * (ATT_HD ** -0.5) for b, kv, parity in combos]
    scores = [jnp.where(mask, s, NEG_INF) for s in scores]
    sinks = [jnp.where(top, sink_ref[ATT_G * kv + parity], sink_ref[ATT_G * kv + 2 + parity])
             for _, kv, parity in combos]
    maxes = [jnp.maximum(jnp.max(s, axis=-1, keepdims=True), sink) for s, sink in zip(scores, sinks)]
    exps = [jnp.exp(s - mx) for s, mx in zip(scores, maxes)]
    dens = [jnp.sum(ex, axis=-1, keepdims=True) + jnp.exp(sink - mx) for ex, sink, mx in zip(exps, sinks, maxes)]
    probs = [(ex / den).astype(BF16) for ex, den in zip(exps, dens)]
    outs = [_dot(p, vparts[b][kv][parity].astype(BF16)) for p, (b, kv, parity) in zip(probs, combos)]
    for b in range(nb):
        pair_out = []
        for kv in range(ATT_HKV):
            acc = outs[(b * ATT_HKV + kv) * 2] + outs[(b * ATT_HKV + kv) * 2 + 1]
            pair_out += [acc[:WINDOW], acc[WINDOW:]]
        y_ref[b] = jnp.concatenate(pair_out, axis=1).astype(y_ref.dtype)


def _attn_prompt_call(z_att, cos, sin, qg, kg, e, sink, ys, batch, seq):
    nb = seq // WINDOW
    nkv = ATT_HKV * ATT_HD
    const = lambda i: (0, 0)
    whole = lambda i: (0, 0, 0)
    ys, ko, vo = pl.pallas_call(
        _attn_prompt_body,
        out_shape=(_sds((N_BRANCH, batch, seq, BRANCH_W), BF16), _sds((batch, WINDOW, nkv), F32),
                   _sds((batch, WINDOW, nkv), F32)),
        grid=(nb,),
        in_specs=[
            pl.BlockSpec((batch, WINDOW, W_ATT), lambda i: (0, i, 0)),
            pl.BlockSpec((WINDOW, LANES), lambda i: (i, 0)),
            pl.BlockSpec((WINDOW, LANES), lambda i: (i, 0)),
            pl.BlockSpec((1, BRANCH_W), const),
            pl.BlockSpec((1, nkv), const),
            pl.BlockSpec((BRANCH_W, BRANCH_W), const),
            pl.BlockSpec(memory_space=pltpu.SMEM),
            pl.BlockSpec(memory_space=pl.ANY),
        ],
        out_specs=(
            pl.BlockSpec((None, batch, WINDOW, BRANCH_W), lambda i: (0, 0, i, 0)),
            pl.BlockSpec((batch, WINDOW, nkv), whole),
            pl.BlockSpec((batch, WINDOW, nkv), whole),
        ),
        input_output_aliases={7: 0},
        scratch_shapes=[pltpu.VMEM((batch, WINDOW, nkv), F32), pltpu.VMEM((batch, WINDOW, nkv), F32)],
        compiler_params=_params("arbitrary"),
        name="attn_prompt",
    )(z_att.reshape(batch, seq, W_ATT), cos, sin, qg, kg, e, sink, ys.reshape(N_BRANCH, batch, seq, BRANCH_W))
    return ys.reshape(N_BRANCH, batch * seq, BRANCH_W), ko, vo


def _attn_sample_body(z_ref, ck_ref, cv_ref, cos_ref, sin_ref, qg_ref, kg_ref, e_ref, sink_ref,
                      y_ref, ko_ref, vo_ref):
    nrow = z_ref.shape[0]
    q_all, k_all, v_all = _qkv_norm_rope(z_ref[...], cos_ref[...], sin_ref[...], qg_ref, kg_ref, e_ref)
    lane = lax.broadcasted_iota(jnp.int32, (1, LANES), 1)
    lo = lane < ATT_HD
    row = lax.broadcasted_iota(jnp.int32, (ATT_HQ, LANES), 0)
    scale = ATT_HD ** -0.5
    sink = sink_ref[:, 0:1]
    rows = range(nrow)
    ks = [k_all[i:i + 1, :] for i in rows]
    vs = [v_all[i:i + 1, :] for i in rows]
    for i in rows:
        ko_ref[i, 0:WINDOW - 1, :] = ck_ref[i, 1:WINDOW, :]
        ko_ref[i, WINDOW - 1:WINDOW, :] = ks[i]
        vo_ref[i, 0:WINDOW - 1, :] = cv_ref[i, 1:WINDOW, :]
        vo_ref[i, WINDOW - 1:WINDOW, :] = vs[i]

    qexps = []
    for i in rows:
        qexp = jnp.zeros((ATT_HQ, LANES), F32)
        for h in range(ATT_HQ):
            pair = q_all[i:i + 1, LANES * (h // 2):LANES * (h // 2 + 1)]
            want_lo = (h // ATT_G) == 0
            have_lo = (h % 2) == 0
            src = pair if want_lo == have_lo else pltpu.roll(pair, ATT_HD, 1)
            placed = jnp.where(lo if want_lo else jnp.logical_not(lo), src, 0.0)
            qexp = jnp.where(row == h, jnp.broadcast_to(placed, (ATT_HQ, LANES)), qexp)
        qexps.append(qexp)
    qbs = [qexp.astype(BF16) for qexp in qexps]
    key = lax.broadcasted_iota(jnp.int32, (ATT_HQ, WINDOW), 1)
    scores = [jnp.where(key >= 1, _dot_nt(qb, ck_ref[i].astype(BF16)) * scale, NEG_INF) for i, qb in zip(rows, qbs)]
    s_news = [jnp.sum(qb.astype(F32) * k.astype(BF16).astype(F32), axis=-1, keepdims=True) * scale
              for qb, k in zip(qbs, ks)]
    maxes = [jnp.maximum(jnp.maximum(jnp.max(s, axis=-1, keepdims=True), sn), sink) for s, sn in zip(scores, s_news)]
    exps = [jnp.exp(s - mx) for s, mx in zip(scores, maxes)]
    ex_news = [jnp.exp(sn - mx) for sn, mx in zip(s_news, maxes)]
    dens = [jnp.sum(ex, axis=-1, keepdims=True) + en + jnp.exp(sink - mx) for ex, en, mx in zip(exps, ex_news, maxes)]
    probs = [(ex / den).astype(BF16) for ex, den in zip(exps, dens)]
    p_news = [(en / den).astype(BF16).astype(F32) for en, den in zip(ex_news, dens)]
    outs = [_dot(p, cv_ref[i].astype(BF16)) + pn * v.astype(BF16).astype(F32)
            for i, p, pn, v in zip(rows, probs, p_news, vs)]

    ys = []
    for o in outs:
        pairs = []
        for j in range(ATT_HQ // 2):
            parts = []
            for par in range(2):
                h = 2 * j + par
                orow = o[h:h + 1, :]
                have_lo = (h // ATT_G) == 0
                want_lo = par == 0
                src = orow if want_lo == have_lo else pltpu.roll(orow, ATT_HD, 1)
                parts.append(jnp.where(lo if want_lo else jnp.logical_not(lo), src, 0.0))
            pairs.append(parts[0] + parts[1])
        ys.append(jnp.concatenate(pairs, axis=1))
    y_ref[...] = jnp.concatenate(ys, axis=0).astype(y_ref.dtype)


def _attn_sample_call(z_att, ck, cv, cos, sin, qg, kg, e, sink_b):
    nbatch = z_att.shape[0]
    nkv = ATT_HKV * ATT_HD
    rows = SAMPLE_ROWS
    const = lambda b: (0, 0)
    return pl.pallas_call(
        _attn_sample_body,
        out_shape=(_sds((nbatch, BRANCH_W), F32), _sds((nbatch, WINDOW, nkv), F32),
                   _sds((nbatch, WINDOW, nkv), F32)),
        grid=(nbatch // rows,),
        in_specs=[
            pl.BlockSpec((rows, W_ATT), lambda b: (b, 0)),
            pl.BlockSpec((rows, WINDOW, nkv), lambda b: (b, 0, 0)),
            pl.BlockSpec((rows, WINDOW, nkv), lambda b: (b, 0, 0)),
            pl.BlockSpec((1, LANES), const),
            pl.BlockSpec((1, LANES), const),
            pl.BlockSpec((1, BRANCH_W), const),
            pl.BlockSpec((1, nkv), const),
            pl.BlockSpec((BRANCH_W, BRANCH_W), const),
            pl.BlockSpec((ATT_HQ, LANES), const),
        ],
        out_specs=(
            pl.BlockSpec((rows, BRANCH_W), lambda b: (b, 0)),
            pl.BlockSpec((rows, WINDOW, nkv), lambda b: (b, 0, 0)),
            pl.BlockSpec((rows, WINDOW, nkv), lambda b: (b, 0, 0)),
        ),
        compiler_params=_params("parallel"),
        name="attn_sample",
    )(z_att, ck, cv, cos, sin, qg, kg, e, sink_b)


def _cmul(ar, ai, br, bi):
    return ar * br - ai * bi, ar * bi + ai * br


def _ssm_in(ub, bm_ref, j):
    blk = slice(j * SSM_BLOCK_CH, (j + 1) * SSM_BLOCK_CH)
    return _dot(ub[:, blk], bm_ref[blk, :])


def _ssm_out(s_re, s_im, cm_ref, j):
    return (_dot(s_re.astype(BF16), cm_ref[j * SSM_BLOCK_N:(j + 1) * SSM_BLOCK_N, :])
            + _dot(s_im.astype(BF16), cm_ref[SSM_N + j * SSM_BLOCK_N:SSM_N + (j + 1) * SSM_BLOCK_N, :]))


def _ssm_tail(y, u, d_ref, gw_ref, gb_ref):
    y = jax.nn.gelu(y + d_ref[...] * u)
    return y * jax.nn.sigmoid(_dot(y.astype(BF16), gw_ref[...]) + gb_ref[...])


def _ssm_prompt_body(u_ref, perm_ref, unperm_ref, bm_ref, cm_ref, a_ref, d_ref, gw_ref, gb_ref, ys_in_ref,
                     y_ref, sre_ref, sim_ref,
                     bu_ref, start_ref, carry_ref):
    del ys_in_ref
    tb = u_ref.shape[0]
    ts = tb // SSM_SEGS
    step_j = pl.program_id(1)

    @pl.when(step_j == 0)
    def _():
        carry_ref[...] = jnp.zeros_like(carry_ref)

    u = u_ref[...]
    u_hi, u_mid = _split2(u)
    u_lo = (u - u_hi.astype(F32) - u_mid.astype(F32)).astype(BF16)
    perm = perm_ref[...]
    up = _dot(perm, u_hi) + (_dot(perm, u_mid) + _dot(perm, u_lo))
    ub = up.astype(BF16)
    cw = SSM_BLOCK_N
    for j in range(SSM_BLOCKS):
        bu = _ssm_in(ub, bm_ref, j)
        bu_ref[:, j * cw:(j + 1) * cw] = bu[:, :cw]
        bu_ref[:, SSM_N + j * cw:SSM_N + (j + 1) * cw] = bu[:, cw:]

    def scan(c0, init_re, init_im, store):
        ar = jnp.broadcast_to(a_ref[0:1, c0:c0 + cw], (SUBLANES, cw))
        ai = jnp.broadcast_to(a_ref[1:2, c0:c0 + cw], (SUBLANES, cw))

        def step(t, st):
            sr, si = st
            r0 = pl.multiple_of(t * SUBLANES, SUBLANES)
            br = bu_ref[pl.ds(r0, SUBLANES), c0:c0 + cw]
            bi = bu_ref[pl.ds(r0, SUBLANES), SSM_N + c0:SSM_N + c0 + cw]
            pr, pi = _cmul(ar, ai, sr, si)
            sr, si = pr + br, pi + bi
            if store:
                bu_ref[pl.ds(r0, SUBLANES), c0:c0 + cw] = sr
                bu_ref[pl.ds(r0, SUBLANES), SSM_N + c0:SSM_N + c0 + cw] = si
            return sr, si

        return lax.fori_loop(0, ts, step, (init_re, init_im), unroll=2)

    zero = jnp.zeros((SUBLANES, cw), F32)
    ys = []
    for j in range(SSM_BLOCKS):
        c0 = j * cw
        er, ei = scan(c0, zero, zero, False)
        pr, pi = a_ref[0:1, c0:c0 + cw], a_ref[1:2, c0:c0 + cw]
        n = 1
        while n < ts:
            pr, pi = _cmul(pr, pi, pr, pi)
            n *= 2
        sr = carry_ref[0:1, c0:c0 + cw]
        si = carry_ref[1:2, c0:c0 + cw]
        for s in range(SSM_SEGS):
            start_ref[s:s + 1, c0:c0 + cw] = sr
            start_ref[s:s + 1, SSM_N + c0:SSM_N + c0 + cw] = si
            qr, qi = _cmul(pr, pi, sr, si)
            sr, si = qr + er[s:s + 1, :], qi + ei[s:s + 1, :]
        carry_ref[0:1, c0:c0 + cw] = sr
        carry_ref[1:2, c0:c0 + cw] = si
        scan(c0, start_ref[:, c0:c0 + cw], start_ref[:, SSM_N + c0:SSM_N + c0 + cw], True)
        ys.append(_ssm_out(bu_ref[:, c0:c0 + cw], bu_ref[:, SSM_N + c0:SSM_N + c0 + cw], cm_ref, j))

    sre_ref[0] = carry_ref[0:1, :]
    sim_ref[0] = carry_ref[1:2, :]
    y = _ssm_tail(jnp.concatenate(ys, axis=1), up, d_ref, gw_ref, gb_ref).astype(BF16)
    y_ref[...] = _dot(unperm_ref[...], y).astype(y_ref.dtype)


def _ssm_prompt_call(z_u, bm, cm, a, d, gw, gb, layer, ys, batch, seq, tb):
    nblk = seq // tb
    const = lambda b, j: (0, 0)
    of_layer = lambda b, j: (layer, 0, 0)
    ts = tb // SSM_SEGS
    src = (jnp.arange(tb) % SSM_SEGS) * ts + jnp.arange(tb) // SSM_SEGS
    perm = (src[:, None] == jnp.arange(tb)[None, :]).astype(BF16)
    return pl.pallas_call(
        _ssm_prompt_body,
        out_shape=(_sds(ys.shape, ys.dtype), _sds((batch, 1, SSM_N), F32), _sds((batch, 1, SSM_N), F32)),
        grid=(batch, nblk),
        in_specs=[
            pl.BlockSpec((tb, BRANCH_W), lambda b, j: (b * nblk + j, 0)),
            pl.BlockSpec((tb, tb), const),
            pl.BlockSpec((tb, tb), const),
            pl.BlockSpec((None, BRANCH_W, 2 * SSM_BLOCK_N), of_layer),
            pl.BlockSpec((None, 2 * SSM_N, SSM_BLOCK_CH), of_layer),
            pl.BlockSpec((2, SSM_N), const),
            pl.BlockSpec((1, BRANCH_W), const),
            pl.BlockSpec((None, BRANCH_W, BRANCH_W), of_layer),
            pl.BlockSpec((1, BRANCH_W), const),
            pl.BlockSpec(memory_space=pl.ANY),
        ],
        out_specs=(
            pl.BlockSpec((None, tb, BRANCH_W), lambda b, j: (1, b * nblk + j, 0)),
            pl.BlockSpec((1, 1, SSM_N), lambda b, j: (b, 0, 0)),
            pl.BlockSpec((1, 1, SSM_N), lambda b, j: (b, 0, 0)),
        ),
        input_output_aliases={9: 0},
        scratch_shapes=[
            pltpu.VMEM((tb, 2 * SSM_N), F32),
            pltpu.VMEM((SSM_SEGS, 2 * SSM_N), F32),
            pltpu.VMEM((2, SSM_N), F32),
        ],
        compiler_params=_params("arbitrary", "arbitrary"),
        name="ssm_prompt",
    )(z_u, perm, perm.T, bm, cm, a, d, gw, gb, ys)


def _ssm_sample_body(u_ref, s0r_ref, s0i_ref, bm_ref, cm_ref, a_ref, d_ref, gw_ref, gb_ref,
                     y_ref, sre_ref, sim_ref):
    u = u_ref[...]
    ub = u.astype(BF16)
    cw = SSM_BLOCK_N
    ys = []
    for j in range(SSM_BLOCKS):
        cols = slice(j * cw, (j + 1) * cw)
        bu = _ssm_in(ub, bm_ref, j)
        pr, pi = _cmul(a_ref[0:1, cols], a_ref[1:2, cols], s0r_ref[:, cols], s0i_ref[:, cols])
        sr = pr + bu[:, :cw]
        si = pi + bu[:, cw:]
        sre_ref[:, cols] = sr
        sim_ref[:, cols] = si
        ys.append(_ssm_out(sr, si, cm_ref, j))
    y_ref[...] = _ssm_tail(jnp.concatenate(ys, axis=1), u, d_ref, gw_ref, gb_ref).astype(y_ref.dtype)


def _ssm_sample_call(z_u, s0r, s0i, bm, cm, a, d, gw, gb, layer):
    nbatch = z_u.shape[0]
    const = lambda i: (0, 0)
    of_layer = lambda i: (layer, 0, 0)
    rows_w = pl.BlockSpec((nbatch, BRANCH_W), const)
    rows_n = pl.BlockSpec((nbatch, SSM_N), const)
    return pl.pallas_call(
        _ssm_sample_body,
        out_shape=(_sds((nbatch, BRANCH_W), F32), _sds((nbatch, SSM_N), F32), _sds((nbatch, SSM_N), F32)),
        grid=(1,),
        in_specs=[
            rows_w, rows_n, rows_n,
            pl.BlockSpec((None, BRANCH_W, 2 * SSM_BLOCK_N), of_layer),
            pl.BlockSpec((None, 2 * SSM_N, SSM_BLOCK_CH), of_layer),
            pl.BlockSpec((2, SSM_N), const),
            pl.BlockSpec((1, BRANCH_W), const),
            pl.BlockSpec((None, BRANCH_W, BRANCH_W), of_layer),
            pl.BlockSpec((1, BRANCH_W), const),
        ],
        out_specs=(rows_w, rows_n, rows_n),
        compiler_params=_params("arbitrary"),
        name="ssm_sample",
    )(z_u, s0r, s0i, bm, cm, a, d, gw, gb)


def _spatial_gate(z, g_ref, w_ref, b_ref):
    nch = z.shape[0] // CHUNK
    uv = jax.nn.gelu(z)
    u = uv[:, :BRANCH_W]
    v = _rms(uv[:, BRANCH_W:], g_ref[...])
    t = lax.broadcasted_iota(jnp.int32, (CHUNK, CHUNK), 0)
    s = lax.broadcasted_iota(jnp.int32, (CHUNK, CHUNK), 1)
    causal = t >= s
    vb = v.astype(BF16)
    ws = [jnp.where(causal, w_ref[g], 0.0).astype(BF16) for g in range(SG_GROUPS)]
    ys = []
    for ci in range(nch):
        rows = slice(ci * CHUNK, (ci + 1) * CHUNK)
        mixed = jnp.concatenate([_dot(ws[g], vb[rows, g * SG_GW:(g + 1) * SG_GW]) for g in range(SG_GROUPS)], axis=1)
        ys.append(u[rows, :] * (mixed + b_ref[...]))
    return jnp.concatenate(ys, axis=0), v[(nch - 1) * CHUNK:, :]


def _sg_sample_body(z_ref, g_ref, w00_ref, b_ref, y_ref, v_ref):
    uv = jax.nn.gelu(z_ref[...])
    u = uv[:, :BRANCH_W]
    v = _rms(uv[:, BRANCH_W:], g_ref[...])
    v_ref[...] = v
    w = w00_ref[...].astype(BF16).astype(F32)
    mixed = w * v.astype(BF16).astype(F32) + b_ref[0:1, :]
    y_ref[...] = (u * mixed).astype(y_ref.dtype)


def _sg_sample_call(z_c, g, w00, b_exp):
    nbatch = z_c.shape[0]
    return pl.pallas_call(
        _sg_sample_body,
        out_shape=(_sds((nbatch, BRANCH_W), F32), _sds((nbatch, BRANCH_W), F32)),
        name="sg_sample",
    )(z_c, g, w00, b_exp)


def _l2n(x):
    return x * lax.rsqrt(jnp.sum(x * x, axis=-1, keepdims=True) + EPS)


def _dn_gates(zab, alog_ref, dtb_ref):
    g = -jnp.exp(alog_ref[...]) * jax.nn.softplus(zab + dtb_ref[...])
    beta = jax.nn.sigmoid(zab)
    return g, beta


def _mm3(a, b):
    (ah, al), (bh, bl) = a, b
    return _dot(ah, bh) + (_dot(ah, bl) + _dot(al, bh))


def _unit_lower_inverse_off(mats, blk):
    n = mats[0].shape[0]
    r = lax.broadcasted_iota(jnp.int32, (n, n), 0)
    c = lax.broadcasted_iota(jnp.int32, (n, n), 1)

    def lower_left(size):
        return ((r // (2 * size)) == (c // (2 * size))) & ((r // size) % 2 == 1) & ((c // size) % 2 == 0)

    first = lower_left(1)
    offs = [-jnp.where(first, a, 0.0) for a in mats]
    size = 2
    while size < blk:
        mask = lower_left(size)
        ams = [jnp.where(mask, a, 0.0) for a in mats]
        obs = [off.astype(BF16) for off in offs]
        ps = [am + _dot(ob, am.astype(BF16)) for am, ob in zip(ams, obs)]
        offs = [off - (p + _dot(p.astype(BF16), ob)) for off, p, ob in zip(offs, ps, obs)]
        size *= 2
    return offs


def _head_lanes(x, rows_per_head):
    r = lax.broadcasted_iota(jnp.int32, x.shape, 0) // rows_per_head
    return jnp.concatenate([jnp.where(r == h, x, 0.0) for h in range(DN_H)], axis=1)


def _dn_chunk_terms(chunks):
    tc = DN_CHUNK
    n = DN_H * tc
    heads = range(DN_H)

    def stack(x, rows, off):
        return jnp.concatenate([x[rows, off + h * DN_HD:off + (h + 1) * DN_HD] for h in heads], axis=0)

    rows = [slice(r0, r0 + tc) for *_, r0 in chunks]
    qs = [_l2n(stack(ch[0], rw, 0)) * (DN_HD ** -0.5) for ch, rw in zip(chunks, rows)]
    ks = [_l2n(stack(ch[0], rw, BRANCH_W)) for ch, rw in zip(chunks, rows)]
    vs = [stack(ch[0], rw, 2 * BRANCH_W) for ch, rw in zip(chunks, rows)]
    zgss = [stack(ch[1], rw, 0) for ch, rw in zip(chunks, rows)]

    rt = lax.broadcasted_iota(jnp.int32, (tc, tc), 0)
    ct = lax.broadcasted_iota(jnp.int32, (tc, tc), 1)
    ones_tril = jnp.where(rt >= ct, 1.0, 0.0).astype(BF16)
    gparts = []
    for ch, rw in zip(chunks, rows):
        g = ch[2][rw]
        g_hi, g_lo = _split2(g)
        gparts.append((g_hi, g_lo, (g - g_hi.astype(F32) - g_lo.astype(F32)).astype(BF16)))
    gc_cs = [_dot(ones_tril, hi) + (_dot(ones_tril, lo) + _dot(ones_tril, lo2)) for hi, lo, lo2 in gparts]
    gcs = [jnp.concatenate([gc_c[:, h:h + 1] for h in heads], axis=0) for gc_c in gc_cs]
    betas = [jnp.concatenate([ch[3][rw, DN_H + h:DN_H + h + 1] for h in heads], axis=0)
             for ch, rw in zip(chunks, rows)]
    gc_lasts = [jnp.concatenate([jnp.broadcast_to(gc_c[tc - 1:tc, h:h + 1], (tc, 1)) for h in heads], axis=0)
                for gc_c in gc_cs]

    r = lax.broadcasted_iota(jnp.int32, (n, n), 0)
    c = lax.broadcasted_iota(jnp.int32, (n, n), 1)
    same = (r // tc) == (c // tc)
    tril = same & (r >= c)
    stril = same & (r > c)
    gmats = [jnp.broadcast_to(gc, (n, n)) for gc in gcs]
    decs = [jnp.where(tril, jnp.exp(jnp.where(tril, gm - gm.T, 0.0)), 0.0) for gm in gmats]
    kbs = [k.astype(BF16) for k in ks]
    kks = [_dot_nt(kb, kb) for kb in kbs]
    mats = [jnp.where(stril, beta * kk * dec, 0.0) for beta, kk, dec in zip(betas, kks, decs)]
    inv_offs = _unit_lower_inverse_off(mats, tc)

    egs = [jnp.exp(gc) for gc in gcs]
    rhss = [jnp.concatenate([beta * v, (beta * eg) * k], axis=1) for beta, eg, k, v in zip(betas, egs, ks, vs)]
    inv_parts = [_split2(x) for x in inv_offs]
    rhs_parts = [_split2(x) for x in rhss]
    sols = [rhs + _mm3(ip, rp) for rhs, ip, rp in zip(rhss, inv_parts, rhs_parts)]
    qks = [(_dot_nt(q.astype(BF16), kb) * dec).astype(BF16) for q, kb, dec in zip(qs, kbs, decs)]
    wqs = [jnp.concatenate([_head_lanes(sol[:, DN_HD:], tc), _head_lanes(q * eg, tc)], axis=0).astype(BF16)
           for sol, q, eg in zip(sols, qs, egs)]
    ke_ts = [_head_lanes(k * jnp.exp(gl - gc), tc).T.astype(BF16) for k, gl, gc in zip(ks, gc_lasts, gcs)]
    ges = [jnp.concatenate([jnp.broadcast_to(jnp.exp(gc_c[tc - 1:tc, h:h + 1]), (DN_HD, 1)) for h in heads], axis=0)
           for gc_c in gc_cs]
    return [(sol[:, :DN_HD], wq, qk, ke_t, ge, zgs)
            for sol, wq, qk, ke_t, ge, zgs in zip(sols, wqs, qks, ke_ts, ges, zgss)]


def _dn_chunk_steps(states, terms, ng):
    n = terms[0][0].shape[0]
    wss = [_dot(t[1], s.astype(BF16)) for s, t in zip(states, terms)]
    ubs = [(t[0] - ws[:n]).astype(BF16) for ws, t in zip(wss, terms)]
    outs = [ws[n:] + _dot(t[2], ub) for ws, ub, t in zip(wss, ubs, terms)]
    states = [t[4] * s + _dot(t[3], ub) for s, ub, t in zip(states, ubs, terms)]
    return states, [_rms(o, ng) * jax.nn.silu(t[5]) for o, t in zip(outs, terms)]


def _dn_prompt_body(x_ref, zg_ref, zab_ref, cw_ref, alog_ref, dtb_ref, ng_ref, ys_in_ref,
                    y_ref, so_ref, co_ref, xin_ref, s_ref):
    del ys_in_ref
    step = pl.program_id(0)
    nb, tb = x_ref.shape[0], x_ref.shape[1]
    tc = DN_CHUNK
    nci = tb // tc
    pad = SUBLANES

    @pl.when(step == 0)
    def _():
        xin_ref[:, 0:pad, :] = jnp.zeros((nb, pad, W_DN_QKV), F32)
        s_ref[...] = jnp.zeros_like(s_ref)

    @pl.when(step > 0)
    def _():
        xin_ref[:, 0:pad, :] = xin_ref[:, tb:tb + pad, :]

    chunks = []
    for b in range(nb):
        x = x_ref[b]
        xin_ref[b, pad:pad + tb, :] = x
        co_ref[b] = x_ref[b, tb - (DN_CONV - 1):tb, :]
        acc = cw_ref[DN_CONV - 1:DN_CONV, :] * x
        for i in range(DN_CONV - 1):
            lag = DN_CONV - 1 - i
            acc = acc + cw_ref[i:i + 1, :] * xin_ref[b, pad - lag:pad - lag + tb, :]
        qkv = jax.nn.silu(acc)
        g_all, beta_all = _dn_gates(zab_ref[b], alog_ref, dtb_ref)
        chunks += [(qkv, zg_ref[b], g_all, beta_all, ci * tc) for ci in range(nci)]
    terms = _dn_chunk_terms(chunks)

    states = [s_ref[b] for b in range(nb)]
    for ci in range(nci):
        states, outs = _dn_chunk_steps(states, [terms[b * nci + ci] for b in range(nb)], ng_ref[...])
        for b in range(nb):
            y_ref[b, ci * tc:(ci + 1) * tc, :] = jnp.concatenate(
                [outs[b][h * tc:(h + 1) * tc] for h in range(DN_H)], axis=1).astype(y_ref.dtype)
    for b in range(nb):
        s_ref[b] = states[b]

    @pl.when(step == pl.num_programs(0) - 1)
    def _():
        so_ref[...] = s_ref[...]


def _dn_prompt_call(z_dn, cw, alog, dtb, ng, ys, batch, seq):
    tb = DN_STEP_CHUNKS * DN_CHUNK if seq % (DN_STEP_CHUNKS * DN_CHUNK) == 0 else DN_CHUNK
    z3 = z_dn.reshape(batch, seq, W_DN)
    ys4 = ys.reshape(N_BRANCH, batch, seq, BRANCH_W)
    const = lambda i: (0, 0)
    whole = lambda i: (0, 0, 0)
    y, so, co = pl.pallas_call(
        _dn_prompt_body,
        out_shape=(_sds(ys4.shape, ys4.dtype), _sds((batch, DN_H * DN_HD, DN_HD), F32),
                   _sds((batch, DN_CONV - 1, W_DN_QKV), F32)),
        grid=(seq // tb,),
        in_specs=[
            pl.BlockSpec((batch, tb, W_DN_QKV), lambda i: (0, i, 0)),
            pl.BlockSpec((batch, tb, BRANCH_W), lambda i: (0, i, W_DN_QKV // BRANCH_W)),
            pl.BlockSpec((batch, tb, LANES), lambda i: (0, i, (W_DN_QKV + BRANCH_W) // LANES)),
            pl.BlockSpec((DN_CONV, W_DN_QKV), const),
            pl.BlockSpec((1, LANES), const),
            pl.BlockSpec((1, LANES), const),
            pl.BlockSpec((1, DN_HD), const),
            pl.BlockSpec(memory_space=pl.ANY),
        ],
        out_specs=(
            pl.BlockSpec((None, batch, tb, BRANCH_W), lambda i: (N_BRANCH - 1, 0, i, 0)),
            pl.BlockSpec((batch, DN_H * DN_HD, DN_HD), whole),
            pl.BlockSpec((batch, DN_CONV - 1, W_DN_QKV), whole),
        ),
        input_output_aliases={7: 0},
        scratch_shapes=[pltpu.VMEM((batch, tb + SUBLANES, W_DN_QKV), F32),
                        pltpu.VMEM((batch, DN_H * DN_HD, DN_HD), F32)],
        compiler_params=_params("arbitrary"),
        name="dn_prompt",
    )(z3, z3, z3, cw, alog, dtb, ng, ys4)
    return y.reshape(ys.shape), so.reshape(batch, DN_H, DN_HD, DN_HD), co


def _dn_sample_body(x_ref, zg_ref, zab_ref, conv_ref, s0_ref, cw_ref, alog_ref, dtb_ref, ng_ref, *rest):
    y_ref, so_ref, co_ref = rest[-3:]
    nrow = x_ref.shape[0]
    x_all = x_ref[...]
    acc = cw_ref[DN_CONV - 1:DN_CONV, :] * x_all
    for j in range(DN_CONV - 1):
        acc = acc + cw_ref[j:j + 1, :] * conv_ref[:, j, :]
    qkv_all = jax.nn.silu(acc)
    g_all, beta_all = _dn_gates(zab_ref[...], alog_ref, dtb_ref)
    zg_all = zg_ref[...]
    row8 = lax.broadcasted_iota(jnp.int32, (SUBLANES, DN_HD), 0)
    rows = range(nrow)
    heads = range(DN_H)
    for i in rows:
        co_ref[i, 0:DN_CONV - 2, :] = conv_ref[i, 1:DN_CONV - 1, :]
        co_ref[i, DN_CONV - 2:DN_CONV - 1, :] = x_all[i:i + 1, :]

    qs = [_l2n(qkv_all[:, h * DN_HD:(h + 1) * DN_HD]) * (DN_HD ** -0.5) for h in heads]
    ks = [_l2n(qkv_all[:, BRANCH_W + h * DN_HD:BRANCH_W + (h + 1) * DN_HD]) for h in heads]
    vs = [qkv_all[:, 2 * BRANCH_W + h * DN_HD:2 * BRANCH_W + (h + 1) * DN_HD] for h in heads]
    egs = [jnp.exp(g_all[:, h:h + 1]) for h in heads]
    betas = [beta_all[:, DN_H + h:DN_H + h + 1] for h in heads]
    w_ks = [(beta * eg) * k for beta, eg, k in zip(betas, egs, ks)]
    q_gs = [q * eg for q, eg in zip(qs, egs)]
    qks = [jnp.sum(q.astype(BF16).astype(F32) * k.astype(BF16).astype(F32), axis=-1, keepdims=True)
           for q, k in zip(qs, ks)]
    bvs = [beta * v for beta, v in zip(betas, vs)]

    pairs = [(i, h) for i in rows for h in heads]
    lhs = [jnp.where(row8 == 0, jnp.broadcast_to(w_ks[h][i:i + 1, :], (SUBLANES, DN_HD)),
                     jnp.where(row8 == 1, jnp.broadcast_to(q_gs[h][i:i + 1, :], (SUBLANES, DN_HD)), 0.0)).astype(BF16)
           for i, h in pairs]
    prods = [_dot(l, s0_ref[i, h].astype(BF16)) for l, (i, h) in zip(lhs, pairs)]
    us = [bvs[h][i:i + 1, :] - p[0:1, :] for p, (i, h) in zip(prods, pairs)]
    outs = [p[1:2, :] + qks[h][i:i + 1, :] * u for p, u, (i, h) in zip(prods, us, pairs)]
    kcols = []
    for i in rows:
        k8 = jnp.zeros((SUBLANES, DN_HD), F32)
        for h in heads:
            k8 = jnp.where(row8 == h, jnp.broadcast_to(ks[h][i:i + 1, :], (SUBLANES, DN_HD)), k8)
        kcols.append(k8.T)
    for u, (i, h) in zip(us, pairs):
        so_ref[i, h] = egs[h][i:i + 1, :] * s0_ref[i, h] + kcols[i][:, h:h + 1] * u

    ys = []
    for h in heads:
        o_h = jnp.concatenate([outs[i * DN_H + h] for i in rows], axis=0)
        ys.append(_rms(o_h, ng_ref[...]) * jax.nn.silu(zg_all[:, h * DN_HD:(h + 1) * DN_HD]))
    y_ref[...] = jnp.concatenate(ys, axis=1).astype(y_ref.dtype)


def _dn_sample_call(z_dn, conv0, s0_all, states_out, layer, cw, alog, dtb, ng):
    nbatch = z_dn.shape[0]
    rows = SAMPLE_ROWS
    const = lambda b: (0, 0)
    state_spec = pl.BlockSpec((None, rows, DN_H, DN_HD, DN_HD), lambda b: (layer, b, 0, 0, 0))
    chained = states_out is not None
    return pl.pallas_call(
        _dn_sample_body,
        out_shape=(_sds((nbatch, BRANCH_W), F32), _sds(s0_all.shape, F32),
                   _sds((nbatch, DN_CONV - 1, W_DN_QKV), F32)),
        grid=(nbatch // rows,),
        in_specs=[
            pl.BlockSpec((rows, W_DN_QKV), lambda b: (b, 0)),
            pl.BlockSpec((rows, BRANCH_W), lambda b: (b, W_DN_QKV // BRANCH_W)),
            pl.BlockSpec((rows, LANES), lambda b: (b, (W_DN_QKV + BRANCH_W) // LANES)),
            pl.BlockSpec((rows, DN_CONV - 1, W_DN_QKV), lambda b: (b, 0, 0)),
            state_spec,
            pl.BlockSpec((DN_CONV, W_DN_QKV), const),
            pl.BlockSpec((1, LANES), const),
            pl.BlockSpec((1, LANES), const),
            pl.BlockSpec((1, DN_HD), const),
        ] + ([pl.BlockSpec(memory_space=pl.ANY)] if chained else []),
        out_specs=(
            pl.BlockSpec((rows, BRANCH_W), lambda b: (b, 0)),
            state_spec,
            pl.BlockSpec((rows, DN_CONV - 1, W_DN_QKV), lambda b: (b, 0, 0)),
        ),
        input_output_aliases={9: 1} if chained else {},
        compiler_params=_params("parallel"),
        name="dn_sample",
    )(z_dn, z_dn, z_dn, conv0, s0_all, cw, alog, dtb, ng, *([states_out] if chained else []))


def _rope_tables(pos):
    half = ATT_HD // 2
    inv = ROPE_THETA ** (-jnp.arange(half, dtype=F32) / half)
    ang = pos.astype(F32)[:, None] * inv[None, :]
    cos, sin = jnp.cos(ang), jnp.sin(ang)
    reps = LANES // ATT_HD
    return (jnp.tile(jnp.concatenate([cos, cos], axis=1), (1, reps)),
            jnp.tile(jnp.concatenate([-sin, sin], axis=1), (1, reps)))


def _ssm_matrices(lam_re, lam_im, log_dt, b_re, b_im, c_re, c_im):
    dt = jnp.exp(log_dt)[:, None]
    mag = jnp.exp(lam_re * dt)
    a_re, a_im = mag * jnp.cos(lam_im * dt), mag * jnp.sin(lam_im * dt)
    den = lam_re * lam_re + lam_im * lam_im
    f_re = ((a_re - 1.0) * lam_re + a_im * lam_im) / den
    f_im = (a_im * lam_re - (a_re - 1.0) * lam_im) / den
    bb_re = f_re[..., None] * b_re - f_im[..., None] * b_im
    bb_im = f_re[..., None] * b_im + f_im[..., None] * b_re
    gpb = SSM_NG // SSM_BLOCKS
    eye = jnp.eye(gpb, dtype=F32)

    def in_layout(b):
        b = b.reshape(SSM_BLOCKS, gpb, SSM_P, SSM_GROUP)
        return jnp.einsum("jipc,ik->jickp", b, eye).reshape(BRANCH_W, SSM_BLOCK_N)

    def out_layout(c):
        c = c.reshape(SSM_BLOCKS, gpb, SSM_GROUP, SSM_P)
        return jnp.einsum("jicp,ik->jipkc", c, eye).reshape(SSM_N, SSM_BLOCK_CH)

    bm = jnp.concatenate([in_layout(bb_re), in_layout(bb_im)], axis=1).astype(BF16)
    cm = jnp.concatenate([out_layout(c_re), out_layout(-c_im)], axis=0).astype(BF16)
    a = jnp.stack([a_re.reshape(SSM_N), a_im.reshape(SSM_N)], axis=0)
    return bm, cm, a


def _w_main_body(w_ref, o_ref):
    rb = o_ref.shape[0]
    row = pl.program_id(1) * rb + lax.broadcasted_iota(jnp.int32, (rb, 1), 0)
    o_ref[...] = jnp.where(row < N_MAIN, w_ref[...], 0.0).astype(o_ref.dtype)


def _w_gate_body(w_ref, buf_ref, o_ref):
    del buf_ref
    o_ref[...] = w_ref[0].astype(o_ref.dtype)


def _w_in_layout_call(w_t):
    depth, n_in, d = w_t.shape
    rows = GATE_COL0 + N_BRANCH * d
    rb_main = W_MAIN // 7
    assert W_MAIN % rb_main == 0 and rb_main % (2 * SUBLANES) == 0 and n_in - N_MAIN == N_BRANCH * d
    buf = pl.pallas_call(
        _w_main_body,
        out_shape=_sds((depth, rows, d), BF16),
        grid=(depth, W_MAIN // rb_main),
        in_specs=[pl.BlockSpec((None, rb_main, d), lambda l, i: (l, i, 0))],
        out_specs=pl.BlockSpec((None, rb_main, d), lambda l, i: (l, i, 0)),
        compiler_params=_params("parallel", "parallel"),
        name="w_main_layout",
    )(w_t)
    rb = 512
    return pl.pallas_call(
        _w_gate_body,
        out_shape=_sds((depth, rows, d), BF16),
        grid=(depth, N_BRANCH * d // rb),
        in_specs=[
            pl.BlockSpec((pl.Element(1), pl.Element(rb), pl.Element(d)),
                         lambda l, i: (l, pl.multiple_of(N_MAIN + i * rb, SUBLANES), 0)),
            pl.BlockSpec(memory_space=pl.ANY),
        ],
        out_specs=pl.BlockSpec((None, rb, d), lambda l, i: (l, GATE_COL0 // rb + i, 0)),
        input_output_aliases={1: 0},
        compiler_params=_params("parallel", "parallel"),
        name="w_gate_layout",
    )(w_t, buf)


def _lane_rows(x):
    return jnp.pad(x, ((0, 0), (0, LANES - x.shape[1])))[:, None, :]


def _prepare(p):
    depth = p["w_in"].shape[0]
    w_in = p["w_in"]
    bm, cm, a = jax.vmap(_ssm_matrices)(p["ssm_lam_re"], p["ssm_lam_im"], p["ssm_log_dt"], p["ssm_b_re"],
                                        p["ssm_b_im"], p["ssm_c_re"], p["ssm_c_im"])
    w_all = _w_in_layout_call(jnp.swapaxes(w_in, 1, 2))
    stacked = dict(
        w_all=w_all, w_branch=p["w_branch"].astype(BF16), w_out=p["w_out"].astype(BF16),
        w_ff1=p["w_ff1"].astype(BF16), w_ff2=p["w_ff2"].astype(BF16),
        bm=bm, cm=cm, glu_w=p["ssm_glu_w"].astype(BF16),
    )
    per_layer = dict(
        norm1_g=p["norm1_g"][:, None, :], norm2_g=p["norm2_g"][:, None, :],
        qg=jnp.tile(p["att_qn_g"], (1, ATT_HQ))[:, None, :],
        kg=jnp.tile(p["att_kn_g"], (1, ATT_HKV))[:, None, :],
        sink=p["att_sink"],
        sink_b=jnp.broadcast_to(p["att_sink"][:, :, None], (depth, ATT_HQ, LANES)),
        a=a, ssm_d=p["ssm_d"][:, None, :], glu_b=p["ssm_glu_b"][:, None, :],
        sg_g=p["sg_norm_g"][:, None, :], sg_w=p["sg_w"],
        sg_b=jnp.repeat(jnp.swapaxes(p["sg_b"], 1, 2), SG_GW, axis=2),
        sg_w00=jnp.repeat(p["sg_w"][:, :, 0, 0], SG_GW, axis=1)[:, None, :],
        cw=p["dn_conv_w"], alog=_lane_rows(p["dn_a_log"]), dtb=_lane_rows(p["dn_dt_bias"]),
        ng=p["dn_norm_g"][:, None, :],
    )
    return stacked, per_layer


def _mix_and_ffn(x, h, ys, wts, layer, lp, g_next, tm):
    mix = _merge_call(h, ys, wts["w_all"], wts["w_branch"], layer, tm)
    x, h2 = _proj_call(mix, wts["w_out"], layer, x, lp["norm2_g"], tm)
    return _ffn_call(h2, wts["w_ff1"], wts["w_ff2"], layer, x, g_next, tm, FFN_TF)


def _layer(xp, hp, xs, hs, st, wts, layer, lp, g_next, rope_p, rope_s, e, batch, seq):
    tmp = ROWS_PROMPT if xp.shape[0] % ROWS_PROMPT == 0 else xp.shape[0]
    tms = xs.shape[0]
    ck, cv, s0r, s0i, s0d_all, sdelta, conv0 = st
    ssm_w = (wts["bm"], wts["cm"], lp["a"], lp["ssm_d"], wts["glu_w"], lp["glu_b"], layer)

    tm_in = ROWS_PROMPT if seq % ROWS_PROMPT == 0 else CHUNK
    z_att, z_ssm, z_dn, ys, pchunk = _in_proj_sg_call(hp, wts["w_all"], layer, lp["sg_g"], lp["sg_w"], lp["sg_b"],
                                                     batch, seq, tm_in)
    ys, pk, pv = _attn_prompt_call(z_att, rope_p[0], rope_p[1], lp["qg"], lp["kg"], e, lp["sink"], ys, batch, seq)
    tb = 512 if seq % 512 == 0 else seq
    ys, pre, pim = _ssm_prompt_call(z_ssm, *ssm_w, ys, batch, seq, tb)
    ys, pdelta, pconv = _dn_prompt_call(z_dn, lp["cw"], lp["alog"], lp["dtb"], lp["ng"], ys, batch, seq)
    xp, hp = _mix_and_ffn(xp, hp, ys, wts, layer, lp, g_next, tmp)

    z_att, z_ssm, z_sg, z_dn = _in_proj_call(hs, wts["w_all"], layer, tms)
    ya, sk, sv = _attn_sample_call(z_att, ck, cv, rope_s[0], rope_s[1], lp["qg"], lp["kg"], e, lp["sink_b"])
    yb, sre, sim = _ssm_sample_call(z_ssm, s0r, s0i, *ssm_w)
    yc, schunk = _sg_sample_call(z_sg, lp["sg_g"], lp["sg_w00"], lp["sg_b"])
    yd, sdelta, sconv = _dn_sample_call(z_dn, conv0, s0d_all, sdelta, layer, lp["cw"], lp["alog"], lp["dtb"], lp["ng"])
    xs, hs = _mix_and_ffn(xs, hs, jnp.stack([ya, yb, yc, yd]).astype(BF16), wts, layer, lp, g_next, tms)

    new_p = (pk, pv, pre, pim, pdelta, pconv, pchunk)
    new_s = (sk, sv, sre, sim, sdelta, sconv, schunk)
    return xp, hp, xs, hs, new_p, new_s


def kernel(x_prompt, x_sample, cache_k, cache_v, state_ssm_re, state_ssm_im, state_delta, state_conv, norm1_g, w_in, att_qn_g, att_kn_g, att_sink, ssm_lam_re, ssm_lam_im, ssm_log_dt, ssm_b_re, ssm_b_im, ssm_c_re, ssm_c_im, ssm_d, ssm_glu_w, ssm_glu_b, sg_norm_g, sg_w, sg_b, dn_conv_w, dn_a_log, dn_dt_bias, dn_norm_g, w_branch, w_out, norm2_g, w_ff1, w_ff2):
    p = dict(norm1_g=norm1_g, w_in=w_in, att_qn_g=att_qn_g, att_kn_g=att_kn_g, att_sink=att_sink,
             ssm_lam_re=ssm_lam_re, ssm_lam_im=ssm_lam_im, ssm_log_dt=ssm_log_dt, ssm_b_re=ssm_b_re,
             ssm_b_im=ssm_b_im, ssm_c_re=ssm_c_re, ssm_c_im=ssm_c_im, ssm_d=ssm_d, ssm_glu_w=ssm_glu_w,
             ssm_glu_b=ssm_glu_b, sg_norm_g=sg_norm_g, sg_w=sg_w, sg_b=sg_b, dn_conv_w=dn_conv_w, dn_a_log=dn_a_log,
             dn_dt_bias=dn_dt_bias, dn_norm_g=dn_norm_g, w_branch=w_branch, w_out=w_out, norm2_g=norm2_g,
             w_ff1=w_ff1, w_ff2=w_ff2)
    depth = w_in.shape[0]
    batch, seq, d = x_prompt.shape
    nsamp = x_sample.shape[0]
    past = cache_k.shape[2]
    assert x_sample.shape[1] == 1 and past == WINDOW and seq % CHUNK == 0 and d == D_MODEL
    assert nsamp % SAMPLE_ROWS == 0
    nkv = ATT_HKV * ATT_HD

    xp = x_prompt.reshape(batch * seq, d)
    xs = x_sample.reshape(nsamp, d)
    rope_p = _rope_tables(jnp.arange(seq))
    rope_s = _rope_tables(PAST_LEN + jnp.arange(1))
    lane = jnp.arange(BRANCH_W) // ATT_HD
    e = (lane[:, None] == lane[None, :]).astype(BF16)

    wts, prm = _prepare(p)
    ck_all = cache_k.reshape(depth, nsamp, past, nkv)
    cv_all = cache_v.reshape(depth, nsamp, past, nkv)
    s0r_all = state_ssm_re.reshape(depth, nsamp, SSM_N)
    s0i_all = state_ssm_im.reshape(depth, nsamp, SSM_N)

    hp = _rmsnorm_call(xp, prm["norm1_g"][0], ROWS_PROMPT if xp.shape[0] % ROWS_PROMPT == 0 else xp.shape[0])
    hs = _rmsnorm_call(xs, prm["norm1_g"][0], nsamp)
    new_p = [[] for _ in range(7)]
    new_s = [[] for _ in range(7)]
    s_delta = None
    for l in range(depth):
        lp = {name: v[l] for name, v in prm.items()}
        st = (ck_all[l], cv_all[l], s0r_all[l], s0i_all[l], state_delta, s_delta, state_conv[l])
        g_next = prm["norm1_g"][l + 1] if l + 1 < depth else jnp.ones((1, d), F32)
        xp, hp, xs, hs, sp, ss = _layer(xp, hp, xs, hs, st, wts, l, lp, g_next, rope_p, rope_s, e, batch, seq)
        s_delta = ss[4]
        for i in range(7):
            new_p[i].append(sp[i])
            new_s[i].append(ss[i])

    def stack(parts, shape):
        return jnp.stack(parts, axis=0).reshape((depth,) + shape)

    keep = min(WINDOW, seq)
    return (
        xp.reshape(batch, seq, d), xs.reshape(nsamp, 1, d),
        stack(new_p[0], (batch, keep, ATT_HKV, ATT_HD)), stack(new_p[1], (batch, keep, ATT_HKV, ATT_HD)),
        stack(new_p[2], (batch, SSM_NG, SSM_P)), stack(new_p[3], (batch, SSM_NG, SSM_P)),
        stack(new_p[4], (batch, DN_H, DN_HD, DN_HD)), stack(new_p[5], (batch, DN_CONV - 1, W_DN_QKV)),
        stack(new_p[6], (batch, CHUNK, BRANCH_W)),
        stack(new_s[0], (nsamp, past, ATT_HKV, ATT_HD)), stack(new_s[1], (nsamp, past, ATT_HKV, ATT_HD)),
        stack(new_s[2], (nsamp, SSM_NG, SSM_P)), stack(new_s[3], (nsamp, SSM_NG, SSM_P)),
        s_delta, stack(new_s[5], (nsamp, DN_CONV - 1, W_DN_QKV)),
        stack(new_s[6], (nsamp, 1, BRANCH_W)),
    )
```
